```python
import math
import jax
import jax.numpy as jnp
from jax import lax
import numpy as np

D_MODEL = 1024
BATCH = 16
SEQ = 2048
DEPTH = 4
DEC_BATCH = 128
DEC_SEQ = 8
PAST_LEN = 8192
PAGE_SIZE = 128

D_MIX = D_MODEL
N_MIXERS = 4
W_GROUP = D_MIX // N_MIXERS
HEAD_DIM = 64
N_OUT_HEADS = D_MIX // HEAD_DIM
SSD_HEADS = W_GROUP // HEAD_DIM
SSD_P = HEAD_DIM
SSD_N = 128
SSD_GROUPS = 2
SSD_CHUNK = 128
CONV_W = 4
SSD_CONV_DIM = W_GROUP + 2 * SSD_GROUPS * SSD_N
LRU_W = W_GROUP
LRU_BLOCKS = W_GROUP // HEAD_DIM
LRU_BW = LRU_W // LRU_BLOCKS
LRU_C = 8.0
HG_HEADS = 4
HG_DK = W_GROUP // HG_HEADS
HG_DV = W_GROUP // HG_HEADS
HG_CHUNK = 64
MLA_HEADS = 4
MLA_NOPE = 64
MLA_ROPE = 32
MLA_V = W_GROUP // MLA_HEADS
Q_LORA = 256
KV_LORA = 128
ROPE_THETA = 10000.0
Q_BLOCK = 128
MLA_SCALE = 1.0 / math.sqrt(MLA_NOPE + MLA_ROPE)
D_FF = 2816
N_MOD = 9
EPS = 1e-6
IN_SSD = W_GROUP + SSD_CONV_DIM + SSD_HEADS
IN_LRU = 2 * LRU_W
IN_HG = 2 * HG_HEADS * HG_DK + 2 * HG_HEADS * HG_DV
IN_MLA = Q_LORA + KV_LORA + MLA_ROPE
OFF_LRU = IN_SSD
OFF_HG = OFF_LRU + IN_LRU
OFF_MLA = OFF_HG + IN_HG
N_IN = OFF_MLA + IN_MLA

kernel_name = 'hymba_ssd_rglru_hgrn2_mla_decoder_step'


def rms_norm(x, g):
    xf = x.astype(jnp.float32)
    y = xf * lax.rsqrt(jnp.mean(xf * xf, axis=-1, keepdims=True) + EPS)
    return (y * g.astype(jnp.float32)).astype(x.dtype)


def ada_modulation(c, w, b):
    m = jax.nn.silu(c) @ w + b
    return jnp.split(m[:, None, :], N_MOD, axis=-1)


def modulate(h, shift, scale):
    return h * (1.0 + scale) + shift


def swiglu(h, w_in, w_out):
    g, u = jnp.split(h @ w_in, 2, axis=-1)
    return (jax.nn.silu(g) * u) @ w_out


def causal_conv(u, buf, w, b):
    ext = jnp.concatenate([buf.astype(u.dtype), u], axis=1)
    L = u.shape[1]
    y = sum((ext[:, k:k + L] * w[k] for k in range(CONV_W)), b)
    return y, ext[:, -(CONV_W - 1):]


def rope(x, pos):
    half = MLA_ROPE // 2
    inv_freq = ROPE_THETA ** (-jnp.arange(half, dtype=jnp.float32) / half)
    ang = pos.astype(jnp.float32)[:, None] * inv_freq
    shape = (ang.shape[0],) + (1,) * (x.ndim - 3) + (half,)
    cos = jnp.cos(ang).reshape(shape)
    sin = jnp.sin(ang).reshape(shape)
    xf = x.astype(jnp.float32)
    x1, x2 = xf[..., :half], xf[..., half:]
    return jnp.concatenate([x1 * cos - x2 * sin, x1 * sin + x2 * cos], axis=-1).astype(x.dtype)


def ssd_chunk_scan(x, dt, a, bm, cm, h0):
    b, L = x.shape[:2]
    cs = min(SSD_CHUNK, L)
    pad = (-L) % cs
    if pad:
        padf = lambda t: jnp.pad(t, [(0, 0), (0, pad)] + [(0, 0)] * (t.ndim - 2))
        x, dt, bm, cm = padf(x), padf(dt), padf(bm), padf(cm)
    nc = (L + pad) // cs
    chunk = lambda t: t.reshape((b, nc, cs) + t.shape[2:])
    x, dt, bm, cm = chunk(x), chunk(dt), chunk(bm), chunk(cm)
    a_cum = jnp.cumsum(dt * a, axis=2)
    xdt = x * dt[..., None]
    causal = jnp.tril(jnp.ones((cs, cs), dtype=bool))[:, :, None, None]
    decay = jnp.exp(jnp.where(causal, a_cum[:, :, :, None] - a_cum[:, :, None, :], -jnp.inf))
    cb = jnp.einsum('bclgn,bcsgn->bclsg', cm, bm)
    y_diag = jnp.einsum('bclsgh,bcsghp->bclghp', cb[..., None] * decay, xdt)
    decay_end = jnp.exp(a_cum[:, :, -1:] - a_cum)
    states = jnp.einsum('bclgn,bclgh,bclghp->bcghpn', bm, decay_end, xdt)
    chunk_decay = jnp.exp(a_cum[:, :, -1])

    def step(h, inp):
        st, dec = inp
        return dec[..., None, None] * h + st, h

    h_last, h_starts = lax.scan(step, h0, (jnp.moveaxis(states, 1, 0), jnp.moveaxis(chunk_decay, 1, 0)))
    h_starts = jnp.moveaxis(h_starts, 0, 1)
    y_off = jnp.einsum('bclgn,bcghpn,bclgh->bclghp', cm, h_starts, jnp.exp(a_cum))
    y = (y_diag + y_off).reshape((b, nc * cs) + x.shape[3:])[:, :L]
    return y, h_last


def ssd_mixer(u, conv_buf, h0, conv_w, conv_b, dt_bias, a_log, d_skip):
    b, L = u.shape[:2]
    hg = SSD_HEADS // SSD_GROUPS
    z, xbc, dt_raw = jnp.split(u, [W_GROUP, W_GROUP + SSD_CONV_DIM], axis=-1)
    xbc, conv_new = causal_conv(xbc, conv_buf, conv_w, conv_b)
    xbc = jax.nn.silu(xbc).astype(jnp.float32)
    xs = xbc[..., :W_GROUP].reshape(b, L, SSD_GROUPS, hg, SSD_P)
    bm = xbc[..., W_GROUP:W_GROUP + SSD_GROUPS * SSD_N].reshape(b, L, SSD_GROUPS, SSD_N)
    cm = xbc[..., W_GROUP + SSD_GROUPS * SSD_N:].reshape(b, L, SSD_GROUPS, SSD_N)
    dt = jax.nn.softplus(dt_raw.astype(jnp.float32) + dt_bias.astype(jnp.float32)).reshape(b, L, SSD_GROUPS, hg)
    a = -jnp.exp(a_log.astype(jnp.float32)).reshape(SSD_GROUPS, hg)
    h0 = h0.astype(jnp.float32).reshape(b, SSD_GROUPS, hg, SSD_P, SSD_N)
    y, h_new = ssd_chunk_scan(xs, dt, a, bm, cm, h0)
    y = y + d_skip.astype(jnp.float32).reshape(SSD_GROUPS, hg, 1) * xs
    y = y.reshape(b, L, W_GROUP).astype(u.dtype) * jax.nn.silu(z)
    return y, conv_new, h_new.reshape(b, SSD_HEADS, SSD_P, SSD_N)


def rglru_mixer(u, conv_buf, h0, conv_w, conv_b, w_r, b_r, w_i, b_i, lam):
    xb, gate = jnp.split(u, 2, axis=-1)
    xb, conv_new = causal_conv(xb, conv_buf, conv_w, conv_b)
    b, L = xb.shape[:2]
    xh = xb.reshape(b, L, LRU_BLOCKS, LRU_BW)
    r = jax.nn.sigmoid(jnp.einsum('blhi,hij->blhj', xh, w_r).reshape(b, L, LRU_W) + b_r).astype(jnp.float32)
    i = jax.nn.sigmoid(jnp.einsum('blhi,hij->blhj', xh, w_i).reshape(b, L, LRU_W) + b_i)
    log_a = -LRU_C * r * jax.nn.softplus(-lam.astype(jnp.float32))
    a = jnp.exp(log_a)
    bterm = jnp.sqrt(-jnp.expm1(2.0 * log_a)) * (i * xb).astype(jnp.float32)
    bterm = bterm.at[:, 0].add(a[:, 0] * h0.astype(jnp.float32))

    def combine(e1, e2):
        a1, b1 = e1
        a2, b2 = e2
        return a1 * a2, a2 * b1 + b2

    _, h = lax.associative_scan(combine, (a, bterm), axis=1)
    y = h.astype(u.dtype) * jax.nn.gelu(gate)
    return y, conv_new, h[:, -1]


def hgrn2_chunk_scan(q, log_f, k, v, s0):
    b, L = q.shape[:2]
    cs = min(HG_CHUNK, L)
    pad = (-L) % cs
    if pad:
        q, log_f, k, v = [jnp.pad(t, [(0, 0), (0, pad), (0, 0), (0, 0)]) for t in (q, log_f, k, v)]
    nc = (L + pad) // cs

    def to_chunks(t):
        return jnp.moveaxis(t.reshape(b, nc, cs, t.shape[2], t.shape[3]), 1, 0)

    causal = jnp.tril(jnp.ones((cs, cs), dtype=bool))[:, :, None, None]

    def step(s, inp):
        qc, lfc, kc, vc = inp
        g = jnp.cumsum(lfc, axis=1)
        decay = jnp.exp(jnp.where(causal, g[:, :, None] - g[:, None, :], -jnp.inf))
        scores = jnp.einsum('btshk,bshk->btsh', qc[:, :, None] * decay, kc)
        o = jnp.einsum('btsh,bshv->bthv', scores, vc) + jnp.einsum('bthk,bhkv->bthv', qc * jnp.exp(g), s)
        g_end = g[:, -1]
        s = jnp.exp(g_end)[..., None] * s + jnp.einsum('bshk,bshv->bhkv', kc * jnp.exp(g_end[:, None] - g), vc)
        return s, o

    s_last, o = lax.scan(step, s0, (to_chunks(q), to_chunks(log_f), to_chunks(k), to_chunks(v)))
    o = jnp.moveaxis(o, 0, 1).reshape(b, nc * cs, HG_HEADS, HG_DV)[:, :L]
    return o, s_last


def hgrn2_mixer(u, s0, lb):
    b, L = u.shape[:2]
    nk = HG_HEADS * HG_DK
    q, fz, iv, g = jnp.split(u, [nk, 2 * nk, 2 * nk + HG_HEADS * HG_DV], axis=-1)
    fz = fz.astype(jnp.float32)
    lb = lb.astype(jnp.float32)
    log_f = jnp.logaddexp(jnp.log(lb), jnp.log1p(-lb) + jax.nn.log_sigmoid(fz))
    k = (1.0 - lb) * jax.nn.sigmoid(-fz)
    shp = (b, L, HG_HEADS, HG_DK)
    o, s_new = hgrn2_chunk_scan(jax.nn.silu(q).astype(jnp.float32).reshape(shp), log_f.reshape(shp), k.reshape(shp),
                                iv.astype(jnp.float32).reshape(b, L, HG_HEADS, HG_DV), s0.astype(jnp.float32))
    o = o.reshape(b, L, HG_HEADS * HG_DV).astype(u.dtype) * jax.nn.silu(g)
    return o, s_new


def mla_project(u, pos, q_norm, w_uq, kv_norm, qn_nope, qn_rope, kn_rope):
    b, L = u.shape[:2]
    cq, ckv, kr = jnp.split(u, [Q_LORA, Q_LORA + KV_LORA], axis=-1)
    q = (rms_norm(cq, q_norm) @ w_uq).reshape(b, L, MLA_HEADS, MLA_NOPE + MLA_ROPE)
    q_nope = rms_norm(q[..., :MLA_NOPE], qn_nope)
    q_rope = rope(rms_norm(q[..., MLA_NOPE:], qn_rope), pos)
    latent = rms_norm(ckv, kv_norm)
    k_rope = rope(rms_norm(kr, kn_rope), pos)
    return q_nope, q_rope, latent, k_rope


def mla_expand(latent, w_ukv, kn_nope):
    kv = (latent @ w_ukv).reshape(latent.shape[:-1] + (MLA_HEADS, MLA_NOPE + MLA_V))
    return rms_norm(kv[..., :MLA_NOPE], kn_nope), kv[..., MLA_NOPE:]


def mla_scores(q_nope, q_rope, k_nope, k_rope):
    s = jnp.einsum('bthd,bshd->bhts', q_nope, k_nope) + jnp.einsum('bthr,bsr->bhts', q_rope, k_rope)
    return s.astype(jnp.float32) * MLA_SCALE


def mla_prompt_attention(q_nope, q_rope, k_nope, v, k_rope):
    b, L = q_nope.shape[:2]
    qb = Q_BLOCK if L % Q_BLOCK == 0 else L
    nq = L // qb
    qn = jnp.moveaxis(q_nope.reshape(b, nq, qb, MLA_HEADS, MLA_NOPE), 1, 0)
    qr = jnp.moveaxis(q_rope.reshape(b, nq, qb, MLA_HEADS, MLA_ROPE), 1, 0)
    kpos = jnp.arange(L)

    def block(inp):
        j, qn_j, qr_j = inp
        s = mla_scores(qn_j, qr_j, k_nope, k_rope)
        qpos = j * qb + jnp.arange(qb)
        s = jnp.where(kpos[None, :] <= qpos[:, None], s, -jnp.inf)
        p = jax.nn.softmax(s, axis=-1)
        return jnp.einsum('bhts,bshv->bthv', p.astype(v.dtype), v)

    o = lax.map(block, (jnp.arange(nq), qn, qr))
    return jnp.moveaxis(o, 0, 1).reshape(b, L, MLA_HEADS * MLA_V)


def online_update(carry, s, v):
    m, l, acc = carry
    m_new = jnp.maximum(m, s.max(axis=-1))
    alpha = jnp.exp(m - m_new)
    p = jnp.exp(s - m_new[..., None])
    l = alpha * l + p.sum(axis=-1)
    acc = alpha[..., None] * acc + jnp.einsum('bhts,bshv->bhtv', p, v.astype(jnp.float32))
    return m_new, l, acc


def mla_sample_attention(q_nope, q_rope, k_nope_new, v_new, k_rope_new,
                         cache_lat, cache_kr, page_table, layer, w_ukv, kn_nope):
    b, T = q_nope.shape[:2]
    carry = (jnp.full((b, MLA_HEADS, T), -jnp.inf, jnp.float32),
             jnp.zeros((b, MLA_HEADS, T), jnp.float32),
             jnp.zeros((b, MLA_HEADS, T, MLA_V), jnp.float32))

    def page_step(carry, phys):
        k_nope, v = mla_expand(cache_lat[layer, phys], w_ukv, kn_nope)
        s = mla_scores(q_nope, q_rope, k_nope, cache_kr[layer, phys])
        return online_update(carry, s, v), None

    carry, _ = lax.scan(page_step, carry, page_table.T)
    s = mla_scores(q_nope, q_rope, k_nope_new, k_rope_new)
    s = jnp.where(jnp.tril(jnp.ones((T, T), dtype=bool)), s, -jnp.inf)
    _, l, acc = online_update(carry, s, v_new)
    o = acc / l[..., None]
    return jnp.swapaxes(o, 1, 2).reshape(b, T, MLA_HEADS * MLA_V)


def layer_forward(x, c, pos, w, lb, conv_a, ssm_a, conv_b, h_b, s_c, paged):
    sh1, sc1, g1, sh2, sc2, g2, sh3, sc3, g3 = ada_modulation(c, w['w_ada'], w['b_ada'])
    h = modulate(rms_norm(x, w['norm_ffn1']), sh1, sc1)
    x = x + 0.5 * g1 * swiglu(h, w['w_ffn1_in'], w['w_ffn1_out'])
    h = modulate(rms_norm(x, w['norm_mix']), sh2, sc2)
    u = h @ w['w_in']
    u_a, u_b, u_c, u_d = jnp.split(u, [OFF_LRU, OFF_HG, OFF_MLA], axis=-1)
    y_a, conv_a, ssm_a = ssd_mixer(u_a, conv_a, ssm_a, w['ssd_conv_w'], w['ssd_conv_b'],
                                   w['ssd_dt_bias'], w['ssd_a_log'], w['ssd_d'])
    y_b, conv_b, h_b = rglru_mixer(u_b, conv_b, h_b, w['lru_conv_w'], w['lru_conv_b'], w['lru_w_r'],
                                   w['lru_b_r'], w['lru_w_i'], w['lru_b_i'], w['lru_lambda'])
    y_c, s_c = hgrn2_mixer(u_c, s_c, lb)
    q_nope, q_rope, lat, k_rope = mla_project(u_d, pos, w['mla_q_norm'], w['mla_w_uq'], w['mla_kv_norm'],
                                              w['mla_qn_nope'], w['mla_qn_rope'], w['mla_kn_rope'])
    k_nope, v = mla_expand(lat, w['mla_w_ukv'], w['mla_kn_nope'])
    if paged is None:
        y_d = mla_prompt_attention(q_nope, q_rope, k_nope, v, k_rope)
    else:
        cache_lat, cache_kr, page_table, layer = paged
        y_d = mla_sample_attention(q_nope, q_rope, k_nope, v, k_rope, cache_lat, cache_kr, page_table, layer,
                                   w['mla_w_ukv'], w['mla_kn_nope'])
    b, L = x.shape[:2]
    y = jnp.concatenate([y_a, y_b, y_c, y_d.astype(x.dtype)], axis=-1).reshape(b, L, N_OUT_HEADS, HEAD_DIM)
    y = rms_norm(y, w['out_norm'].reshape(N_OUT_HEADS, HEAD_DIM)).reshape(b, L, D_MIX)
    x = x + g2 * (y @ w['w_out'])
    h = modulate(rms_norm(x, w['norm_ffn2']), sh3, sc3)
    x = x + 0.5 * g3 * swiglu(h, w['w_ffn2_in'], w['w_ffn2_out'])
    return x, (lat, k_rope, ssm_a, conv_a, h_b, conv_b, s_c)


def setup_inputs(seed: int = 0) -> dict:
    key = jax.random.key(seed)
    ks = iter(jax.random.split(key, 64))
    nrm = lambda shape, scale=1.0: scale * jax.random.normal(next(ks), shape, jnp.float32)
    gain = lambda shape: 1.0 + nrm(shape, 0.02)
    n_pages = PAST_LEN // PAGE_SIZE
    n_pool = (DEC_BATCH * n_pages * 5) // 4
    page_table = jax.random.permutation(next(ks), n_pool)[:DEC_BATCH * n_pages].reshape(DEC_BATCH, n_pages).astype(jnp.int32)
    dt0 = jnp.exp(jax.random.uniform(next(ks), (DEPTH, SSD_HEADS), jnp.float32, math.log(1e-3), math.log(1e-1)))
    dt_bias = dt0 + jnp.log(-jnp.expm1(-dt0))
    a_log = jnp.log(jax.random.uniform(next(ks), (DEPTH, SSD_HEADS), jnp.float32, 1.0, 16.0))
    a0 = jax.random.uniform(next(ks), (DEPTH, LRU_W), jnp.float32, 0.9, 0.999)
    lam = jnp.log(a0) - jnp.log1p(-a0)
    return {
        'x_prompt': nrm((BATCH, SEQ, D_MODEL)),
        'x_sample': nrm((DEC_BATCH, DEC_SEQ, D_MODEL)),
        'cache_mla_latent': nrm((DEPTH, n_pool, PAGE_SIZE, KV_LORA)),
        'cache_mla_krope': nrm((DEPTH, n_pool, PAGE_SIZE, MLA_ROPE)),
        'state_ssd': nrm((DEPTH, DEC_BATCH, SSD_HEADS, SSD_P, SSD_N), 0.3),
        'state_ssd_conv': nrm((DEPTH, DEC_BATCH, CONV_W - 1, SSD_CONV_DIM)),
        'state_lru': nrm((DEPTH, DEC_BATCH, LRU_W), 0.5),
        'state_lru_conv': nrm((DEPTH, DEC_BATCH, CONV_W - 1, LRU_W)),
        'state_hgrn': nrm((DEPTH, DEC_BATCH, HG_HEADS, HG_DK, HG_DV), 0.3),
        'page_table': page_table,
        'c_prompt': nrm((BATCH, D_MODEL)),
        'c_sample': nrm((DEC_BATCH, D_MODEL)),
        'w_ada': nrm((DEPTH, D_MODEL, N_MOD * D_MODEL), 0.5 * D_MODEL ** -0.5),
        'b_ada': nrm((DEPTH, N_MOD * D_MODEL), 0.02),
        'norm_ffn1': gain((DEPTH, D_MODEL)),
        'w_ffn1_in': nrm((DEPTH, D_MODEL, 2 * D_FF), D_MODEL ** -0.5),
        'w_ffn1_out': nrm((DEPTH, D_FF, D_MODEL), D_FF ** -0.5),
        'norm_mix': gain((DEPTH, D_MODEL)),
        'w_in': nrm((DEPTH, D_MODEL, N_IN), D_MODEL ** -0.5),
        'ssd_conv_w': nrm((DEPTH, CONV_W, SSD_CONV_DIM), CONV_W ** -0.5),
        'ssd_conv_b': nrm((DEPTH, SSD_CONV_DIM), 0.02),
        'ssd_dt_bias': dt_bias,
        'ssd_a_log': a_log,
        'ssd_d': gain((DEPTH, SSD_HEADS)),
        'lru_conv_w': nrm((DEPTH, CONV_W, LRU_W), CONV_W ** -0.5),
        'lru_conv_b': nrm((DEPTH, LRU_W), 0.02),
        'lru_w_r': nrm((DEPTH, LRU_BLOCKS, LRU_BW, LRU_BW), LRU_BW ** -0.5),
        'lru_b_r': nrm((DEPTH, LRU_W), 0.02),
        'lru_w_i': nrm((DEPTH, LRU_BLOCKS, LRU_BW, LRU_BW), LRU_BW ** -0.5),
        'lru_b_i': nrm((DEPTH, LRU_W), 0.02),
        'lru_lambda': lam,
        'hgrn_lb': nrm((DEPTH, HG_HEADS * HG_DK), 0.5),
        'mla_q_norm': gain((DEPTH, Q_LORA)),
        'mla_w_uq': nrm((DEPTH, Q_LORA, MLA_HEADS * (MLA_NOPE + MLA_ROPE)), Q_LORA ** -0.5),
        'mla_kv_norm': gain((DEPTH, KV_LORA)),
        'mla_w_ukv': nrm((DEPTH, KV_LORA, MLA_HEADS * (MLA_NOPE + MLA_V)), KV_LORA ** -0.5),
        'mla_qn_nope': gain((DEPTH, MLA_NOPE)),
        'mla_qn_rope': gain((DEPTH, MLA_ROPE)),
        'mla_kn_nope': gain((DEPTH, MLA_NOPE)),
        'mla_kn_rope': gain((DEPTH, MLA_ROPE)),
        'out_norm': gain((DEPTH, D_MIX)),
        'w_out': nrm((DEPTH, D_MIX, D_MODEL), D_MIX ** -0.5),
        'norm_ffn2': gain((DEPTH, D_MODEL)),
        'w_ffn2_in': nrm((DEPTH, D_MODEL, 2 * D_FF), D_MODEL ** -0.5),
        'w_ffn2_out': nrm((DEPTH, D_FF, D_MODEL), D_FF ** -0.5),
    }


def reference(x_prompt, x_sample, cache_mla_latent, cache_mla_krope, state_ssd, state_ssd_conv, state_lru,
              state_lru_conv, state_hgrn, page_table, c_prompt, c_sample, w_ada, b_ada, norm_ffn1, w_ffn1_in,
              w_ffn1_out, norm_mix, w_in, ssd_conv_w, ssd_conv_b, ssd_dt_bias, ssd_a_log, ssd_d, lru_conv_w,
              lru_conv_b, lru_w_r, lru_b_r, lru_w_i, lru_b_i, lru_lambda, hgrn_lb, mla_q_norm, mla_w_uq,
              mla_kv_norm, mla_w_ukv, mla_qn_nope, mla_qn_rope, mla_kn_nope, mla_kn_rope, out_norm, w_out,
              norm_ffn2, w_ffn2_in, w_ffn2_out):
    params = dict(w_ada=w_ada, b_ada=b_ada, norm_ffn1=norm_ffn1, w_ffn1_in=w_ffn1_in, w_ffn1_out=w_ffn1_out,
                  norm_mix=norm_mix, w_in=w_in, ssd_conv_w=ssd_conv_w, ssd_conv_b=ssd_conv_b,
                  ssd_dt_bias=ssd_dt_bias, ssd_a_log=ssd_a_log, ssd_d=ssd_d, lru_conv_w=lru_conv_w,
                  lru_conv_b=lru_conv_b, lru_w_r=lru_w_r, lru_b_r=lru_b_r, lru_w_i=lru_w_i, lru_b_i=lru_b_i,
                  lru_lambda=lru_lambda, mla_q_norm=mla_q_norm, mla_w_uq=mla_w_uq, mla_kv_norm=mla_kv_norm,
                  mla_w_ukv=mla_w_ukv, mla_qn_nope=mla_qn_nope, mla_qn_rope=mla_qn_rope,
                  mla_kn_nope=mla_kn_nope, mla_kn_rope=mla_kn_rope, out_norm=out_norm, w_out=w_out,
                  norm_ffn2=norm_ffn2, w_ffn2_in=w_ffn2_in, w_ffn2_out=w_ffn2_out)
    lb_cum = jnp.cumsum(jax.nn.softmax(hgrn_lb.astype(jnp.float32), axis=0), axis=0)
    lower_bounds = lb_cum - lb_cum[0:1]
    bp, lp = x_prompt.shape[:2]
    pos_p = jnp.arange(lp)
    pos_s = PAST_LEN + jnp.arange(x_sample.shape[1])
    dt = x_prompt.dtype
    z_conv_a = jnp.zeros((bp, CONV_W - 1, SSD_CONV_DIM), dt)
    z_ssm = jnp.zeros((bp, SSD_HEADS, SSD_P, SSD_N), jnp.float32)
    z_conv_b = jnp.zeros((bp, CONV_W - 1, LRU_W), dt)
    z_h = jnp.zeros((bp, LRU_W), jnp.float32)
    z_s = jnp.zeros((bp, HG_HEADS, HG_DK, HG_DV), jnp.float32)
    yp, ys = x_prompt, x_sample
    new_p = [[] for _ in range(7)]
    new_s = [[] for _ in range(7)]
    for l in range(DEPTH):
        w = {name: arr[l] for name, arr in params.items()}
        yp, st_p = layer_forward(yp, c_prompt, pos_p, w, lower_bounds[l], z_conv_a, z_ssm, z_conv_b, z_h, z_s, None)
        ys, st_s = layer_forward(ys, c_sample, pos_s, w, lower_bounds[l], state_ssd_conv[l], state_ssd[l],
                                 state_lru_conv[l], state_lru[l], state_hgrn[l],
                                 (cache_mla_latent, cache_mla_krope, page_table, l))
        for i in range(7):
            new_p[i].append(st_p[i])
            new_s[i].append(st_s[i])
    stk = lambda rows: jnp.stack(rows).astype(dt)
    lat_p, kr_p, ssd_p, ssdc_p, lru_p, lruc_p, hg_p = [stk(r) for r in new_p]
    lat_s, kr_s, ssd_s, ssdc_s, lru_s, lruc_s, hg_s = [stk(r) for r in new_s]
    return (yp, ys, lat_p, lat_s, kr_p, kr_s, ssd_p, ssd_s, ssdc_p, ssdc_s, lru_p, lru_s, lruc_p, lruc_s, hg_p, hg_s)
```

```python
import functools
import math

import jax
import jax.numpy as jnp
from jax import lax
from jax.experimental import pallas as pl
from jax.experimental.pallas import tpu as pltpu

F32 = jnp.float32
BF16 = jnp.bfloat16

D_MODEL = 1024
DEPTH = 4
PAST_LEN = 8192
PAGE_SIZE = 128
W_GROUP = 256
HEAD_DIM = 64
SSD_HEADS = 4
SSD_P = 64
SSD_N = 128
SSD_GROUPS = 2
SSD_CHUNK = 128
CONV_W = 4
SSD_CONV_DIM = W_GROUP + 2 * SSD_GROUPS * SSD_N
LRU_W = W_GROUP
LRU_BLOCKS = 4
LRU_BW = LRU_W // LRU_BLOCKS
LRU_C = 8.0
HG_HEADS = 4
HG_DK = 64
HG_DV = 64
HG_CHUNK = 64
MLA_HEADS = 4
MLA_NOPE = 64
MLA_ROPE = 32
MLA_V = 64
Q_LORA = 256
KV_LORA = 128
ROPE_THETA = 10000.0
MLA_SCALE = 1.0 / math.sqrt(MLA_NOPE + MLA_ROPE)
D_FF = 2816
N_MOD = 9
EPS = 1e-6
IN_SSD = W_GROUP + SSD_CONV_DIM + SSD_HEADS
IN_LRU = 2 * LRU_W
IN_HG = 4 * HG_HEADS * HG_DK
IN_MLA = Q_LORA + KV_LORA + MLA_ROPE
OFF_LRU = IN_SSD
OFF_HG = OFF_LRU + IN_LRU
OFF_MLA = OFF_HG + IN_HG

LANES = 128
SUBLANES = 8
VMEM_BYTES_V7X = 64 * 1024 * 1024
VMEM_LIMIT = VMEM_BYTES_V7X * 3 // 4

PROJ_SSD = W_GROUP + SSD_CONV_DIM + LANES
PROJ_LRU = IN_LRU
PROJ_HG = IN_HG
PROJ_MLA = Q_LORA + KV_LORA + LANES


def _cparams(*sem):
    return pltpu.CompilerParams(dimension_semantics=sem, vmem_limit_bytes=VMEM_LIMIT)


def _silu(x):
    return x * jax.nn.sigmoid(x)


def _rms(x, w):
    return x * lax.rsqrt(jnp.mean(x * x, axis=-1, keepdims=True) + EPS) * w


def _ada_body(c_ref, w_ref, b_ref, o_ref):
    a = _silu(c_ref[...]).astype(BF16)
    o_ref[...] = jnp.dot(a, w_ref[...].astype(BF16), preferred_element_type=F32) + b_ref[...]


def ada_modulation(c_all, w_ada, b_ada, *, tn=1152):
    r, d = c_all.shape
    depth, _, n = w_ada.shape
    return pl.pallas_call(
        _ada_body,
        out_shape=jax.ShapeDtypeStruct((depth, r, n), F32),
        grid=(depth, n // tn),
        in_specs=[
            pl.BlockSpec((r, d), lambda l, j: (0, 0)),
            pl.BlockSpec((None, d, tn), lambda l, j: (l, 0, j)),
            pl.BlockSpec((None, 1, tn), lambda l, j: (l, 0, j)),
        ],
        out_specs=pl.BlockSpec((None, r, tn), lambda l, j: (l, 0, j)),
        compiler_params=_cparams("arbitrary", "arbitrary"),
        name="ada_modulation",
    )(c_all, w_ada, b_ada.reshape(depth, 1, n))


class _Tiling:
    def __init__(self, b, l, rows):
        if l >= rows:
            assert l % rows == 0
            self.gb, self.r = 1, rows
        else:
            gb = min(b, max(1, rows // l))
            assert b % gb == 0
            self.gb, self.r = gb, l
        self.b, self.l = b, l
        self.nb = b // self.gb
        self.nl = l // self.r
        self.steps = self.nb * self.nl
        self.m = self.gb * self.r

    def tok_index(self, i):
        return (i // self.nl, i % self.nl)


def _mod_spec(t, layer, row0, k):
    assert row0 % t.gb == 0
    return pl.BlockSpec((None, t.gb, 1, D_MODEL), lambda i, *_: (layer, row0 // t.gb + i // t.nl, 0, k))


def _ffn_body(x_ref, sh_ref, sc_ref, gt_ref, nw_ref, win_ref, wout_ref, o_ref, h_ref, acc_ref, *, tf, nj):
    j = pl.program_id(1)
    gb, r, d = x_ref.shape

    @pl.when(j == 0)
    def _():
        h = _rms(x_ref[...], nw_ref[...]) * (1.0 + sc_ref[...]) + sh_ref[...]
        h_ref[...] = h.reshape(gb * r, d).astype(BF16)
        acc_ref[...] = jnp.zeros_like(acc_ref)

    gu = jnp.dot(h_ref[...], win_ref[...], preferred_element_type=F32)
    a = _silu(gu[:, :tf]) * gu[:, tf:]
    acc_ref[...] += jnp.dot(a.astype(BF16), wout_ref[...], preferred_element_type=F32)

    @pl.when(j == nj - 1)
    def _():
        o_ref[...] = x_ref[...] + 0.5 * gt_ref[...] * acc_ref[...].reshape(gb, r, d)


def ffn_halfstep(x, mods, layer, row0, k0, norm_w, w_in_packed, w_out, *, rows=1024, tf=256):
    b, l, d = x.shape
    t = _Tiling(b, l, rows)
    nj = D_FF // tf
    xspec = pl.BlockSpec((t.gb, t.r, d), lambda i, j: (*t.tok_index(i), 0))
    return pl.pallas_call(
        functools.partial(_ffn_body, tf=tf, nj=nj),
        out_shape=jax.ShapeDtypeStruct(x.shape, F32),
        grid=(t.steps, nj),
        in_specs=[
            xspec,
            _mod_spec(t, layer, row0, k0),
            _mod_spec(t, layer, row0, k0 + 1),
            _mod_spec(t, layer, row0, k0 + 2),
            pl.BlockSpec((1, d), lambda i, j: (0, 0)),
            pl.BlockSpec((d, 2 * tf), lambda i, j: (0, j)),
            pl.BlockSpec((tf, d), lambda i, j: (j, 0)),
        ],
        out_specs=xspec,
        scratch_shapes=[pltpu.VMEM((t.m, d), BF16), pltpu.VMEM((t.m, d), F32)],
        compiler_params=_cparams("arbitrary", "arbitrary"),
        name="ffn_halfstep",
    )(x, mods, mods, mods, norm_w.reshape(1, d), w_in_packed, w_out)


def pack_ffn_in(w_in, tf=256):
    d = w_in.shape[0]
    g = w_in[:, :D_FF].reshape(d, D_FF // tf, tf)
    u = w_in[:, D_FF:].reshape(d, D_FF // tf, tf)
    return jnp.concatenate([g, u], axis=-1).reshape(d, 2 * D_FF).astype(BF16)


def _proj_body(x_ref, sh_ref, sc_ref, nw_ref, w_ref, ssd_ref, lru_ref, hg_ref, mla_ref):
    gb, r, d = x_ref.shape
    h = _rms(x_ref[...], nw_ref[...]) * (1.0 + sc_ref[...]) + sh_ref[...]
    h = h.reshape(gb * r, d).astype(BF16)
    off = 0
    for o_ref in (ssd_ref, lru_ref, hg_ref, mla_ref):
        n = o_ref.shape[-1]
        u = jnp.dot(h, w_ref[:, off:off + n], preferred_element_type=F32)
        o_ref[...] = u.reshape(gb, r, n)
        off += n


def mixer_projection(x, mods, layer, row0, norm_w, w_packed, *, rows=512):
    b, l, d = x.shape
    t = _Tiling(b, l, rows)
    widths = (PROJ_SSD, PROJ_LRU, PROJ_HG, PROJ_MLA)
    xspec = pl.BlockSpec((t.gb, t.r, d), lambda i: (*t.tok_index(i), 0))
    return pl.pallas_call(
        _proj_body,
        out_shape=[jax.ShapeDtypeStruct((b, l, n), F32) for n in widths],
        grid=(t.steps,),
        in_specs=[
            xspec,
            _mod_spec(t, layer, row0, 3),
            _mod_spec(t, layer, row0, 4),
            pl.BlockSpec((1, d), lambda i: (0, 0)),
            pl.BlockSpec((d, sum(widths)), lambda i: (0, 0)),
        ],
        out_specs=[pl.BlockSpec((t.gb, t.r, n), lambda i: (*t.tok_index(i), 0)) for n in widths],
        compiler_params=_cparams("arbitrary"),
        name="mixer_projection",
    )(x, mods, mods, norm_w.reshape(1, d), w_packed)


def pack_proj_in(w_in):
    d = w_in.shape[0]
    z = lambda n: jnp.zeros((d, n), w_in.dtype)
    cols = [
        w_in[:, :W_GROUP + SSD_CONV_DIM], w_in[:, W_GROUP + SSD_CONV_DIM:IN_SSD], z(LANES - SSD_HEADS),
        w_in[:, OFF_LRU:OFF_HG],
        w_in[:, OFF_HG:OFF_MLA],
        w_in[:, OFF_MLA:], z(LANES - MLA_ROPE),
    ]
    return jnp.concatenate(cols, axis=1).astype(BF16)


def _head_ones(n):
    i = lax.broadcasted_iota(jnp.int32, (n, n), 0) // HEAD_DIM
    j = lax.broadcasted_iota(jnp.int32, (n, n), 1) // HEAD_DIM
    return (i == j).astype(BF16)


def _out_body(x_ref, ya_ref, yb_ref, yc_ref, yd_ref, gt_ref, nw_ref, w_ref, o_ref):
    gb, r, d = x_ref.shape
    ones = _head_ones(W_GROUP)
    acc = jnp.zeros((gb * r, d), F32)
    for k, y_ref in enumerate((ya_ref, yb_ref, yc_ref, yd_ref)):
        y = y_ref[...].reshape(gb * r, W_GROUP)
        ss = jnp.dot((y * y).astype(BF16), ones, preferred_element_type=F32)
        yn = y * lax.rsqrt(ss * (1.0 / HEAD_DIM) + EPS) * nw_ref[:, k * W_GROUP:(k + 1) * W_GROUP]
        acc += jnp.dot(yn.astype(BF16), w_ref[k * W_GROUP:(k + 1) * W_GROUP, :], preferred_element_type=F32)
    o_ref[...] = x_ref[...] + gt_ref[...] * acc.reshape(gb, r, d)


def output_merge(x, ys, mods, layer, row0, out_norm, w_out, *, rows=512):
    b, l, d = x.shape
    t = _Tiling(b, l, rows)
    xspec = pl.BlockSpec((t.gb, t.r, d), lambda i: (*t.tok_index(i), 0))
    yspec = pl.BlockSpec((t.gb, t.r, W_GROUP), lambda i: (*t.tok_index(i), 0))
    return pl.pallas_call(
        _out_body,
        out_shape=jax.ShapeDtypeStruct(x.shape, F32),
        grid=(t.steps,),
        in_specs=[xspec, yspec, yspec, yspec, yspec,
                  _mod_spec(t, layer, row0, 5),
                  pl.BlockSpec((1, d), lambda i: (0, 0)),
                  pl.BlockSpec((d, d), lambda i: (0, 0))],
        out_specs=xspec,
        compiler_params=_cparams("arbitrary"),
        name="output_merge",
    )(x, *ys, mods, out_norm.reshape(1, d), w_out)


def _cumsum_rows(x):
    n = x.shape[0]
    row = lax.broadcasted_iota(jnp.int32, x.shape, 0)
    d = 1
    while d < n:
        x = x + jnp.where(row >= d, pltpu.roll(x, d, 0), 0.0)
        d *= 2
    return x


def _dot_nt(a, b):
    return lax.dot_general(a, b, (((1,), (1,)), ((), ())), preferred_element_type=F32)


def _dot_tn(a, b):
    return lax.dot_general(a, b, (((0,), (0,)), ((), ())), preferred_element_type=F32)


def _causal_conv(ext_ref, s, x, cw_ref, cb_ref, first, q):
    @pl.when(jnp.logical_not(first))
    def _():
        ext_ref[s, pl.ds(0, SUBLANES), :] = ext_ref[s, pl.ds(q, SUBLANES), :]

    ext_ref[s, pl.ds(SUBLANES, q), :] = x
    out = cb_ref[...]
    for k in range(CONV_W):
        out = out + cw_ref[k:k + 1, :] * ext_ref[s, pl.ds(SUBLANES - (CONV_W - 1) + k, q), :]
    return out


def _ssd_body(u_ref, cs_ref, st_ref, cw_ref, cb_ref, dtb_ref, alog_ref, dsk_ref,
              y_ref, cnew_ref, snew_ref, ext_ref, h_ref, *, q, nc):
    c = pl.program_id(1)
    gb = u_ref.shape[0]
    first = c == 0
    hp = SSD_HEADS // SSD_GROUPS * SSD_P

    @pl.when(first)
    def _():
        ext_ref[:, pl.ds(SUBLANES - (CONV_W - 1), CONV_W - 1), :] = cs_ref[...]
        h_ref[...] = st_ref[...]

    row = lax.broadcasted_iota(jnp.int32, (q, q), 0)
    col = lax.broadcasted_iota(jnp.int32, (q, q), 1)
    causal = row >= col
    a = -jnp.exp(alog_ref[...])

    for s in range(gb):
        u = u_ref[s]
        z = u[:, :W_GROUP]
        conv = _causal_conv(ext_ref, s, u[:, W_GROUP:W_GROUP + SSD_CONV_DIM], cw_ref, cb_ref, first, q)
        xbc = _silu(conv)
        xs = xbc[:, :W_GROUP]
        bm = xbc[:, W_GROUP:W_GROUP + SSD_GROUPS * SSD_N].astype(BF16)
        cm = xbc[:, W_GROUP + SSD_GROUPS * SSD_N:].astype(BF16)
        dt = jax.nn.softplus(u[:, W_GROUP + SSD_CONV_DIM:] + dtb_ref[...])
        acum = _cumsum_rows(dt * a)
        acum_t = acum.T
        a_end = acum[q - 1:q, :]
        ys = []
        for g in range(SSD_GROUPS):
            bg = bm[:, g * SSD_N:(g + 1) * SSD_N]
            cg = cm[:, g * SSD_N:(g + 1) * SSD_N]
            cb = _dot_nt(cg, bg)
            h_prev = h_ref[s, g * hp:(g + 1) * hp, :]
            y_off = _dot_nt(cg, h_prev.astype(BF16))
            for j in range(SSD_HEADS // SSD_GROUPS):
                h = g * (SSD_HEADS // SSD_GROUPS) + j
                dt_h = dt[:, h:h + 1]
                ac_h = acum[:, h:h + 1]
                ae_h = a_end[:, h:h + 1]
                decay = jnp.exp(jnp.where(causal, ac_h - acum_t[h:h + 1, :], -jnp.inf))
                xs_h = xs[:, h * SSD_P:(h + 1) * SSD_P]
                xdt_h = xs_h * dt_h
                y_h = jnp.dot((cb * decay).astype(BF16), xdt_h.astype(BF16), preferred_element_type=F32)
                y_h = y_h + y_off[:, j * SSD_P:(j + 1) * SSD_P] * jnp.exp(ac_h)
                y_h = y_h + dsk_ref[:, h * SSD_P:(h + 1) * SSD_P] * xs_h
                ys.append(y_h)
                w_h = (xdt_h * jnp.exp(ae_h - ac_h)).astype(BF16)
                st_h = _dot_tn(w_h, bg)
                rows = pl.ds(h * SSD_P, SSD_P)
                h_ref[s, rows, :] = jnp.exp(ae_h) * h_ref[s, rows, :] + st_h
        y_ref[s] = jnp.concatenate(ys, axis=1) * _silu(z)

    @pl.when(c == nc - 1)
    def _():
        cnew_ref[...] = ext_ref[:, pl.ds(q + SUBLANES - (CONV_W - 1), CONV_W - 1), :]
        snew_ref[...] = h_ref[...]


def ssd_mixer(u, conv_state, ssm_state, conv_w, conv_b, dt_bias, a_log, d_skip, *, gb=1):
    b, l, _ = u.shape
    q = min(SSD_CHUNK, l)
    assert l % q == 0 and b % gb == 0 and q % SUBLANES == 0
    nc = l // q
    nrow = SSD_HEADS * SSD_P
    pad = lambda v: jnp.zeros((1, LANES), F32).at[0, :SSD_HEADS].set(v)
    bspec = lambda shape: pl.BlockSpec((gb,) + shape, lambda i, c: (i, 0, 0))
    pspec = lambda shape: pl.BlockSpec(shape, lambda i, c: (0, 0))
    y, cnew, snew = pl.pallas_call(
        functools.partial(_ssd_body, q=q, nc=nc),
        out_shape=[jax.ShapeDtypeStruct((b, l, W_GROUP), F32),
                   jax.ShapeDtypeStruct((b, CONV_W - 1, SSD_CONV_DIM), F32),
                   jax.ShapeDtypeStruct((b, nrow, SSD_N), F32)],
        grid=(b // gb, nc),
        in_specs=[pl.BlockSpec((gb, q, PROJ_SSD), lambda i, c: (i, c, 0)),
                  bspec((CONV_W - 1, SSD_CONV_DIM)), bspec((nrow, SSD_N)),
                  pspec((CONV_W, SSD_CONV_DIM)), pspec((1, SSD_CONV_DIM)),
                  pspec((1, LANES)), pspec((1, LANES)), pspec((1, W_GROUP))],
        out_specs=[pl.BlockSpec((gb, q, W_GROUP), lambda i, c: (i, c, 0)),
                   bspec((CONV_W - 1, SSD_CONV_DIM)), bspec((nrow, SSD_N))],
        scratch_shapes=[pltpu.VMEM((gb, q + SUBLANES, SSD_CONV_DIM), F32), pltpu.VMEM((gb, nrow, SSD_N), F32)],
        compiler_params=_cparams("arbitrary", "arbitrary"),
        name="ssd_mixer",
    )(u, conv_state, ssm_state.reshape(b, nrow, SSD_N), conv_w, conv_b.reshape(1, -1),
      pad(dt_bias), pad(a_log), jnp.repeat(d_skip, SSD_P).reshape(1, W_GROUP))
    return y, cnew, snew.reshape(b, SSD_HEADS, SSD_P, SSD_N)


def _linear_scan_rows(a, b):
    n = a.shape[0]
    row = lax.broadcasted_iota(jnp.int32, a.shape, 0)
    d = 1
    while d < n:
        m = row >= d
        b = jnp.where(m, a * pltpu.roll(b, d, 0) + b, b)
        a = jnp.where(m, a * pltpu.roll(a, d, 0), a)
        d *= 2
    return a, b


def _lru_body(u_ref, cs_ref, h0_ref, cw_ref, cb_ref, wri_ref, bri_ref, lam_ref,
              y_ref, cnew_ref, hnew_ref, ext_ref, h_ref, *, q, nc):
    c = pl.program_id(1)
    gb = u_ref.shape[0]
    first = c == 0

    @pl.when(first)
    def _():
        ext_ref[:, pl.ds(SUBLANES - (CONV_W - 1), CONV_W - 1), :] = cs_ref[...]
        h_ref[...] = h0_ref[...]

    sp = jax.nn.softplus(-lam_ref[...])
    for s in range(gb):
        u = u_ref[s]
        xb = _causal_conv(ext_ref, s, u[:, :LRU_W], cw_ref, cb_ref, first, q)
        ri = jnp.dot(xb.astype(BF16), wri_ref[...], preferred_element_type=F32) + bri_ref[...]
        r = jax.nn.sigmoid(ri[:, :LRU_W])
        i = jax.nn.sigmoid(ri[:, LRU_W:])
        log_a = -LRU_C * r * sp
        a = jnp.exp(log_a)
        bterm = jnp.sqrt(-jnp.tanh(log_a) * (a * a + 1.0)) * (i * xb)
        a_cum, h = _linear_scan_rows(a, bterm)
        h = h + a_cum * h_ref[s]
        h_ref[s] = h[q - 1:q, :]
        y_ref[s] = h * jax.nn.gelu(u[:, LRU_W:])

    @pl.when(c == nc - 1)
    def _():
        cnew_ref[...] = ext_ref[:, pl.ds(q + SUBLANES - (CONV_W - 1), CONV_W - 1), :]
        hnew_ref[...] = h_ref[...]


def _block_diag(w):
    nb, k, _ = w.shape
    eye = jnp.eye(nb, dtype=w.dtype)
    return (eye[:, None, :, None] * w[:, :, None, :]).reshape(nb * k, nb * k)


def rglru_mixer(u, conv_state, h0, conv_w, conv_b, w_r, b_r, w_i, b_i, lam, *, gb=1, chunk=256):
    b, l, _ = u.shape
    q = min(chunk, l)
    assert l % q == 0 and b % gb == 0 and q % SUBLANES == 0
    nc = l // q
    w_ri = jnp.concatenate([_block_diag(w_r), _block_diag(w_i)], axis=1).astype(BF16)
    b_ri = jnp.concatenate([b_r, b_i]).reshape(1, 2 * LRU_W)
    bspec = lambda shape: pl.BlockSpec((gb,) + shape, lambda i, c: (i, 0, 0))
    pspec = lambda shape: pl.BlockSpec(shape, lambda i, c: (0, 0))
    y, cnew, hnew = pl.pallas_call(
        functools.partial(_lru_body, q=q, nc=nc),
        out_shape=[jax.ShapeDtypeStruct((b, l, LRU_W), F32),
                   jax.ShapeDtypeStruct((b, CONV_W - 1, LRU_W), F32),
                   jax.ShapeDtypeStruct((b, 1, LRU_W), F32)],
        grid=(b // gb, nc),
        in_specs=[pl.BlockSpec((gb, q, PROJ_LRU), lambda i, c: (i, c, 0)),
                  bspec((CONV_W - 1, LRU_W)), bspec((1, LRU_W)),
                  pspec((CONV_W, LRU_W)), pspec((1, LRU_W)),
                  pspec((LRU_W, 2 * LRU_W)), pspec((1, 2 * LRU_W)), pspec((1, LRU_W))],
        out_specs=[pl.BlockSpec((gb, q, LRU_W), lambda i, c: (i, c, 0)),
                   bspec((CONV_W - 1, LRU_W)), bspec((1, LRU_W))],
        scratch_shapes=[pltpu.VMEM((gb, q + SUBLANES, LRU_W), F32), pltpu.VMEM((gb, 1, LRU_W), F32)],
        compiler_params=_cparams("arbitrary", "arbitrary"),
        name="rglru_mixer",
    )(u, conv_state, h0.reshape(b, 1, LRU_W), conv_w, conv_b.reshape(1, -1), w_ri, b_ri, lam.reshape(1, -1))
    return y, cnew, hnew.reshape(b, LRU_W)


def _hgrn_levels(c):
    ws, w = [], c // 2
    while w >= 1:
        ws.append(w)
        w //= 2
    return ws


def _boundary_rows(g, w):
    c, n = g.shape
    if 2 * w >= SUBLANES:
        gr = g.reshape(c // (2 * w), 2 * w, n)[:, w - 1:w, :]
        return jnp.broadcast_to(gr, (c // (2 * w), 2 * w, n)).reshape(c, n)
    r = lax.broadcasted_iota(jnp.int32, g.shape, 0) % (2 * w)
    out = g
    for delta in range(-(w - 1), w + 1):
        if delta != 0:
            out = jnp.where(r - (w - 1) == delta, pltpu.roll(g, delta % c, 0), out)
    return out


def _hgrn_body(u_ref, st0_ref, lbp_ref, y_ref, stn_ref, st_ref, *, c, nsteps, layer):
    step = pl.program_id(1)
    gb, tb, _ = u_ref.shape
    nk = HG_HEADS * HG_DK

    @pl.when(step == 0)
    def _():
        st_ref[...] = st0_ref[...]

    p = lbp_ref[...]
    e = jnp.exp(p - jnp.max(p, axis=0, keepdims=True))
    lb = jnp.zeros((1, nk), F32)
    for i in range(1, layer + 1):
        lb = lb + e[i:i + 1, :]
    lb = lb / jnp.sum(e, axis=0, keepdims=True)
    log_lb = jnp.log(lb)
    log_1mlb = jnp.log1p(-lb)

    t_idx = lax.broadcasted_iota(jnp.int32, (c, nk), 0)
    row_s = lax.broadcasted_iota(jnp.int32, (HG_HEADS * c, nk), 0)
    lane = lax.broadcasted_iota(jnp.int32, (HG_HEADS * c, nk), 1)
    head_rows = (row_s // c) == (lane // HG_DK)
    t_sc = lax.broadcasted_iota(jnp.int32, (c, HG_HEADS * c), 0)
    s_sc = lax.broadcasted_iota(jnp.int32, (c, HG_HEADS * c), 1) % c
    hv = lax.broadcasted_iota(jnp.int32, (nk, nk), 0) // HG_DV
    hk = lax.broadcasted_iota(jnp.int32, (nk, nk), 1) // HG_DK
    state_mask = hv == hk

    def tile_heads(x):
        xt = jnp.concatenate([x.astype(BF16)] * HG_HEADS, axis=0)
        return jnp.where(head_rows, xt, jnp.zeros_like(xt))

    def chunk(s, i):
        rows = pl.ds(pl.multiple_of(i * c, c), c)
        u = u_ref[s, rows, :]
        q = _silu(u[:, :nk])
        fz = u[:, nk:2 * nk]
        v = u[:, 2 * nk:3 * nk]
        gate = u[:, 3 * nk:]
        lf = jnp.logaddexp(log_lb, log_1mlb + jax.nn.log_sigmoid(fz))
        k = (1.0 - lb) * jax.nn.sigmoid(-fz)
        g = _cumsum_rows(lf)
        g_end = g[c - 1:c, :]
        st = st_ref[s]
        o = _dot_nt((q * jnp.exp(g)).astype(BF16), st.astype(BF16))
        sc = jnp.where(t_sc == s_sc, _dot_nt(q.astype(BF16), tile_heads(k)), 0.0)
        for w in _hgrn_levels(c):
            gr = _boundary_rows(g, w)
            upper = (t_idx // w) % 2 == 1
            ef = jnp.exp(jnp.where(upper, g - gr, gr - g))
            qt = jnp.where(upper, q * ef, 0.0)
            kt = jnp.where(upper, 0.0, k * ef)
            blk = (t_sc // (2 * w)) == (s_sc // (2 * w))
            sc = sc + jnp.where(blk, _dot_nt(qt.astype(BF16), tile_heads(kt)), 0.0)
        o = o + jnp.dot(sc.astype(BF16), tile_heads(v), preferred_element_type=F32)
        y_ref[s, rows, :] = o * _silu(gate)
        upd = _dot_tn(v.astype(BF16), (k * jnp.exp(g_end - g)).astype(BF16))
        st_ref[s] = st * jnp.exp(g_end) + jnp.where(state_mask, upd, 0.0)

    for s in range(gb):
        if tb == c:
            chunk(s, 0)
        else:
            lax.fori_loop(0, tb // c, lambda i, _: chunk(s, i), None)

    @pl.when(step == nsteps - 1)
    def _():
        stn_ref[...] = st_ref[...]


def hgrn2_mixer(u, state, lb_param, layer, *, gb=1, rows=512):
    b, l, _ = u.shape
    c = min(HG_CHUNK, l)
    tb = min(rows, l)
    assert l % tb == 0 and tb % c == 0 and b % gb == 0
    nsteps = l // tb
    nk = HG_HEADS * HG_DK
    eye = jnp.eye(HG_HEADS, dtype=F32)
    st0 = jnp.einsum('bhkv,hg->bgvhk', state, eye).reshape(b, nk, nk)
    y, stn = pl.pallas_call(
        functools.partial(_hgrn_body, c=c, nsteps=nsteps, layer=layer),
        out_shape=[jax.ShapeDtypeStruct((b, l, nk), F32), jax.ShapeDtypeStruct((b, nk, nk), F32)],
        grid=(b // gb, nsteps),
        in_specs=[pl.BlockSpec((gb, tb, PROJ_HG), lambda i, j: (i, j, 0)),
                  pl.BlockSpec((gb, nk, nk), lambda i, j: (i, 0, 0)),
                  pl.BlockSpec(lb_param.shape, lambda i, j: (0, 0))],
        out_specs=[pl.BlockSpec((gb, tb, nk), lambda i, j: (i, j, 0)),
                   pl.BlockSpec((gb, nk, nk), lambda i, j: (i, 0, 0))],
        scratch_shapes=[pltpu.VMEM((gb, nk, nk), F32)],
        compiler_params=_cparams("arbitrary", "arbitrary"),
        name="hgrn2_mixer",
    )(u, st0, lb_param)
    stn = stn.reshape(b, HG_HEADS, HG_DV, HG_HEADS, HG_DK)
    new_state = jnp.stack([stn[:, h, :, h, :] for h in range(HG_HEADS)], axis=1)
    return y, jnp.swapaxes(new_state, 2, 3)


MLA_BLOCK = LANES
MLA_QK = MLA_HEADS * MLA_BLOCK
MLA_NOPE_OFF = MLA_BLOCK - MLA_NOPE


def _segment_ones():
    seg = lambda i: jnp.where(i < MLA_ROPE, 0, jnp.where(i < MLA_NOPE_OFF, 1, 2))
    i = seg(lax.broadcasted_iota(jnp.int32, (MLA_BLOCK, MLA_BLOCK), 0))
    j = seg(lax.broadcasted_iota(jnp.int32, (MLA_BLOCK, MLA_BLOCK), 1))
    return (i == j).astype(BF16)


def _blocked_rms(x, ones, inv_cnt, gain):
    parts = []
    for h in range(MLA_HEADS):
        xb = x[:, h * MLA_BLOCK:(h + 1) * MLA_BLOCK]
        ss = jnp.dot((xb * xb).astype(BF16), ones, preferred_element_type=F32)
        parts.append(xb * lax.rsqrt(ss * inv_cnt + EPS))
    return jnp.concatenate(parts, axis=1) * gain


def _rope(x, cos, sin_lo, sin_hi):
    n = x.shape[-1]
    half = MLA_ROPE // 2
    return x * cos + pltpu.roll(x, n - half, 1) * sin_lo + pltpu.roll(x, half, 1) * sin_hi


def _paged_body(pt_ref, q_ref, latn_ref, krn_ref, wk_ref, wv_ref, clat_hbm, ckr_hbm, o_ref,
                lat_buf, kr_buf, sem, *, layer, n_pages, chunk_pages):
    b = pl.program_id(0)
    nb = pl.num_programs(0)
    t_new = q_ref.shape[0]
    hq = MLA_HEADS * t_new
    nk = MLA_HEADS * MLA_NOPE

    def page_copies(seq, slot, p):
        page = pt_ref[seq, p]
        return (pltpu.make_async_copy(clat_hbm.at[layer, page], lat_buf.at[slot, p], sem.at[slot, 0]),
                pltpu.make_async_copy(ckr_hbm.at[layer, page], kr_buf.at[slot, p], sem.at[slot, 1]))

    def fetch(seq, slot):
        def body(p, _):
            for cp in page_copies(seq, slot, p):
                cp.start()
        lax.fori_loop(0, n_pages, body, None)

    def wait(seq, slot):
        def body(p, _):
            for cp in page_copies(seq, slot, p):
                cp.wait()
        lax.fori_loop(0, n_pages, body, None)

    @pl.when(b == 0)
    def _():
        fetch(0, 0)

    @pl.when(b + 1 < nb)
    def _():
        fetch(b + 1, (b + 1) % 2)

    q = q_ref[...].astype(F32)
    qc = jnp.concatenate([q[:, h * MLA_BLOCK + MLA_NOPE_OFF:(h + 1) * MLA_BLOCK] for h in range(MLA_HEADS)], axis=1)
    row_h = lax.broadcasted_iota(jnp.int32, (hq, nk), 0) // t_new
    lane_h = lax.broadcasted_iota(jnp.int32, (hq, nk), 1) // MLA_NOPE
    own = row_h == lane_h
    qn = jnp.where(own, jnp.concatenate([qc] * MLA_HEADS, axis=0), 0.0).astype(BF16)
    ones_sel = own.astype(BF16)
    qr = jnp.concatenate([q[:, h * MLA_BLOCK:h * MLA_BLOCK + MLA_ROPE] for h in range(MLA_HEADS)], axis=0).astype(BF16)
    wk = wk_ref[...]

    def scores(lat_bf, kr_bf):
        kx = jnp.dot(lat_bf, wk, preferred_element_type=F32)
        s_nope = _dot_nt(qn, kx.astype(BF16))
        ssq = _dot_nt(ones_sel, (kx * kx).astype(BF16))
        s_rope = _dot_nt(qr, kr_bf)
        return (s_nope * lax.rsqrt(ssq * (1.0 / MLA_NOPE) + EPS) + s_rope) * MLA_SCALE

    def update(carry, s, lat_bf):
        m, l, acc = carry
        m_new = jnp.maximum(m, jnp.max(s, axis=-1, keepdims=True))
        alpha = jnp.exp(m - m_new)
        p = jnp.exp(s - m_new)
        l = alpha * l + jnp.sum(p, axis=-1, keepdims=True)
        acc = alpha * acc + jnp.dot(p.astype(BF16), lat_bf, preferred_element_type=F32)
        return m_new, l, acc

    slot = b % 2
    wait(b, slot)
    n_tok = chunk_pages * PAGE_SIZE

    def chunk(j, carry):
        pages = pl.ds(pl.multiple_of(j * chunk_pages, chunk_pages), chunk_pages)
        lat_bf = lat_buf[slot, pages].reshape(n_tok, KV_LORA).astype(BF16)
        kr_bf = kr_buf[slot, pages].reshape(n_tok, MLA_ROPE).astype(BF16)
        return update(carry, scores(lat_bf, kr_bf), lat_bf)

    carry = (jnp.full((hq, 1), -jnp.inf, F32), jnp.zeros((hq, 1), F32), jnp.zeros((hq, KV_LORA), F32))
    carry = lax.fori_loop(0, n_pages // chunk_pages, chunk, carry)

    lat_bf = latn_ref[...].astype(BF16)
    s = scores(lat_bf, krn_ref[:, :MLA_ROPE].astype(BF16))
    t_q = lax.broadcasted_iota(jnp.int32, (hq, t_new), 0) % t_new
    t_k = lax.broadcasted_iota(jnp.int32, (hq, t_new), 1)
    m, l, acc = update(carry, jnp.where(t_k <= t_q, s, -jnp.inf), lat_bf)

    ov = jnp.dot((acc / l).astype(BF16), wv_ref[...], preferred_element_type=F32)
    out = jnp.zeros((t_new, MLA_HEADS * MLA_V), F32)
    lane_v = lax.broadcasted_iota(jnp.int32, (t_new, MLA_HEADS * MLA_V), 1) // MLA_V
    for h in range(MLA_HEADS):
        out = out + jnp.where(lane_v == h, ov[h * t_new:(h + 1) * t_new, :], 0.0)
    o_ref[...] = out


def mla_sample_attention(q, lat_new, kr_new, cache_lat, cache_kr, page_table, layer, w_ukv, *, chunk_pages=8):
    b, t_new, _ = q.shape
    n_pages = page_table.shape[1]
    assert n_pages % chunk_pages == 0
    wkv = w_ukv.reshape(KV_LORA, MLA_HEADS, MLA_NOPE + MLA_V)
    wk = wkv[:, :, :MLA_NOPE].reshape(KV_LORA, MLA_HEADS * MLA_NOPE).astype(BF16)
    wv = wkv[:, :, MLA_NOPE:].reshape(KV_LORA, MLA_HEADS * MLA_V).astype(BF16)
    seq = lambda n: pl.BlockSpec((None, t_new, n), lambda i, pt: (i, 0, 0))
    full = lambda a: pl.BlockSpec(a.shape, lambda i, pt: (0, 0))
    return pl.pallas_call(
        functools.partial(_paged_body, layer=layer, n_pages=n_pages, chunk_pages=chunk_pages),
        out_shape=jax.ShapeDtypeStruct((b, t_new, MLA_HEADS * MLA_V), F32),
        grid_spec=pltpu.PrefetchScalarGridSpec(
            num_scalar_prefetch=1,
            grid=(b,),
            in_specs=[seq(MLA_QK), seq(KV_LORA), seq(MLA_BLOCK), full(wk), full(wv),
                      pl.BlockSpec(memory_space=pl.ANY), pl.BlockSpec(memory_space=pl.ANY)],
            out_specs=seq(MLA_HEADS * MLA_V),
            scratch_shapes=[pltpu.VMEM((2, n_pages, PAGE_SIZE, KV_LORA), F32),
                            pltpu.VMEM((2, n_pages, PAGE_SIZE, MLA_ROPE), F32),
                            pltpu.SemaphoreType.DMA((2, 2))],
        ),
        compiler_params=_cparams("arbitrary"),
        name="mla_sample_attention",
    )(page_table, q, lat_new, kr_new, wk, wv, cache_lat, cache_kr)


def _mla_proj_body(u_ref, cos_ref, slo_ref, shi_ref, qnorm_ref, wq_ref, gq_ref, kvnorm_ref, gkr_ref, *rest, with_kv):
    if with_kv:
        wkv_ref, gk_ref, q_ref, lat_ref, kr_ref, k_ref, v_ref = rest
    else:
        q_ref, lat_ref, kr_ref = rest
    gb, r, _ = u_ref.shape
    m = gb * r
    u = u_ref[...]
    tab = lambda t_ref: jnp.broadcast_to(t_ref[...], (gb, r, MLA_BLOCK)).reshape(m, MLA_BLOCK)
    cos, slo, shi = tab(cos_ref), tab(slo_ref), tab(shi_ref)
    tile = lambda t: jnp.concatenate([t] * MLA_HEADS, axis=1)
    ones = _segment_ones()
    lane = lax.broadcasted_iota(jnp.int32, (1, MLA_BLOCK), 1)
    inv_cnt = jnp.where(lane < MLA_NOPE_OFF, 1.0 / MLA_ROPE, 1.0 / MLA_NOPE)

    cq = _rms(u[..., :Q_LORA], qnorm_ref[...]).reshape(m, Q_LORA)
    q = jnp.dot(cq.astype(BF16), wq_ref[...], preferred_element_type=F32)
    q = _blocked_rms(q, ones, inv_cnt, gq_ref[...])
    q = _rope(q, tile(cos), tile(slo), tile(shi))
    q_ref[...] = q.reshape(gb, r, MLA_QK).astype(q_ref.dtype)

    lat = _rms(u[..., Q_LORA:Q_LORA + KV_LORA], kvnorm_ref[...])
    lat_ref[...] = lat
    kr = u[..., Q_LORA + KV_LORA:].reshape(m, MLA_BLOCK)
    kr = kr * lax.rsqrt(jnp.sum(kr * kr, axis=-1, keepdims=True) * (1.0 / MLA_ROPE) + EPS) * gkr_ref[...]
    kr = _rope(kr, cos, slo, shi)
    kr_ref[...] = kr.reshape(gb, r, MLA_BLOCK)

    if with_kv:
        kv = jnp.dot(lat.reshape(m, KV_LORA).astype(BF16), wkv_ref[...], preferred_element_type=F32)
        k = _blocked_rms(kv[:, :MLA_QK], ones, inv_cnt, gk_ref[...]) + tile(kr)
        k_ref[...] = k.reshape(gb, r, MLA_QK).astype(k_ref.dtype)
        v_ref[...] = kv[:, MLA_QK:].reshape(gb, r, MLA_HEADS * MLA_V).astype(v_ref.dtype)


def _rope_tables(pos0, l):
    half = MLA_ROPE // 2
    inv_freq = ROPE_THETA ** (-jnp.arange(half, dtype=F32) / half)
    ang = (pos0 + jnp.arange(l)).astype(F32)[:, None] * inv_freq
    cos, sin = jnp.cos(ang), jnp.sin(ang)
    pad = lambda t, fill: jnp.concatenate([t, jnp.full((l, MLA_BLOCK - t.shape[1]), fill, F32)], axis=1)
    zero = jnp.zeros_like(sin)
    return (pad(jnp.concatenate([cos, cos], 1), 1.0), pad(jnp.concatenate([-sin, zero], 1), 0.0),
            pad(jnp.concatenate([zero, sin], 1), 0.0))


def _head_blocked(rope_part, nope_part):
    rows = (rope_part if rope_part is not None else nope_part).shape[0]
    cols = []
    for h in range(MLA_HEADS):
        rp = rope_part[:, h] if rope_part is not None else jnp.zeros((rows, MLA_ROPE), F32)
        npart = nope_part[:, h] if nope_part is not None else jnp.zeros((rows, MLA_NOPE), F32)
        cols += [rp, jnp.zeros((rows, MLA_NOPE_OFF - MLA_ROPE), F32), npart]
    return jnp.concatenate(cols, axis=1)


def mla_project(u, pos0, q_norm, w_uq, kv_norm, w_ukv, qn_nope, qn_rope, kn_nope, kn_rope, *, with_kv, rows=512):
    b, l, _ = u.shape
    t = _Tiling(b, l, rows)
    wq = w_uq.reshape(Q_LORA, MLA_HEADS, MLA_NOPE + MLA_ROPE)
    wq = _head_blocked(wq[:, :, MLA_NOPE:], wq[:, :, :MLA_NOPE]).astype(BF16)
    ones_h = jnp.ones((1, MLA_HEADS, 1), F32)
    q_gain = qn_nope if with_kv else qn_nope * kn_nope
    gq = _head_blocked(ones_h * qn_rope.reshape(1, 1, -1), ones_h * q_gain.reshape(1, 1, -1))
    gkr = jnp.concatenate([kn_rope, jnp.zeros((MLA_BLOCK - MLA_ROPE,), F32)]).reshape(1, MLA_BLOCK)
    cos, slo, shi = _rope_tables(pos0, l)
    uspec = lambda n: pl.BlockSpec((t.gb, t.r, n), lambda i: (*t.tok_index(i), 0))
    tspec = pl.BlockSpec((t.r, MLA_BLOCK), lambda i: (i % t.nl, 0))
    pspec = lambda a: pl.BlockSpec(a.shape, lambda i: (0, 0))
    ins = [u, cos, slo, shi, q_norm.reshape(1, -1), wq, gq, kv_norm.reshape(1, -1), gkr]
    in_specs = [uspec(PROJ_MLA), tspec, tspec, tspec] + [pspec(a) for a in ins[4:]]
    outs = [jax.ShapeDtypeStruct((b, l, MLA_QK), BF16), jax.ShapeDtypeStruct((b, l, KV_LORA), F32),
            jax.ShapeDtypeStruct((b, l, MLA_BLOCK), F32)]
    out_specs = [uspec(MLA_QK), uspec(KV_LORA), uspec(MLA_BLOCK)]
    if with_kv:
        wkv = w_ukv.reshape(KV_LORA, MLA_HEADS, MLA_NOPE + MLA_V)
        wkv = jnp.concatenate([_head_blocked(None, wkv[:, :, :MLA_NOPE]),
                               wkv[:, :, MLA_NOPE:].reshape(KV_LORA, MLA_HEADS * MLA_V)], axis=1).astype(BF16)
        gk = _head_blocked(None, ones_h * kn_nope.reshape(1, 1, -1))
        ins += [wkv, gk]
        in_specs += [pspec(wkv), pspec(gk)]
        outs += [jax.ShapeDtypeStruct((b, l, MLA_QK), BF16), jax.ShapeDtypeStruct((b, l, MLA_HEADS * MLA_V), BF16)]
        out_specs += [uspec(MLA_QK), uspec(MLA_HEADS * MLA_V)]
    return pl.pallas_call(
        functools.partial(_mla_proj_body, with_kv=with_kv),
        out_shape=outs, grid=(t.steps,), in_specs=in_specs, out_specs=out_specs,
        compiler_params=_cparams("arbitrary"),
        name="mla_project",
    )(*ins)


def _flash_body(q_ref, k_ref, v_ref, o_ref, *, tq):
    i = pl.program_id(1)
    row = lax.broadcasted_iota(jnp.int32, (tq, tq), 0)
    col = lax.broadcasted_iota(jnp.int32, (tq, tq), 1)
    outs = []
    for h in range(MLA_HEADS):
        lanes = pl.ds(h * MLA_BLOCK, MLA_BLOCK)
        vl = pl.ds(h * MLA_V, MLA_V)
        q = q_ref[:, lanes]

        def update(carry, kb, vb, mask):
            m, l, acc = carry
            s = _dot_nt(q, kb) * MLA_SCALE
            if mask:
                s = jnp.where(row >= col, s, -jnp.inf)
            m_new = jnp.maximum(m, jnp.max(s, axis=-1, keepdims=True))
            alpha = jnp.exp(m - m_new)
            p = jnp.exp(s - m_new)
            l = alpha * l + jnp.sum(p, axis=-1, keepdims=True)
            acc = alpha * acc + jnp.dot(p.astype(BF16), vb, preferred_element_type=F32)
            return m_new, l, acc

        def past(j, carry):
            rows = pl.ds(pl.multiple_of(j * tq, tq), tq)
            return update(carry, k_ref[rows, lanes], v_ref[rows, vl], False)

        carry = (jnp.full((tq, 1), -jnp.inf, F32), jnp.zeros((tq, 1), F32), jnp.zeros((tq, MLA_V), F32))
        carry = lax.fori_loop(0, i, past, carry)
        rows = pl.ds(pl.multiple_of(i * tq, tq), tq)
        m, l, acc = update(carry, k_ref[rows, lanes], v_ref[rows, vl], True)
        outs.append(acc / l)
    o_ref[...] = jnp.concatenate(outs, axis=1)


def mla_prompt_attention(q, k, v, *, tq=512):
    b, l, _ = q.shape
    tq = min(tq, l)
    assert l % tq == 0
    return pl.pallas_call(
        functools.partial(_flash_body, tq=tq),
        out_shape=jax.ShapeDtypeStruct((b, l, MLA_HEADS * MLA_V), F32),
        grid=(b, l // tq),
        in_specs=[pl.BlockSpec((None, tq, MLA_QK), lambda bi, i: (bi, i, 0)),
                  pl.BlockSpec((None, l, MLA_QK), lambda bi, i: (bi, 0, 0)),
                  pl.BlockSpec((None, l, MLA_HEADS * MLA_V), lambda bi, i: (bi, 0, 0))],
        out_specs=pl.BlockSpec((None, tq, MLA_HEADS * MLA_V), lambda bi, i: (bi, i, 0)),
        compiler_params=_cparams("arbitrary", "arbitrary"),
        name="mla_prompt_attention",
    )(q, k, v)


def _layer(x, mods, layer, row0, w, conv_a, ssm_a, conv_b, h_b, s_c, paged, seq_block):
    x = ffn_halfstep(x, mods, layer, row0, 0, w['norm_ffn1'], w['ffn1_in'], w['ffn1_out'])
    u_a, u_b, u_c, u_d = mixer_projection(x, mods, layer, row0, w['norm_mix'], w['proj_in'])
    y_a, conv_a, ssm_a = ssd_mixer(u_a, conv_a, ssm_a, w['ssd_conv_w'], w['ssd_conv_b'], w['ssd_dt_bias'],
                                   w['ssd_a_log'], w['ssd_d'], gb=seq_block)
    y_b, conv_b, h_b = rglru_mixer(u_b, conv_b, h_b, w['lru_conv_w'], w['lru_conv_b'], w['lru_w_r'], w['lru_b_r'],
                                   w['lru_w_i'], w['lru_b_i'], w['lru_lambda'], gb=seq_block)
    y_c, s_c = hgrn2_mixer(u_c, s_c, w['hgrn_lb'], layer, gb=seq_block)
    mla_w = (w['mla_q_norm'], w['mla_w_uq'], w['mla_kv_norm'], w['mla_w_ukv'], w['mla_qn_nope'], w['mla_qn_rope'],
             w['mla_kn_nope'], w['mla_kn_rope'])
    if paged is None:
        q, lat, kr, k, v = mla_project(u_d, 0, *mla_w, with_kv=True)
        y_d = mla_prompt_attention(q, k, v)
    else:
        cache_lat, cache_kr, page_table = paged
        q, lat, kr = mla_project(u_d, PAST_LEN, *mla_w, with_kv=False)
        y_d = mla_sample_attention(q, lat, kr, cache_lat, cache_kr, page_table, layer, w['mla_w_ukv'])
    x = output_merge(x, (y_a, y_b, y_c, y_d), mods, layer, row0, w['out_norm'], w['w_out'])
    x = ffn_halfstep(x, mods, layer, row0, 6, w['norm_ffn2'], w['ffn2_in'], w['ffn2_out'])
    return x, (lat, kr[..., :MLA_ROPE], ssm_a, conv_a, h_b, conv_b, s_c)


def kernel(x_prompt, x_sample, cache_mla_latent, cache_mla_krope, state_ssd, state_ssd_conv, state_lru,
           state_lru_conv, state_hgrn, page_table, c_prompt, c_sample, w_ada, b_ada, norm_ffn1, w_ffn1_in,
           w_ffn1_out, norm_mix, w_in, ssd_conv_w, ssd_conv_b, ssd_dt_bias, ssd_a_log, ssd_d, lru_conv_w,
           lru_conv_b, lru_w_r, lru_b_r, lru_w_i, lru_b_i, lru_lambda, hgrn_lb, mla_q_norm, mla_w_uq,
           mla_kv_norm, mla_w_ukv, mla_qn_nope, mla_qn_rope, mla_kn_nope, mla_kn_rope, out_norm, w_out,
           norm_ffn2, w_ffn2_in, w_ffn2_out):
    per_layer = dict(norm_ffn1=norm_ffn1, norm_mix=norm_mix, ssd_conv_w=ssd_conv_w, ssd_conv_b=ssd_conv_b,
                     ssd_dt_bias=ssd_dt_bias, ssd_a_log=ssd_a_log, ssd_d=ssd_d, lru_conv_w=lru_conv_w,
                     lru_conv_b=lru_conv_b, lru_w_r=lru_w_r, lru_b_r=lru_b_r, lru_w_i=lru_w_i, lru_b_i=lru_b_i,
                     lru_lambda=lru_lambda, mla_q_norm=mla_q_norm, mla_w_uq=mla_w_uq, mla_kv_norm=mla_kv_norm,
                     mla_w_ukv=mla_w_ukv, mla_qn_nope=mla_qn_nope, mla_qn_rope=mla_qn_rope,
                     mla_kn_nope=mla_kn_nope, mla_kn_rope=mla_kn_rope, out_norm=out_norm, norm_ffn2=norm_ffn2)
    bp, bs = x_prompt.shape[0], x_sample.shape[0]
    dt = x_prompt.dtype
    mods = ada_modulation(jnp.concatenate([c_sample, c_prompt], axis=0), w_ada, b_ada)
    mods = mods.reshape(DEPTH, bs + bp, 1, N_MOD * D_MODEL)
    z_conv_a = jnp.zeros((bp, CONV_W - 1, SSD_CONV_DIM), dt)
    z_ssm = jnp.zeros((bp, SSD_HEADS, SSD_P, SSD_N), F32)
    z_conv_b = jnp.zeros((bp, CONV_W - 1, LRU_W), dt)
    z_h = jnp.zeros((bp, LRU_W), F32)
    z_s = jnp.zeros((bp, HG_HEADS, HG_DK, HG_DV), F32)
    yp, ys = x_prompt, x_sample
    new_p = [[] for _ in range(7)]
    new_s = [[] for _ in range(7)]
    sample_block = math.gcd(bs, SUBLANES)
    for l in range(DEPTH):
        w = {name: arr[l] for name, arr in per_layer.items()}
        w['hgrn_lb'] = hgrn_lb
        w['ffn1_in'], w['ffn1_out'] = pack_ffn_in(w_ffn1_in[l]), w_ffn1_out[l].astype(BF16)
        w['ffn2_in'], w['ffn2_out'] = pack_ffn_in(w_ffn2_in[l]), w_ffn2_out[l].astype(BF16)
        w['proj_in'] = pack_proj_in(w_in[l])
        w['w_out'] = w_out[l].astype(BF16)
        yp, st_p = _layer(yp, mods, l, bs, w, z_conv_a, z_ssm, z_conv_b, z_h, z_s, None, 1)
        ys, st_s = _layer(ys, mods, l, 0, w, state_ssd_conv[l], state_ssd[l], state_lru_conv[l], state_lru[l],
                          state_hgrn[l], (cache_mla_latent, cache_mla_krope, page_table), sample_block)
        for i in range(7):
            new_p[i].append(st_p[i])
            new_s[i].append(st_s[i])
    stk = lambda rows: jnp.stack(rows).astype(dt)
    lat_p, kr_p, ssd_p, ssdc_p, lru_p, lruc_p, hg_p = [stk(r) for r in new_p]
    lat_s, kr_s, ssd_s, ssdc_s, lru_s, lruc_s, hg_s = [stk(r) for r in new_s]
    return (yp, ys, lat_p, lat_s, kr_p, kr_s, ssd_p, ssd_s, ssdc_p, ssdc_s, lru_p, lru_s, lruc_p, lruc_s, hg_p, hg_s)
```

```python
import functools
import math

import jax
import jax.numpy as jnp
from jax import lax
from jax.experimental import pallas as pl
from jax.experimental.pallas import tpu as pltpu

F32 = jnp.float32
BF16 = jnp.bfloat16

D_MODEL = 1024
DEPTH = 4
PAST_LEN = 8192
PAGE_SIZE = 128
W_GROUP = 256
HEAD_DIM = 64
SSD_HEADS = 4
SSD_P = 64
SSD_N = 128
SSD_GROUPS = 2
SSD_CHUNK = 128
CONV_W = 4
SSD_CONV_DIM = W_GROUP + 2 * SSD_GROUPS * SSD_N
LRU_W = W_GROUP
LRU_BLOCKS = 4
LRU_BW = LRU_W // LRU_BLOCKS
LRU_C = 8.0
HG_HEADS = 4
HG_DK = 64
HG_DV = 64
HG_CHUNK = 64
MLA_HEADS = 4
MLA_NOPE = 64
MLA_ROPE = 32
MLA_V = 64
Q_LORA = 256
KV_LORA = 128
ROPE_THETA = 10000.0
MLA_SCALE = 1.0 / math.sqrt(MLA_NOPE + MLA_ROPE)
D_FF = 2816
N_MOD = 9
EPS = 1e-6
IN_SSD = W_GROUP + SSD_CONV_DIM + SSD_HEADS
IN_LRU = 2 * LRU_W
IN_HG = 4 * HG_HEADS * HG_DK
IN_MLA = Q_LORA + KV_LORA + MLA_ROPE
OFF_LRU = IN_SSD
OFF_HG = OFF_LRU + IN_LRU
OFF_MLA = OFF_HG + IN_HG

LANES = 128
SUBLANES = 8
VMEM_BYTES_V7X = 64 * 1024 * 1024
VMEM_LIMIT = VMEM_BYTES_V7X * 3 // 4

PROJ_SSD = W_GROUP + SSD_CONV_DIM + LANES
PROJ_LRU = IN_LRU
PROJ_HG = IN_HG
PROJ_MLA = Q_LORA + KV_LORA + LANES


def _cparams(*sem):
    return pltpu.CompilerParams(dimension_semantics=sem, vmem_limit_bytes=VMEM_LIMIT)


def _silu(x):
    return x * jax.nn.sigmoid(x)


def _rms(x, w):
    return x * lax.rsqrt(jnp.mean(x * x, axis=-1, keepdims=True) + EPS) * w


def _ada_body(c_ref, w_ref, b_ref, o_ref):
    a = _silu(c_ref[...]).astype(BF16)
    o_ref[...] = jnp.dot(a, w_ref[...].astype(BF16), preferred_element_type=F32) + b_ref[...]


def ada_modulation(c_all, w_ada, b_ada, *, tn=1152):
    r, d = c_all.shape
    depth, _, n = w_ada.shape
    return pl.pallas_call(
        _ada_body,
        out_shape=jax.ShapeDtypeStruct((depth, r, n), F32),
        grid=(depth, n // tn),
        in_specs=[
            pl.BlockSpec((r, d), lambda l, j: (0, 0)),
            pl.BlockSpec((None, d, tn), lambda l, j: (l, 0, j)),
            pl.BlockSpec((None, 1, tn), lambda l, j: (l, 0, j)),
        ],
        out_specs=pl.BlockSpec((None, r, tn), lambda l, j: (l, 0, j)),
        compiler_params=_cparams("arbitrary", "arbitrary"),
        name="ada_modulation",
    )(c_all, w_ada, b_ada.reshape(depth, 1, n))


class _Tiling:
    def __init__(self, b, l, rows):
        if l >= rows:
            assert l % rows == 0
            self.gb, self.r = 1, rows
        else:
            gb = min(b, max(1, rows // l))
            assert b % gb == 0
            self.gb, self.r = gb, l
        self.b, self.l = b, l
        self.nb = b // self.gb
        self.nl = l // self.r
        self.steps = self.nb * self.nl
        self.m = self.gb * self.r

    def tok_index(self, i):
        return (i // self.nl, i % self.nl)


def _mod_spec(t, layer, row0, k):
    assert row0 % t.gb == 0
    return pl.BlockSpec((None, t.gb, 1, D_MODEL), lambda i, *_: (layer, row0 // t.gb + i // t.nl, 0, k))


def _ffn_body(x_ref, sh_ref, sc_ref, gt_ref, nw_ref, win_ref, wout_ref, o_ref, h_ref, acc_ref, *, tf, nj):
    j = pl.program_id(1)
    gb, r, d = x_ref.shape

    @pl.when(j == 0)
    def _():
        h = _rms(x_ref[...], nw_ref[...]) * (1.0 + sc_ref[...]) + sh_ref[...]
        h_ref[...] = h.reshape(gb * r, d).astype(BF16)
        acc_ref[...] = jnp.zeros_like(acc_ref)

    gu = jnp.dot(h_ref[...], win_ref[...], preferred_element_type=F32)
    a = _silu(gu[:, :tf]) * gu[:, tf:]
    acc_ref[...] += jnp.dot(a.astype(BF16), wout_ref[...], preferred_element_type=F32)

    @pl.when(j == nj - 1)
    def _():
        o_ref[...] = x_ref[...] + 0.5 * gt_ref[...] * acc_ref[...].reshape(gb, r, d)


def ffn_halfstep(x, mods, layer, row0, k0, norm_w, w_in_packed, w_out, *, rows=1024, tf=256):
    b, l, d = x.shape
    t = _Tiling(b, l, rows)
    nj = D_FF // tf
    xspec = pl.BlockSpec((t.gb, t.r, d), lambda i, j: (*t.tok_index(i), 0))
    return pl.pallas_call(
        functools.partial(_ffn_body, tf=tf, nj=nj),
        out_shape=jax.ShapeDtypeStruct(x.shape, F32),
        grid=(t.steps, nj),
        in_specs=[
            xspec,
            _mod_spec(t, layer, row0, k0),
            _mod_spec(t, layer, row0, k0 + 1),
            _mod_spec(t, layer, row0, k0 + 2),
            pl.BlockSpec((1, d), lambda i, j: (0, 0)),
            pl.BlockSpec((d, 2 * tf), lambda i, j: (0, j)),
            pl.BlockSpec((tf, d), lambda i, j: (j, 0)),
        ],
        out_specs=xspec,
        scratch_shapes=[pltpu.VMEM((t.m, d), BF16), pltpu.VMEM((t.m, d), F32)],
        compiler_params=_cparams("arbitrary", "arbitrary"),
        name="ffn_halfstep",
    )(x, mods, mods, mods, norm_w.reshape(1, d), w_in_packed, w_out)


def pack_ffn_in(w_in, tf=256):
    d = w_in.shape[0]
    g = w_in[:, :D_FF].reshape(d, D_FF // tf, tf)
    u = w_in[:, D_FF:].reshape(d, D_FF // tf, tf)
    return jnp.concatenate([g, u], axis=-1).reshape(d, 2 * D_FF).astype(BF16)


def _proj_body(x_ref, sh_ref, sc_ref, nw_ref, w_ref, ssd_ref, lru_ref, hg_ref, mla_ref):
    gb, r, d = x_ref.shape
    h = _rms(x_ref[...], nw_ref[...]) * (1.0 + sc_ref[...]) + sh_ref[...]
    h = h.reshape(gb * r, d).astype(BF16)
    off = 0
    for o_ref in (ssd_ref, lru_ref, hg_ref, mla_ref):
        n = o_ref.shape[-1]
        u = jnp.dot(h, w_ref[:, off:off + n], preferred_element_type=F32)
        o_ref[...] = u.reshape(gb, r, n)
        off += n


def mixer_projection(x, mods, layer, row0, norm_w, w_packed, *, rows=512):
    b, l, d = x.shape
    t = _Tiling(b, l, rows)
    widths = (PROJ_SSD, PROJ_LRU, PROJ_HG, PROJ_MLA)
    xspec = pl.BlockSpec((t.gb, t.r, d), lambda i: (*t.tok_index(i), 0))
    return pl.pallas_call(
        _proj_body,
        out_shape=[jax.ShapeDtypeStruct((b, l, n), F32) for n in widths],
        grid=(t.steps,),
        in_specs=[
            xspec,
            _mod_spec(t, layer, row0, 3),
            _mod_spec(t, layer, row0, 4),
            pl.BlockSpec((1, d), lambda i: (0, 0)),
            pl.BlockSpec((d, sum(widths)), lambda i: (0, 0)),
        ],
        out_specs=[pl.BlockSpec((t.gb, t.r, n), lambda i: (*t.tok_index(i), 0)) for n in widths],
        compiler_params=_cparams("arbitrary"),
        name="mixer_projection",
    )(x, mods, mods, norm_w.reshape(1, d), w_packed)


def pack_proj_in(w_in, kr_copies):
    d = w_in.shape[0]
    z = lambda n: jnp.zeros((d, n), w_in.dtype)
    w_kr = w_in[:, OFF_MLA + Q_LORA + KV_LORA:]
    cols = [
        w_in[:, :W_GROUP + SSD_CONV_DIM], w_in[:, W_GROUP + SSD_CONV_DIM:IN_SSD], z(LANES - SSD_HEADS),
        w_in[:, OFF_LRU:OFF_HG],
        w_in[:, OFF_HG:OFF_MLA],
        w_in[:, OFF_MLA:OFF_MLA + Q_LORA + KV_LORA], *([w_kr] * kr_copies), z(LANES - kr_copies * MLA_ROPE),
    ]
    return jnp.concatenate(cols, axis=1).astype(BF16)


def _head_ones(n):
    i = lax.broadcasted_iota(jnp.int32, (n, n), 0) // HEAD_DIM
    j = lax.broadcasted_iota(jnp.int32, (n, n), 1) // HEAD_DIM
    return (i == j).astype(BF16)


def _out_body(x_ref, ya_ref, yb_ref, yc_ref, yd_ref, gt_ref, nw_ref, w_ref, o_ref):
    gb, r, d = x_ref.shape
    ones = _head_ones(W_GROUP)
    acc = jnp.zeros((gb * r, d), F32)
    for k, y_ref in enumerate((ya_ref, yb_ref, yc_ref, yd_ref)):
        y = y_ref[...].reshape(gb * r, W_GROUP)
        ss = jnp.dot((y * y).astype(BF16), ones, preferred_element_type=F32)
        yn = y * lax.rsqrt(ss * (1.0 / HEAD_DIM) + EPS) * nw_ref[:, k * W_GROUP:(k + 1) * W_GROUP]
        acc += jnp.dot(yn.astype(BF16), w_ref[k * W_GROUP:(k + 1) * W_GROUP, :], preferred_element_type=F32)
    o_ref[...] = x_ref[...] + gt_ref[...] * acc.reshape(gb, r, d)


def output_merge(x, ys, mods, layer, row0, out_norm, w_out, *, rows=512):
    b, l, d = x.shape
    t = _Tiling(b, l, rows)
    xspec = pl.BlockSpec((t.gb, t.r, d), lambda i: (*t.tok_index(i), 0))
    yspec = pl.BlockSpec((t.gb, t.r, W_GROUP), lambda i: (*t.tok_index(i), 0))
    return pl.pallas_call(
        _out_body,
        out_shape=jax.ShapeDtypeStruct(x.shape, F32),
        grid=(t.steps,),
        in_specs=[xspec, yspec, yspec, yspec, yspec,
                  _mod_spec(t, layer, row0, 5),
                  pl.BlockSpec((1, d), lambda i: (0, 0)),
                  pl.BlockSpec((d, d), lambda i: (0, 0))],
        out_specs=xspec,
        compiler_params=_cparams("arbitrary"),
        name="output_merge",
    )(x, *ys, mods, out_norm.reshape(1, d), w_out)


def _cumsum_rows(x):
    n = x.shape[0]
    row = lax.broadcasted_iota(jnp.int32, x.shape, 0)
    d = 1
    while d < n:
        x = x + jnp.where(row >= d, pltpu.roll(x, d, 0), 0.0)
        d *= 2
    return x


def _dot_nt(a, b):
    return lax.dot_general(a, b, (((1,), (1,)), ((), ())), preferred_element_type=F32)


def _dot_tn(a, b):
    return lax.dot_general(a, b, (((0,), (0,)), ((), ())), preferred_element_type=F32)


def _causal_conv(ext_ref, s, x, cw_ref, cb_ref, first, q):
    @pl.when(jnp.logical_not(first))
    def _():
        ext_ref[s, pl.ds(0, SUBLANES), :] = ext_ref[s, pl.ds(q, SUBLANES), :]

    ext_ref[s, pl.ds(SUBLANES, q), :] = x
    out = cb_ref[...]
    for k in range(CONV_W):
        out = out + cw_ref[k:k + 1, :] * ext_ref[s, pl.ds(SUBLANES - (CONV_W - 1) + k, q), :]
    return out


def _ssd_body(u_ref, cs_ref, st_ref, cw_ref, cb_ref, dtb_ref, alog_ref, dsk_ref,
              y_ref, cnew_ref, snew_ref, ext_ref, h_ref, *, q, nc):
    c = pl.program_id(1)
    gb = u_ref.shape[0]
    first = c == 0
    hp = SSD_HEADS // SSD_GROUPS * SSD_P

    @pl.when(first)
    def _():
        ext_ref[:, pl.ds(SUBLANES - (CONV_W - 1), CONV_W - 1), :] = cs_ref[...]
        h_ref[...] = st_ref[...]

    row = lax.broadcasted_iota(jnp.int32, (q, q), 0)
    col = lax.broadcasted_iota(jnp.int32, (q, q), 1)
    causal = row >= col
    a = -jnp.exp(alog_ref[...])

    for s in range(gb):
        u = u_ref[s]
        z = u[:, :W_GROUP]
        conv = _causal_conv(ext_ref, s, u[:, W_GROUP:W_GROUP + SSD_CONV_DIM], cw_ref, cb_ref, first, q)
        xbc = _silu(conv)
        xs = xbc[:, :W_GROUP]
        bm = xbc[:, W_GROUP:W_GROUP + SSD_GROUPS * SSD_N].astype(BF16)
        cm = xbc[:, W_GROUP + SSD_GROUPS * SSD_N:].astype(BF16)
        dt = jax.nn.softplus(u[:, W_GROUP + SSD_CONV_DIM:] + dtb_ref[...])
        acum = _cumsum_rows(dt * a)
        acum_t = acum.T
        a_end = acum[q - 1:q, :]
        ys = []
        for g in range(SSD_GROUPS):
            bg = bm[:, g * SSD_N:(g + 1) * SSD_N]
            cg = cm[:, g * SSD_N:(g + 1) * SSD_N]
            cb = _dot_nt(cg, bg)
            h_prev = h_ref[s, g * hp:(g + 1) * hp, :]
            y_off = _dot_nt(cg, h_prev.astype(BF16))
            for j in range(SSD_HEADS // SSD_GROUPS):
                h = g * (SSD_HEADS // SSD_GROUPS) + j
                dt_h = dt[:, h:h + 1]
                ac_h = acum[:, h:h + 1]
                ae_h = a_end[:, h:h + 1]
                decay = jnp.exp(jnp.where(causal, ac_h - acum_t[h:h + 1, :], -jnp.inf))
                xs_h = xs[:, h * SSD_P:(h + 1) * SSD_P]
                xdt_h = xs_h * dt_h
                y_h = jnp.dot((cb * decay).astype(BF16), xdt_h.astype(BF16), preferred_element_type=F32)
                y_h = y_h + y_off[:, j * SSD_P:(j + 1) * SSD_P] * jnp.exp(ac_h)
                y_h = y_h + dsk_ref[:, h * SSD_P:(h + 1) * SSD_P] * xs_h
                ys.append(y_h)
                w_h = (xdt_h * jnp.exp(ae_h - ac_h)).astype(BF16)
                st_h = _dot_tn(w_h, bg)
                rows = pl.ds(h * SSD_P, SSD_P)
                h_ref[s, rows, :] = jnp.exp(ae_h) * h_ref[s, rows, :] + st_h
        y_ref[s] = jnp.concatenate(ys, axis=1) * _silu(z)

    @pl.when(c == nc - 1)
    def _():
        cnew_ref[...] = ext_ref[:, pl.ds(q + SUBLANES - (CONV_W - 1), CONV_W - 1), :]
        snew_ref[...] = h_ref[...]


def ssd_mixer(u, conv_state, ssm_state, conv_w, conv_b, dt_bias, a_log, d_skip, *, gb=1):
    b, l, _ = u.shape
    q = min(SSD_CHUNK, l)
    assert l % q == 0 and b % gb == 0 and q % SUBLANES == 0
    nc = l // q
    nrow = SSD_HEADS * SSD_P
    pad = lambda v: jnp.zeros((1, LANES), F32).at[0, :SSD_HEADS].set(v)
    bspec = lambda shape: pl.BlockSpec((gb,) + shape, lambda i, c: (i, 0, 0))
    pspec = lambda shape: pl.BlockSpec(shape, lambda i, c: (0, 0))
    y, cnew, snew = pl.pallas_call(
        functools.partial(_ssd_body, q=q, nc=nc),
        out_shape=[jax.ShapeDtypeStruct((b, l, W_GROUP), F32),
                   jax.ShapeDtypeStruct((b, CONV_W - 1, SSD_CONV_DIM), F32),
                   jax.ShapeDtypeStruct((b, nrow, SSD_N), F32)],
        grid=(b // gb, nc),
        in_specs=[pl.BlockSpec((gb, q, PROJ_SSD), lambda i, c: (i, c, 0)),
                  bspec((CONV_W - 1, SSD_CONV_DIM)), bspec((nrow, SSD_N)),
                  pspec((CONV_W, SSD_CONV_DIM)), pspec((1, SSD_CONV_DIM)),
                  pspec((1, LANES)), pspec((1, LANES)), pspec((1, W_GROUP))],
        out_specs=[pl.BlockSpec((gb, q, W_GROUP), lambda i, c: (i, c, 0)),
                   bspec((CONV_W - 1, SSD_CONV_DIM)), bspec((nrow, SSD_N))],
        scratch_shapes=[pltpu.VMEM((gb, q + SUBLANES, SSD_CONV_DIM), F32), pltpu.VMEM((gb, nrow, SSD_N), F32)],
        compiler_params=_cparams("arbitrary", "arbitrary"),
        name="ssd_mixer",
    )(u, conv_state, ssm_state.reshape(b, nrow, SSD_N), conv_w, conv_b.reshape(1, -1),
      pad(dt_bias), pad(a_log), jnp.repeat(d_skip, SSD_P).reshape(1, W_GROUP))
    return y, cnew, snew.reshape(b, SSD_HEADS, SSD_P, SSD_N)


def _linear_scan_rows(a, b):
    n = a.shape[0]
    row = lax.broadcasted_iota(jnp.int32, a.shape, 0)
    d = 1
    while d < n:
        m = row >= d
        b = jnp.where(m, a * pltpu.roll(b, d, 0) + b, b)
        a = jnp.where(m, a * pltpu.roll(a, d, 0), a)
        d *= 2
    return a, b


def _lru_body(u_ref, cs_ref, h0_ref, cw_ref, cb_ref, wri_ref, bri_ref, lam_ref,
              y_ref, cnew_ref, hnew_ref, ext_ref, h_ref, *, q, nc):
    c = pl.program_id(1)
    gb = u_ref.shape[0]
    first = c == 0

    @pl.when(first)
    def _():
        ext_ref[:, pl.ds(SUBLANES - (CONV_W - 1), CONV_W - 1), :] = cs_ref[...]
        h_ref[...] = h0_ref[...]

    sp = jax.nn.softplus(-lam_ref[...])
    for s in range(gb):
        u = u_ref[s]
        xb = _causal_conv(ext_ref, s, u[:, :LRU_W], cw_ref, cb_ref, first, q)
        ri = jnp.dot(xb.astype(BF16), wri_ref[...], preferred_element_type=F32) + bri_ref[...]
        r = jax.nn.sigmoid(ri[:, :LRU_W])
        i = jax.nn.sigmoid(ri[:, LRU_W:])
        log_a = -LRU_C * r * sp
        a = jnp.exp(log_a)
        bterm = jnp.sqrt(-jnp.tanh(log_a) * (a * a + 1.0)) * (i * xb)
        a_cum, h = _linear_scan_rows(a, bterm)
        h = h + a_cum * h_ref[s]
        h_ref[s] = h[q - 1:q, :]
        y_ref[s] = h * jax.nn.gelu(u[:, LRU_W:])

    @pl.when(c == nc - 1)
    def _():
        cnew_ref[...] = ext_ref[:, pl.ds(q + SUBLANES - (CONV_W - 1), CONV_W - 1), :]
        hnew_ref[...] = h_ref[...]


def _block_diag(w):
    nb, k, _ = w.shape
    eye = jnp.eye(nb, dtype=w.dtype)
    return (eye[:, None, :, None] * w[:, :, None, :]).reshape(nb * k, nb * k)


def rglru_mixer(u, conv_state, h0, conv_w, conv_b, w_r, b_r, w_i, b_i, lam, *, gb=1, chunk=256):
    b, l, _ = u.shape
    q = min(chunk, l)
    assert l % q == 0 and b % gb == 0 and q % SUBLANES == 0
    nc = l // q
    w_ri = jnp.concatenate([_block_diag(w_r), _block_diag(w_i)], axis=1).astype(BF16)
    b_ri = jnp.concatenate([b_r, b_i]).reshape(1, 2 * LRU_W)
    bspec = lambda shape: pl.BlockSpec((gb,) + shape, lambda i, c: (i, 0, 0))
    pspec = lambda shape: pl.BlockSpec(shape, lambda i, c: (0, 0))
    y, cnew, hnew = pl.pallas_call(
        functools.partial(_lru_body, q=q, nc=nc),
        out_shape=[jax.ShapeDtypeStruct((b, l, LRU_W), F32),
                   jax.ShapeDtypeStruct((b, CONV_W - 1, LRU_W), F32),
                   jax.ShapeDtypeStruct((b, 1, LRU_W), F32)],
        grid=(b // gb, nc),
        in_specs=[pl.BlockSpec((gb, q, PROJ_LRU), lambda i, c: (i, c, 0)),
                  bspec((CONV_W - 1, LRU_W)), bspec((1, LRU_W)),
                  pspec((CONV_W, LRU_W)), pspec((1, LRU_W)),
                  pspec((LRU_W, 2 * LRU_W)), pspec((1, 2 * LRU_W)), pspec((1, LRU_W))],
        out_specs=[pl.BlockSpec((gb, q, LRU_W), lambda i, c: (i, c, 0)),
                   bspec((CONV_W - 1, LRU_W)), bspec((1, LRU_W))],
        scratch_shapes=[pltpu.VMEM((gb, q + SUBLANES, LRU_W), F32), pltpu.VMEM((gb, 1, LRU_W), F32)],
        compiler_params=_cparams("arbitrary", "arbitrary"),
        name="rglru_mixer",
    )(u, conv_state, h0.reshape(b, 1, LRU_W), conv_w, conv_b.reshape(1, -1), w_ri, b_ri, lam.reshape(1, -1))
    return y, cnew, hnew.reshape(b, LRU_W)


def _hgrn_levels(c):
    ws, w = [], c // 2
    while w >= 1:
        ws.append(w)
        w //= 2
    return ws


def _boundary_rows(g, w):
    c, n = g.shape
    if 2 * w >= SUBLANES:
        gr = g.reshape(c // (2 * w), 2 * w, n)[:, w - 1:w, :]
        return jnp.broadcast_to(gr, (c // (2 * w), 2 * w, n)).reshape(c, n)
    r = lax.broadcasted_iota(jnp.int32, g.shape, 0) % (2 * w)
    out = g
    for delta in range(-(w - 1), w + 1):
        if delta != 0:
            out = jnp.where(r - (w - 1) == delta, pltpu.roll(g, delta % c, 0), out)
    return out


def _hgrn_body(u_ref, st0_ref, lbp_ref, y_ref, stn_ref, st_ref, *, c, nsteps, layer):
    step = pl.program_id(1)
    gb, tb, _ = u_ref.shape
    nk = HG_HEADS * HG_DK

    @pl.when(step == 0)
    def _():
        st_ref[...] = st0_ref[...]

    p = lbp_ref[...]
    e = jnp.exp(p - jnp.max(p, axis=0, keepdims=True))
    lb = jnp.zeros((1, nk), F32)
    for i in range(1, layer + 1):
        lb = lb + e[i:i + 1, :]
    lb = lb / jnp.sum(e, axis=0, keepdims=True)
    log_lb = jnp.log(lb)
    log_1mlb = jnp.log1p(-lb)

    t_idx = lax.broadcasted_iota(jnp.int32, (c, nk), 0)
    row_s = lax.broadcasted_iota(jnp.int32, (HG_HEADS * c, nk), 0)
    lane = lax.broadcasted_iota(jnp.int32, (HG_HEADS * c, nk), 1)
    head_rows = (row_s // c) == (lane // HG_DK)
    t_sc = lax.broadcasted_iota(jnp.int32, (c, HG_HEADS * c), 0)
    s_sc = lax.broadcasted_iota(jnp.int32, (c, HG_HEADS * c), 1) % c
    hv = lax.broadcasted_iota(jnp.int32, (nk, nk), 0) // HG_DV
    hk = lax.broadcasted_iota(jnp.int32, (nk, nk), 1) // HG_DK
    state_mask = hv == hk

    def tile_heads(x):
        xt = jnp.concatenate([x.astype(BF16)] * HG_HEADS, axis=0)
        return jnp.where(head_rows, xt, jnp.zeros_like(xt))

    def chunk(s, i):
        rows = pl.ds(pl.multiple_of(i * c, c), c)
        u = u_ref[s, rows, :]
        q = _silu(u[:, :nk])
        fz = u[:, nk:2 * nk]
        v = u[:, 2 * nk:3 * nk]
        gate = u[:, 3 * nk:]
        lf = jnp.logaddexp(log_lb, log_1mlb + jax.nn.log_sigmoid(fz))
        k = (1.0 - lb) * jax.nn.sigmoid(-fz)
        g = _cumsum_rows(lf)
        g_end = g[c - 1:c, :]
        st = st_ref[s]
        o = _dot_nt((q * jnp.exp(g)).astype(BF16), st.astype(BF16))
        sc = jnp.where(t_sc == s_sc, _dot_nt(q.astype(BF16), tile_heads(k)), 0.0)
        for w in _hgrn_levels(c):
            gr = _boundary_rows(g, w)
            upper = (t_idx // w) % 2 == 1
            ef = jnp.exp(jnp.where(upper, g - gr, gr - g))
            qt = jnp.where(upper, q * ef, 0.0)
            kt = jnp.where(upper, 0.0, k * ef)
            blk = (t_sc // (2 * w)) == (s_sc // (2 * w))
            sc = sc + jnp.where(blk, _dot_nt(qt.astype(BF16), tile_heads(kt)), 0.0)
        o = o + jnp.dot(sc.astype(BF16), tile_heads(v), preferred_element_type=F32)
        y_ref[s, rows, :] = o * _silu(gate)
        upd = _dot_tn(v.astype(BF16), (k * jnp.exp(g_end - g)).astype(BF16))
        st_ref[s] = st * jnp.exp(g_end) + jnp.where(state_mask, upd, 0.0)

    for s in range(gb):
        if tb == c:
            chunk(s, 0)
        else:
            lax.fori_loop(0, tb // c, lambda i, _: chunk(s, i), None)

    @pl.when(step == nsteps - 1)
    def _():
        stn_ref[...] = st_ref[...]


def hgrn2_mixer(u, state, lb_param, layer, *, gb=1, rows=512):
    b, l, _ = u.shape
    c = min(HG_CHUNK, l)
    tb = min(rows, l)
    assert l % tb == 0 and tb % c == 0 and b % gb == 0
    nsteps = l // tb
    nk = HG_HEADS * HG_DK
    eye = jnp.eye(HG_HEADS, dtype=F32)
    st0 = jnp.einsum('bhkv,hg->bgvhk', state, eye).reshape(b, nk, nk)
    y, stn = pl.pallas_call(
        functools.partial(_hgrn_body, c=c, nsteps=nsteps, layer=layer),
        out_shape=[jax.ShapeDtypeStruct((b, l, nk), F32), jax.ShapeDtypeStruct((b, nk, nk), F32)],
        grid=(b // gb, nsteps),
        in_specs=[pl.BlockSpec((gb, tb, PROJ_HG), lambda i, j: (i, j, 0)),
                  pl.BlockSpec((gb, nk, nk), lambda i, j: (i, 0, 0)),
                  pl.BlockSpec(lb_param.shape, lambda i, j: (0, 0))],
        out_specs=[pl.BlockSpec((gb, tb, nk), lambda i, j: (i, j, 0)),
                   pl.BlockSpec((gb, nk, nk), lambda i, j: (i, 0, 0))],
        scratch_shapes=[pltpu.VMEM((gb, nk, nk), F32)],
        compiler_params=_cparams("arbitrary", "arbitrary"),
        name="hgrn2_mixer",
    )(u, st0, lb_param)
    stn = stn.reshape(b, HG_HEADS, HG_DV, HG_HEADS, HG_DK)
    new_state = jnp.stack([stn[:, h, :, h, :] for h in range(HG_HEADS)], axis=1)
    return y, jnp.swapaxes(new_state, 2, 3)


MLA_BLOCK = LANES
MLA_QK = MLA_HEADS * MLA_BLOCK
MLA_NOPE_OFF = MLA_BLOCK - MLA_NOPE


def _segment_ones():
    seg = lambda i: jnp.where(i < MLA_ROPE, 0, jnp.where(i < MLA_NOPE_OFF, 1, 2))
    i = seg(lax.broadcasted_iota(jnp.int32, (MLA_BLOCK, MLA_BLOCK), 0))
    j = seg(lax.broadcasted_iota(jnp.int32, (MLA_BLOCK, MLA_BLOCK), 1))
    return (i == j).astype(BF16)


def _blocked_rms(x, ones, inv_cnt, gain):
    parts = []
    for h in range(MLA_HEADS):
        xb = x[:, h * MLA_BLOCK:(h + 1) * MLA_BLOCK]
        ss = jnp.dot((xb * xb).astype(BF16), ones, preferred_element_type=F32)
        parts.append(xb * lax.rsqrt(ss * inv_cnt + EPS))
    return jnp.concatenate(parts, axis=1) * gain


def _rope(x, cos, sin_lo, sin_hi):
    n = x.shape[-1]
    half = MLA_ROPE // 2
    return x * cos + pltpu.roll(x, n - half, 1) * sin_lo + pltpu.roll(x, half, 1) * sin_hi


def _paged_body(pt_ref, qn_ref, qr_ref, latn_ref, kr4n_ref, wkt_ref, wv_ref, clat_hbm, ckr_hbm, o_ref,
                lat_buf, kr_buf, latbf_ref, s_ref, sem, *, layer, n_pages, chunk_pages):
    b = pl.program_id(0)
    nb = pl.num_programs(0)
    t_new = qn_ref.shape[0]
    hq = MLA_HEADS * t_new
    nk = MLA_HEADS * MLA_NOPE
    n_tok = chunk_pages * PAGE_SIZE
    issue_unroll = 8

    def page_copies(seq, slot, p):
        page = pt_ref[seq, p]
        return (pltpu.make_async_copy(clat_hbm.at[layer, page], lat_buf.at[slot, p], sem.at[slot, 0]),
                pltpu.make_async_copy(ckr_hbm.at[layer, page], kr_buf.at[slot, p], sem.at[slot, 1]))

    def fetch(seq, slot):
        def body(g, _):
            for k in range(issue_unroll):
                for cp in page_copies(seq, slot, g * issue_unroll + k):
                    cp.start()
        lax.fori_loop(0, n_pages // issue_unroll, body, None)

    @pl.when(b == 0)
    def _():
        fetch(0, 0)

    @pl.when(b + 1 < nb)
    def _():
        fetch(b + 1, (b + 1) % 2)

    tile_rows = lambda x: jnp.concatenate([x] * MLA_HEADS, axis=0)
    own_n = (lax.broadcasted_iota(jnp.int32, (hq, nk), 0) // t_new
             == lax.broadcasted_iota(jnp.int32, (hq, nk), 1) // MLA_NOPE)
    own_r = (lax.broadcasted_iota(jnp.int32, (hq, LANES), 0) // t_new
             == lax.broadcasted_iota(jnp.int32, (hq, LANES), 1) // MLA_ROPE)
    qn_sel = jnp.where(own_n, tile_rows(qn_ref[...]), 0.0).astype(BF16)
    wkt = wkt_ref[...]
    q_abs = jnp.dot(qn_sel, wkt, preferred_element_type=F32).astype(BF16)
    qr_sel = jnp.where(own_r, tile_rows(qr_ref[...]), 0.0).astype(BF16)
    lhs = jnp.concatenate([wkt, q_abs], axis=0)

    def scores(lat_bf, kr4_bf):
        n = lat_bf.shape[0]
        both = _dot_nt(lhs, lat_bf)
        sq = both[:nk] * both[:nk]
        ssq = []
        for h in range(MLA_HEADS):
            part = sq[h * MLA_NOPE:(h + 1) * MLA_NOPE].reshape(MLA_NOPE // SUBLANES, SUBLANES, n).sum(axis=0)
            shift = SUBLANES // 2
            while shift >= 1:
                part = part + pltpu.roll(part, shift, 0)
                shift //= 2
            ssq.append(part)
        ssq = jnp.concatenate(ssq, axis=0)
        s_rope = jnp.dot(qr_sel, kr4_bf, preferred_element_type=F32)
        return (both[nk:] * lax.rsqrt(ssq * (1.0 / MLA_NOPE) + EPS) + s_rope) * MLA_SCALE

    latn_bf = latn_ref[...].astype(BF16)
    s_new = scores(latn_bf, kr4n_ref[...].T.astype(BF16))
    t_q = lax.broadcasted_iota(jnp.int32, (hq, t_new), 0) % t_new
    t_k = lax.broadcasted_iota(jnp.int32, (hq, t_new), 1)
    s_new = jnp.where(t_k <= t_q, s_new, -jnp.inf)
    m = jnp.max(s_new, axis=-1, keepdims=True)

    slot = b % 2
    for p in range(n_pages):
        for cp in page_copies(b, slot, p):
            cp.wait()

    for c in range(n_pages // chunk_pages):
        p0 = c * chunk_pages
        lat_bf = lat_buf[slot, p0:p0 + chunk_pages].reshape(n_tok, KV_LORA).astype(BF16)
        latbf_ref[c * n_tok:(c + 1) * n_tok, :] = lat_bf
        kr_t = jnp.concatenate([kr_buf[slot, p0 + i] for i in range(chunk_pages)], axis=1).astype(BF16)
        s = scores(lat_bf, tile_rows(kr_t))
        s_ref[:, c * n_tok:(c + 1) * n_tok] = s
        m = jnp.maximum(m, jnp.max(s, axis=-1, keepdims=True))

    p_old = jnp.exp(s_ref[...] - m)
    p_new = jnp.exp(s_new - m)
    l = jnp.sum(p_old, axis=-1, keepdims=True) + jnp.sum(p_new, axis=-1, keepdims=True)
    acc = jnp.dot(p_old.astype(BF16), latbf_ref[...], preferred_element_type=F32)
    acc = acc + jnp.dot(p_new.astype(BF16), latn_bf, preferred_element_type=F32)
    ov = jnp.dot((acc / l).astype(BF16), wv_ref[...], preferred_element_type=F32)
    out = jnp.zeros((t_new, MLA_HEADS * MLA_V), F32)
    lane_v = lax.broadcasted_iota(jnp.int32, (t_new, MLA_HEADS * MLA_V), 1) // MLA_V
    for h in range(MLA_HEADS):
        out = out + jnp.where(lane_v == h, ov[h * t_new:(h + 1) * t_new, :], 0.0)
    o_ref[...] = out


def mla_sample_attention(qn, qr, lat_new, kr4_new, cache_lat, cache_kr_t, page_table, layer, w_ukv, *, chunk_pages=8):
    b, t_new, _ = qn.shape
    n_pages = page_table.shape[1]
    assert n_pages % chunk_pages == 0 and n_pages % 8 == 0
    assert t_new == SUBLANES
    n_past = n_pages * PAGE_SIZE
    wkv = w_ukv.reshape(KV_LORA, MLA_HEADS, MLA_NOPE + MLA_V)
    wk = wkv[:, :, :MLA_NOPE].reshape(KV_LORA, MLA_HEADS * MLA_NOPE).astype(BF16)
    wv = wkv[:, :, MLA_NOPE:].reshape(KV_LORA, MLA_HEADS * MLA_V).astype(BF16)
    seq = lambda n: pl.BlockSpec((None, t_new, n), lambda i, pt: (i, 0, 0))
    full = lambda a: pl.BlockSpec(a.shape, lambda i, pt: (0, 0))
    return pl.pallas_call(
        functools.partial(_paged_body, layer=layer, n_pages=n_pages, chunk_pages=chunk_pages),
        out_shape=jax.ShapeDtypeStruct((b, t_new, MLA_HEADS * MLA_V), F32),
        grid_spec=pltpu.PrefetchScalarGridSpec(
            num_scalar_prefetch=1,
            grid=(b,),
            in_specs=[seq(MLA_HEADS * MLA_NOPE), seq(LANES), seq(KV_LORA), seq(LANES), full(wk.T), full(wv),
                      pl.BlockSpec(memory_space=pl.ANY), pl.BlockSpec(memory_space=pl.ANY)],
            out_specs=seq(MLA_HEADS * MLA_V),
            scratch_shapes=[pltpu.VMEM((2, n_pages, PAGE_SIZE, KV_LORA), F32),
                            pltpu.VMEM((2, n_pages, MLA_ROPE, PAGE_SIZE), F32),
                            pltpu.VMEM((n_past, KV_LORA), BF16),
                            pltpu.VMEM((MLA_HEADS * t_new, n_past), F32),
                            pltpu.SemaphoreType.DMA((2, 2))],
        ),
        compiler_params=_cparams("arbitrary"),
        name="mla_sample_attention",
    )(page_table, qn, qr, lat_new, kr4_new, wk.T, wv, cache_lat, cache_kr_t)


def _mla_proj_sample_body(u_ref, cos_ref, slo_ref, shi_ref, qnorm_ref, wq_ref, gn_ref, gr_ref, kvnorm_ref, gkr_ref,
                          qn_ref, qr_ref, lat_ref, kr_ref):
    gb, r, _ = u_ref.shape
    m = gb * r
    nk = MLA_HEADS * MLA_NOPE
    u = u_ref[...]
    tab = lambda t_ref: jnp.broadcast_to(t_ref[...], (gb, r, LANES)).reshape(m, LANES)
    cos, slo, shi = tab(cos_ref), tab(slo_ref), tab(shi_ref)
    i = lax.broadcasted_iota(jnp.int32, (LANES, LANES), 0) // MLA_ROPE
    j = lax.broadcasted_iota(jnp.int32, (LANES, LANES), 1) // MLA_ROPE
    ones_r = (i == j).astype(BF16)

    cq = _rms(u[..., :Q_LORA], qnorm_ref[...]).reshape(m, Q_LORA)
    q = jnp.dot(cq.astype(BF16), wq_ref[...], preferred_element_type=F32)
    qn, qr = q[:, :nk], q[:, nk:]
    ssn = jnp.dot((qn * qn).astype(BF16), _head_ones(nk), preferred_element_type=F32)
    qn_ref[...] = (qn * lax.rsqrt(ssn * (1.0 / MLA_NOPE) + EPS) * gn_ref[...]).reshape(gb, r, nk)
    ssr = jnp.dot((qr * qr).astype(BF16), ones_r, preferred_element_type=F32)
    qr = qr * lax.rsqrt(ssr * (1.0 / MLA_ROPE) + EPS) * gr_ref[...]
    qr_ref[...] = _rope(qr, cos, slo, shi).reshape(gb, r, LANES)

    lat_ref[...] = _rms(u[..., Q_LORA:Q_LORA + KV_LORA], kvnorm_ref[...])
    kr = u[..., Q_LORA + KV_LORA:].reshape(m, LANES)
    kr = kr * lax.rsqrt(jnp.mean(kr * kr, axis=-1, keepdims=True) + EPS) * gkr_ref[...]
    kr_ref[...] = _rope(kr, cos, slo, shi).reshape(gb, r, LANES)


def _mla_proj_body(u_ref, cos_ref, slo_ref, shi_ref, qnorm_ref, wq_ref, gq_ref, kvnorm_ref, gkr_ref,
                   wkv_ref, gk_ref, q_ref, lat_ref, kr_ref, k_ref, v_ref):
    gb, r, _ = u_ref.shape
    m = gb * r
    u = u_ref[...]
    tab = lambda t_ref: jnp.broadcast_to(t_ref[...], (gb, r, MLA_BLOCK)).reshape(m, MLA_BLOCK)
    cos, slo, shi = tab(cos_ref), tab(slo_ref), tab(shi_ref)
    tile = lambda t: jnp.concatenate([t] * MLA_HEADS, axis=1)
    ones = _segment_ones()
    lane = lax.broadcasted_iota(jnp.int32, (1, MLA_BLOCK), 1)
    inv_cnt = jnp.where(lane < MLA_NOPE_OFF, 1.0 / MLA_ROPE, 1.0 / MLA_NOPE)

    cq = _rms(u[..., :Q_LORA], qnorm_ref[...]).reshape(m, Q_LORA)
    q = jnp.dot(cq.astype(BF16), wq_ref[...], preferred_element_type=F32)
    q = _blocked_rms(q, ones, inv_cnt, gq_ref[...])
    q = _rope(q, tile(cos), tile(slo), tile(shi))
    q_ref[...] = q.reshape(gb, r, MLA_QK).astype(q_ref.dtype)

    lat = _rms(u[..., Q_LORA:Q_LORA + KV_LORA], kvnorm_ref[...])
    lat_ref[...] = lat
    kr = u[..., Q_LORA + KV_LORA:].reshape(m, MLA_BLOCK)
    kr = kr * lax.rsqrt(jnp.sum(kr * kr, axis=-1, keepdims=True) * (1.0 / MLA_ROPE) + EPS) * gkr_ref[...]
    kr = _rope(kr, cos, slo, shi)
    kr_ref[...] = kr.reshape(gb, r, MLA_BLOCK)

    kv = jnp.dot(lat.reshape(m, KV_LORA).astype(BF16), wkv_ref[...], preferred_element_type=F32)
    k = _blocked_rms(kv[:, :MLA_QK], ones, inv_cnt, gk_ref[...]) + tile(kr)
    k_ref[...] = k.reshape(gb, r, MLA_QK).astype(k_ref.dtype)
    lane_v = lax.broadcasted_iota(jnp.int32, (1, MLA_QK), 1) % MLA_BLOCK
    v = kv[:, MLA_QK:] + jnp.where(lane_v == MLA_V, 1.0, 0.0)
    v_ref[...] = v.reshape(gb, r, MLA_QK).astype(v_ref.dtype)


def _rope_tables(pos0, l, copies):
    half = MLA_ROPE // 2
    inv_freq = ROPE_THETA ** (-jnp.arange(half, dtype=F32) / half)
    ang = (pos0 + jnp.arange(l)).astype(F32)[:, None] * inv_freq
    cos, sin = jnp.cos(ang), jnp.sin(ang)
    rest = LANES - copies * MLA_ROPE
    lay = lambda t, fill: jnp.concatenate([jnp.tile(t, (1, copies)), jnp.full((l, rest), fill, F32)], axis=1)
    zero = jnp.zeros_like(sin)
    return (lay(jnp.concatenate([cos, cos], 1), 1.0), lay(jnp.concatenate([-sin, zero], 1), 0.0),
            lay(jnp.concatenate([zero, sin], 1), 0.0))


def _head_blocked(rope_part, nope_part):
    rows = (rope_part if rope_part is not None else nope_part).shape[0]
    cols = []
    for h in range(MLA_HEADS):
        rp = rope_part[:, h] if rope_part is not None else jnp.zeros((rows, MLA_ROPE), F32)
        npart = nope_part[:, h] if nope_part is not None else jnp.zeros((rows, MLA_NOPE), F32)
        cols += [rp, jnp.zeros((rows, MLA_NOPE_OFF - MLA_ROPE), F32), npart]
    return jnp.concatenate(cols, axis=1)


def _mla_project_call(body, name, u, tables, params, out_widths, out_dtypes, rows):
    b, l, _ = u.shape
    t = _Tiling(b, l, rows)
    uspec = lambda n: pl.BlockSpec((t.gb, t.r, n), lambda i: (*t.tok_index(i), 0))
    tspec = pl.BlockSpec((t.r, LANES), lambda i: (i % t.nl, 0))
    pspec = lambda a: pl.BlockSpec(a.shape, lambda i: (0, 0))
    return pl.pallas_call(
        body,
        out_shape=[jax.ShapeDtypeStruct((b, l, n), dt) for n, dt in zip(out_widths, out_dtypes)],
        grid=(t.steps,),
        in_specs=[uspec(PROJ_MLA), tspec, tspec, tspec] + [pspec(a) for a in params],
        out_specs=[uspec(n) for n in out_widths],
        compiler_params=_cparams("arbitrary"),
        name=name,
    )(u, *tables, *params)


def mla_project_prompt(u, q_norm, w_uq, kv_norm, w_ukv, qn_nope, qn_rope, kn_nope, kn_rope, *, rows=512):
    wq = w_uq.reshape(Q_LORA, MLA_HEADS, MLA_NOPE + MLA_ROPE)
    wq = _head_blocked(wq[:, :, MLA_NOPE:], wq[:, :, :MLA_NOPE]).astype(BF16)
    ones_h = jnp.ones((1, MLA_HEADS, 1), F32)
    gq = _head_blocked(ones_h * qn_rope.reshape(1, 1, -1), ones_h * qn_nope.reshape(1, 1, -1))
    gkr = jnp.concatenate([kn_rope, jnp.zeros((MLA_BLOCK - MLA_ROPE,), F32)]).reshape(1, MLA_BLOCK)
    wkv = w_ukv.reshape(KV_LORA, MLA_HEADS, MLA_NOPE + MLA_V)
    w_v = jnp.pad(wkv[:, :, MLA_NOPE:], ((0, 0), (0, 0), (0, MLA_BLOCK - MLA_V))).reshape(KV_LORA, MLA_QK)
    wkv = jnp.concatenate([_head_blocked(None, wkv[:, :, :MLA_NOPE]), w_v], axis=1).astype(BF16)
    gk = _head_blocked(None, ones_h * kn_nope.reshape(1, 1, -1))
    params = [q_norm.reshape(1, -1), wq, gq, kv_norm.reshape(1, -1), gkr, wkv, gk]
    return _mla_project_call(_mla_proj_body, "mla_project_prompt", u, _rope_tables(0, u.shape[1], 1), params,
                             (MLA_QK, KV_LORA, MLA_BLOCK, MLA_QK, MLA_QK), (BF16, F32, F32, BF16, BF16), rows)


def mla_project_sample(u, pos0, q_norm, w_uq, kv_norm, qn_nope, qn_rope, kn_nope, kn_rope, *, rows=512):
    wq = w_uq.reshape(Q_LORA, MLA_HEADS, MLA_NOPE + MLA_ROPE)
    wq = jnp.concatenate([wq[:, :, :MLA_NOPE].reshape(Q_LORA, -1), wq[:, :, MLA_NOPE:].reshape(Q_LORA, -1)],
                         axis=1).astype(BF16)
    tile4 = lambda g: jnp.tile(g, MLA_HEADS).reshape(1, -1)
    params = [q_norm.reshape(1, -1), wq, tile4(qn_nope * kn_nope), tile4(qn_rope), kv_norm.reshape(1, -1), tile4(kn_rope)]
    return _mla_project_call(_mla_proj_sample_body, "mla_project_sample", u,
                             _rope_tables(pos0, u.shape[1], LANES // MLA_ROPE), params,
                             (MLA_HEADS * MLA_NOPE, LANES, KV_LORA, LANES), (F32, F32, F32, F32), rows)


def _flash_body(q_ref, k_ref, v_ref, o_ref, *, tq):
    i = pl.program_id(1)
    row = lax.broadcasted_iota(jnp.int32, (tq, tq), 0)
    col = lax.broadcasted_iota(jnp.int32, (tq, tq), 1)
    c = MLA_SCALE * math.log2(math.e)
    blk = lambda h: pl.ds(h * MLA_BLOCK, MLA_BLOCK)
    qs = [q_ref[:, blk(h)] for h in range(MLA_HEADS)]

    def update(carry, rows, mask):
        new = []
        for h in range(MLA_HEADS):
            m, acc = carry[h]
            s = _dot_nt(qs[h], k_ref[rows, blk(h)])
            if mask:
                s = jnp.where(row >= col, s, -jnp.inf)
            m_new = jnp.maximum(m, jnp.max(s, axis=-1, keepdims=True))
            alpha = jnp.exp2((m - m_new) * c)
            p = jnp.exp2((s - m_new) * c)
            acc = alpha * acc + jnp.dot(p.astype(BF16), v_ref[rows, blk(h)], preferred_element_type=F32)
            new.append((m_new, acc))
        return tuple(new)

    def past(j, carry):
        return update(carry, pl.ds(pl.multiple_of(j * tq, tq), tq), False)

    carry = tuple((jnp.full((tq, 1), -jnp.inf, F32), jnp.zeros((tq, MLA_BLOCK), F32)) for _ in range(MLA_HEADS))
    carry = lax.fori_loop(0, i, past, carry)
    carry = update(carry, pl.ds(pl.multiple_of(i * tq, tq), tq), True)
    o_ref[...] = jnp.concatenate([acc[:, :MLA_V] / acc[:, MLA_V:MLA_V + 1] for _, acc in carry], axis=1)


def mla_prompt_attention(q, k, v, *, tq=512):
    b, l, _ = q.shape
    tq = min(tq, l)
    assert l % tq == 0
    return pl.pallas_call(
        functools.partial(_flash_body, tq=tq),
        out_shape=jax.ShapeDtypeStruct((b, l, MLA_HEADS * MLA_V), F32),
        grid=(b, l // tq),
        in_specs=[pl.BlockSpec((None, tq, MLA_QK), lambda bi, i: (bi, i, 0)),
                  pl.BlockSpec((None, l, MLA_QK), lambda bi, i: (bi, 0, 0)),
                  pl.BlockSpec((None, l, MLA_QK), lambda bi, i: (bi, 0, 0))],
        out_specs=pl.BlockSpec((None, tq, MLA_HEADS * MLA_V), lambda bi, i: (bi, i, 0)),
        compiler_params=_cparams("arbitrary", "arbitrary"),
        name="mla_prompt_attention",
    )(q, k, v)


def _layer(x, mods, layer, row0, w, conv_a, ssm_a, conv_b, h_b, s_c, paged, seq_block):
    x = ffn_halfstep(x, mods, layer, row0, 0, w['norm_ffn1'], w['ffn1_in'], w['ffn1_out'])
    w_proj = w['proj_in_prompt'] if paged is None else w['proj_in_sample']
    u_a, u_b, u_c, u_d = mixer_projection(x, mods, layer, row0, w['norm_mix'], w_proj)
    y_a, conv_a, ssm_a = ssd_mixer(u_a, conv_a, ssm_a, w['ssd_conv_w'], w['ssd_conv_b'], w['ssd_dt_bias'],
                                   w['ssd_a_log'], w['ssd_d'], gb=seq_block)
    y_b, conv_b, h_b = rglru_mixer(u_b, conv_b, h_b, w['lru_conv_w'], w['lru_conv_b'], w['lru_w_r'], w['lru_b_r'],
                                   w['lru_w_i'], w['lru_b_i'], w['lru_lambda'], gb=seq_block)
    y_c, s_c = hgrn2_mixer(u_c, s_c, w['hgrn_lb'], layer, gb=seq_block)
    mla_w = (w['mla_q_norm'], w['mla_w_uq'], w['mla_kv_norm'], w['mla_w_ukv'], w['mla_qn_nope'], w['mla_qn_rope'],
             w['mla_kn_nope'], w['mla_kn_rope'])
    if paged is None:
        q, lat, kr, k, v = mla_project_prompt(u_d, *mla_w)
        y_d = mla_prompt_attention(q, k, v)
    else:
        cache_lat, cache_kr_t, page_table = paged
        qn, qr, lat, kr = mla_project_sample(u_d, PAST_LEN, *(mla_w[:3] + mla_w[4:]))
        y_d = mla_sample_attention(qn, qr, lat, kr, cache_lat, cache_kr_t, page_table, layer, w['mla_w_ukv'])
    x = output_merge(x, (y_a, y_b, y_c, y_d), mods, layer, row0, w['out_norm'], w['w_out'])
    x = ffn_halfstep(x, mods, layer, row0, 6, w['norm_ffn2'], w['ffn2_in'], w['ffn2_out'])
    return x, (lat, kr[..., :MLA_ROPE], ssm_a, conv_a, h_b, conv_b, s_c)


def kernel(x_prompt, x_sample, cache_mla_latent, cache_mla_krope, state_ssd, state_ssd_conv, state_lru,
           state_lru_conv, state_hgrn, page_table, c_prompt, c_sample, w_ada, b_ada, norm_ffn1, w_ffn1_in,
           w_ffn1_out, norm_mix, w_in, ssd_conv_w, ssd_conv_b, ssd_dt_bias, ssd_a_log, ssd_d, lru_conv_w,
           lru_conv_b, lru_w_r, lru_b_r, lru_w_i, lru_b_i, lru_lambda, hgrn_lb, mla_q_norm, mla_w_uq,
           mla_kv_norm, mla_w_ukv, mla_qn_nope, mla_qn_rope, mla_kn_nope, mla_kn_rope, out_norm, w_out,
           norm_ffn2, w_ffn2_in, w_ffn2_out):
    per_layer = dict(norm_ffn1=norm_ffn1, norm_mix=norm_mix, ssd_conv_w=ssd_conv_w, ssd_conv_b=ssd_conv_b,
                     ssd_dt_bias=ssd_dt_bias, ssd_a_log=ssd_a_log, ssd_d=ssd_d, lru_conv_w=lru_conv_w,
                     lru_conv_b=lru_conv_b, lru_w_r=lru_w_r, lru_b_r=lru_b_r, lru_w_i=lru_w_i, lru_b_i=lru_b_i,
                     lru_lambda=lru_lambda, mla_q_norm=mla_q_norm, mla_w_uq=mla_w_uq, mla_kv_norm=mla_kv_norm,
                     mla_w_ukv=mla_w_ukv, mla_qn_nope=mla_qn_nope, mla_qn_rope=mla_qn_rope,
                     mla_kn_nope=mla_kn_nope, mla_kn_rope=mla_kn_rope, out_norm=out_norm, norm_ffn2=norm_ffn2)
    bp, bs = x_prompt.shape[0], x_sample.shape[0]
    dt = x_prompt.dtype
    mods = ada_modulation(jnp.concatenate([c_sample, c_prompt], axis=0), w_ada, b_ada)
    mods = mods.reshape(DEPTH, bs + bp, 1, N_MOD * D_MODEL)
    z_conv_a = jnp.zeros((bp, CONV_W - 1, SSD_CONV_DIM), dt)
    z_ssm = jnp.zeros((bp, SSD_HEADS, SSD_P, SSD_N), F32)
    z_conv_b = jnp.zeros((bp, CONV_W - 1, LRU_W), dt)
    z_h = jnp.zeros((bp, LRU_W), F32)
    z_s = jnp.zeros((bp, HG_HEADS, HG_DK, HG_DV), F32)
    cache_kr_t = jnp.swapaxes(cache_mla_krope, 2, 3)
    yp, ys = x_prompt, x_sample
    new_p = [[] for _ in range(7)]
    new_s = [[] for _ in range(7)]
    sample_block = math.gcd(bs, SUBLANES)
    for l in range(DEPTH):
        w = {name: arr[l] for name, arr in per_layer.items()}
        w['hgrn_lb'] = hgrn_lb
        w['ffn1_in'], w['ffn1_out'] = pack_ffn_in(w_ffn1_in[l]), w_ffn1_out[l].astype(BF16)
        w['ffn2_in'], w['ffn2_out'] = pack_ffn_in(w_ffn2_in[l]), w_ffn2_out[l].astype(BF16)
        w['proj_in_prompt'] = pack_proj_in(w_in[l], 1)
        w['proj_in_sample'] = pack_proj_in(w_in[l], LANES // MLA_ROPE)
        w['w_out'] = w_out[l].astype(BF16)
        yp, st_p = _layer(yp, mods, l, bs, w, z_conv_a, z_ssm, z_conv_b, z_h, z_s, None, 1)
        ys, st_s = _layer(ys, mods, l, 0, w, state_ssd_conv[l], state_ssd[l], state_lru_conv[l], state_lru[l],
                          state_hgrn[l], (cache_mla_latent, cache_kr_t, page_table), sample_block)
        for i in range(7):
            new_p[i].append(st_p[i])
            new_s[i].append(st_s[i])
    stk = lambda rows: jnp.stack(rows).astype(dt)
    lat_p, kr_p, ssd_p, ssdc_p, lru_p, lruc_p, hg_p = [stk(r) for r in new_p]
    lat_s, kr_s, ssd_s, ssdc_s, lru_s, lruc_s, hg_s = [stk(r) for r in new_s]
    return (yp, ys, lat_p, lat_s, kr_p, kr_s, ssd_p, ssd_s, ssdc_p, ssdc_s, lru_p, lru_s, lruc_p, lruc_s, hg_p, hg_s)
```

```python
import functools
import math

import jax
import jax.numpy as jnp
from jax import lax
from jax.experimental import pallas as pl
from jax.experimental.pallas import tpu as pltpu

F32 = jnp.float32
BF16 = jnp.bfloat16

D_MODEL = 1024
DEPTH = 4
PAST_LEN = 8192
PAGE_SIZE = 128
W_GROUP = 256
HEAD_DIM = 64
SSD_HEADS = 4
SSD_P = 64
SSD_N = 128
SSD_GROUPS = 2
SSD_CHUNK = 128
CONV_W = 4
SSD_CONV_DIM = W_GROUP + 2 * SSD_GROUPS * SSD_N
LRU_W = W_GROUP
LRU_BLOCKS = 4
LRU_BW = LRU_W // LRU_BLOCKS
LRU_C = 8.0
HG_HEADS = 4
HG_DK = 64
HG_DV = 64
HG_CHUNK = 64
MLA_HEADS = 4
MLA_NOPE = 64
MLA_ROPE = 32
MLA_V = 64
Q_LORA = 256
KV_LORA = 128
ROPE_THETA = 10000.0
MLA_SCALE = 1.0 / math.sqrt(MLA_NOPE + MLA_ROPE)
D_FF = 2816
N_MOD = 9
EPS = 1e-6
IN_SSD = W_GROUP + SSD_CONV_DIM + SSD_HEADS
IN_LRU = 2 * LRU_W
IN_HG = 4 * HG_HEADS * HG_DK
IN_MLA = Q_LORA + KV_LORA + MLA_ROPE
OFF_LRU = IN_SSD
OFF_HG = OFF_LRU + IN_LRU
OFF_MLA = OFF_HG + IN_HG
N_IN = OFF_MLA + IN_MLA

LANES = 128
SUBLANES = 8
VMEM_BYTES_V7X = 64 * 1024 * 1024
VMEM_LIMIT = VMEM_BYTES_V7X * 3 // 4

PROJ_SSD = W_GROUP + SSD_CONV_DIM + LANES
PROJ_LRU = IN_LRU
PROJ_HG = IN_HG
PROJ_MLA = Q_LORA + KV_LORA + LANES
PROJ_ALL = PROJ_SSD + PROJ_LRU + PROJ_HG + PROJ_MLA


def _cparams(*sem):
    return pltpu.CompilerParams(dimension_semantics=sem, vmem_limit_bytes=VMEM_LIMIT)


def _silu(x):
    return x * jax.nn.sigmoid(x)


def _rms(x, w):
    return x * lax.rsqrt(jnp.mean(x * x, axis=-1, keepdims=True) + EPS) * w


def _lspec(arr, layer):
    nd = arr.ndim - 1
    return pl.BlockSpec((None,) + arr.shape[1:], lambda *_: (layer,) + (0,) * nd)


def _any_spec():
    return pl.BlockSpec(memory_space=pl.ANY)


def _ada_body(c_ref, w_ref, b_ref, o_ref):
    a = _silu(c_ref[...]).astype(BF16)
    o_ref[...] = jnp.dot(a, w_ref[...].astype(BF16), preferred_element_type=F32) + b_ref[...]


def ada_modulation(c_all, w_ada, b_ada, *, tn=1152):
    r, d = c_all.shape
    depth, _, n = w_ada.shape
    return pl.pallas_call(
        _ada_body,
        out_shape=jax.ShapeDtypeStruct((depth, r, n), F32),
        grid=(depth, n // tn),
        in_specs=[
            pl.BlockSpec((r, d), lambda l, j: (0, 0)),
            pl.BlockSpec((None, d, tn), lambda l, j: (l, 0, j)),
            pl.BlockSpec((None, 1, tn), lambda l, j: (l, 0, j)),
        ],
        out_specs=pl.BlockSpec((None, r, tn), lambda l, j: (l, 0, j)),
        compiler_params=_cparams("arbitrary", "arbitrary"),
        name="ada_modulation",
    )(c_all, w_ada, b_ada.reshape(depth, 1, n))


class _Tiling:
    def __init__(self, b, l, rows):
        if l >= rows:
            assert l % rows == 0
            self.gb, self.r = 1, rows
        else:
            gb = min(b, max(1, rows // l))
            assert b % gb == 0
            self.gb, self.r = gb, l
        self.b, self.l = b, l
        self.nb = b // self.gb
        self.nl = l // self.r
        self.steps = self.nb * self.nl
        self.m = self.gb * self.r

    def tok_index(self, i):
        return (i // self.nl, i % self.nl)


def _mod_spec(t, layer, row0, k):
    assert row0 % t.gb == 0
    return pl.BlockSpec((None, t.gb, 1, D_MODEL), lambda i, *_: (layer, row0 // t.gb + i // t.nl, 0, k))


def _ffn_body(x_ref, sh_ref, sc_ref, gt_ref, nw_ref, wg_ref, wu_ref, wout_ref, o_ref, h_ref, acc_ref, *, nj):
    j = pl.program_id(1)
    gb, r, d = x_ref.shape

    @pl.when(j == 0)
    def _():
        h = _rms(x_ref[...], nw_ref[...]) * (1.0 + sc_ref[...]) + sh_ref[...]
        h_ref[...] = h.reshape(gb * r, d).astype(BF16)
        acc_ref[...] = jnp.zeros_like(acc_ref)

    h = h_ref[...]
    g = jnp.dot(h, wg_ref[...], preferred_element_type=F32)
    u = jnp.dot(h, wu_ref[...], preferred_element_type=F32)
    acc_ref[...] += jnp.dot((_silu(g) * u).astype(BF16), wout_ref[...], preferred_element_type=F32)

    @pl.when(j == nj - 1)
    def _():
        o_ref[...] = x_ref[...] + 0.5 * gt_ref[...] * acc_ref[...].reshape(gb, r, d)


def ffn_halfstep(x, mods, layer, row0, k0, norm_w, w_in, w_out, *, rows=1024, tf=256):
    b, l, d = x.shape
    t = _Tiling(b, l, rows)
    nj = D_FF // tf
    xspec = pl.BlockSpec((t.gb, t.r, d), lambda i, j: (*t.tok_index(i), 0))
    return pl.pallas_call(
        functools.partial(_ffn_body, nj=nj),
        out_shape=jax.ShapeDtypeStruct(x.shape, F32),
        grid=(t.steps, nj),
        in_specs=[
            xspec,
            _mod_spec(t, layer, row0, k0),
            _mod_spec(t, layer, row0, k0 + 1),
            _mod_spec(t, layer, row0, k0 + 2),
            _lspec(norm_w, layer),
            pl.BlockSpec((None, d, tf), lambda i, j: (layer, 0, j)),
            pl.BlockSpec((None, d, tf), lambda i, j: (layer, 0, nj + j)),
            pl.BlockSpec((None, tf, d), lambda i, j: (layer, j, 0)),
        ],
        out_specs=xspec,
        scratch_shapes=[pltpu.VMEM((t.m, d), BF16), pltpu.VMEM((t.m, d), F32)],
        compiler_params=_cparams("arbitrary", "arbitrary"),
        name="ffn_halfstep",
    )(x, mods, mods, mods, norm_w, w_in, w_in, w_out)


def _pack_w_in(w_ref, wp_ref, kr_copies):
    d = w_ref.shape[0]
    rows = 256
    off_kr = OFF_MLA + Q_LORA + KV_LORA
    dst_mla = PROJ_SSD + PROJ_LRU + PROJ_HG

    def chunk(c, _):
        r = pl.ds(pl.multiple_of(c * rows, rows), rows)
        put = lambda dst, src, n: wp_ref.__setitem__((r, pl.ds(dst, n)), w_ref[r, pl.ds(src, n)].astype(BF16))
        zero = lambda dst, n: wp_ref.__setitem__((r, pl.ds(dst, n)), jnp.zeros((rows, n), BF16))
        put(0, 0, W_GROUP + SSD_CONV_DIM)
        zero(W_GROUP + SSD_CONV_DIM, LANES)
        put(W_GROUP + SSD_CONV_DIM, W_GROUP + SSD_CONV_DIM, SSD_HEADS)
        put(PROJ_SSD, OFF_LRU, IN_LRU)
        put(PROJ_SSD + PROJ_LRU, OFF_HG, IN_HG)
        put(dst_mla, OFF_MLA, Q_LORA + KV_LORA)
        zero(dst_mla + Q_LORA + KV_LORA, LANES)
        for k in range(kr_copies):
            put(dst_mla + Q_LORA + KV_LORA + k * MLA_ROPE, off_kr, MLA_ROPE)

    lax.fori_loop(0, d // rows, chunk, None)


def _proj_body(x_ref, sh_ref, sc_ref, nw_ref, w_ref, ssd_ref, lru_ref, hg_ref, mla_ref, wp_ref, *, kr_copies):
    @pl.when(pl.program_id(0) == 0)
    def _():
        _pack_w_in(w_ref, wp_ref, kr_copies)

    gb, r, d = x_ref.shape
    h = _rms(x_ref[...], nw_ref[...]) * (1.0 + sc_ref[...]) + sh_ref[...]
    h = h.reshape(gb * r, d).astype(BF16)
    off = 0
    for o_ref in (ssd_ref, lru_ref, hg_ref, mla_ref):
        n = o_ref.shape[-1]
        u = jnp.dot(h, wp_ref[:, off:off + n], preferred_element_type=F32)
        o_ref[...] = u.reshape(gb, r, n)
        off += n


def mixer_projection(x, mods, layer, row0, norm_w, w_in, kr_copies, *, rows=512):
    b, l, d = x.shape
    t = _Tiling(b, l, rows)
    widths = (PROJ_SSD, PROJ_LRU, PROJ_HG, PROJ_MLA)
    xspec = pl.BlockSpec((t.gb, t.r, d), lambda i: (*t.tok_index(i), 0))
    w_spec = pl.BlockSpec((None, d, N_IN), lambda i: (layer, 0, 0), pipeline_mode=pl.Buffered(1))
    return pl.pallas_call(
        functools.partial(_proj_body, kr_copies=kr_copies),
        out_shape=[jax.ShapeDtypeStruct((b, l, n), F32) for n in widths],
        grid=(t.steps,),
        in_specs=[xspec, _mod_spec(t, layer, row0, 3), _mod_spec(t, layer, row0, 4), _lspec(norm_w, layer), w_spec],
        out_specs=[pl.BlockSpec((t.gb, t.r, n), lambda i: (*t.tok_index(i), 0)) for n in widths],
        scratch_shapes=[pltpu.VMEM((d, PROJ_ALL), BF16)],
        compiler_params=_cparams("arbitrary"),
        name="mixer_projection",
    )(x, mods, mods, norm_w, w_in)


def _head_ones(n):
    i = lax.broadcasted_iota(jnp.int32, (n, n), 0) // HEAD_DIM
    j = lax.broadcasted_iota(jnp.int32, (n, n), 1) // HEAD_DIM
    return (i == j).astype(BF16)


def _out_body(x_ref, ya_ref, yb_ref, yc_ref, yd_ref, gt_ref, nw_ref, w_ref, o_ref):
    gb, r, d = x_ref.shape
    ones = _head_ones(W_GROUP)
    acc = jnp.zeros((gb * r, d), F32)
    for k, y_ref in enumerate((ya_ref, yb_ref, yc_ref, yd_ref)):
        y = y_ref[...].reshape(gb * r, W_GROUP)
        ss = jnp.dot((y * y).astype(BF16), ones, preferred_element_type=F32)
        yn = y * lax.rsqrt(ss * (1.0 / HEAD_DIM) + EPS) * nw_ref[:, k * W_GROUP:(k + 1) * W_GROUP]
        acc += jnp.dot(yn.astype(BF16), w_ref[k * W_GROUP:(k + 1) * W_GROUP, :], preferred_element_type=F32)
    o_ref[...] = x_ref[...] + gt_ref[...] * acc.reshape(gb, r, d)


def output_merge(x, ys, mods, layer, row0, out_norm, w_out, *, rows=512):
    b, l, d = x.shape
    t = _Tiling(b, l, rows)
    xspec = pl.BlockSpec((t.gb, t.r, d), lambda i: (*t.tok_index(i), 0))
    yspec = pl.BlockSpec((t.gb, t.r, W_GROUP), lambda i: (*t.tok_index(i), 0))
    return pl.pallas_call(
        _out_body,
        out_shape=jax.ShapeDtypeStruct(x.shape, F32),
        grid=(t.steps,),
        in_specs=[xspec, yspec, yspec, yspec, yspec, _mod_spec(t, layer, row0, 5),
                  _lspec(out_norm, layer), _lspec(w_out, layer)],
        out_specs=xspec,
        compiler_params=_cparams("arbitrary"),
        name="output_merge",
    )(x, *ys, mods, out_norm, w_out)


def _cumsum_rows(x):
    n = x.shape[0]
    row = lax.broadcasted_iota(jnp.int32, x.shape, 0)
    d = 1
    while d < n:
        x = x + jnp.where(row >= d, pltpu.roll(x, d, 0), 0.0)
        d *= 2
    return x


def _dot_nt(a, b):
    return lax.dot_general(a, b, (((1,), (1,)), ((), ())), preferred_element_type=F32)


def _dot_tn(a, b):
    return lax.dot_general(a, b, (((0,), (0,)), ((), ())), preferred_element_type=F32)


def _causal_conv(ext_ref, s, x, cw_ref, cb_ref, first, q):
    @pl.when(jnp.logical_not(first))
    def _():
        ext_ref[s, pl.ds(0, SUBLANES), :] = ext_ref[s, pl.ds(q, SUBLANES), :]

    ext_ref[s, pl.ds(SUBLANES, q), :] = x
    out = cb_ref[...]
    for k in range(CONV_W):
        out = out + cw_ref[k:k + 1, :] * ext_ref[s, pl.ds(SUBLANES - (CONV_W - 1) + k, q), :]
    return out


def _state_specs(gb, init_layer, layer, *shapes):
    ins = [pl.BlockSpec((None, gb) + s, lambda i, c, n=len(s): (init_layer, i) + (0,) * n) for s in shapes]
    outs = [pl.BlockSpec((None, gb) + s, lambda i, c, n=len(s): (layer, i) + (0,) * n) for s in shapes]
    return ins, outs


def _ssd_body(u_ref, cs_ref, st_ref, cw_ref, cb_ref, dtb_ref, alog_ref, dsk_ref, _c_alias, _s_alias,
              y_ref, cnew_ref, snew_ref, ext_ref, h_ref, *, q, nsteps):
    step = pl.program_id(1)
    gb, tb, _ = u_ref.shape
    hp = SSD_HEADS // SSD_GROUPS * SSD_P

    @pl.when(step == 0)
    def _():
        ext_ref[:, pl.ds(SUBLANES - (CONV_W - 1), CONV_W - 1), :] = cs_ref[...]
        h_ref[...] = st_ref[...]

    row = lax.broadcasted_iota(jnp.int32, (q, q), 0)
    col = lax.broadcasted_iota(jnp.int32, (q, q), 1)
    causal = row >= col
    a = -jnp.exp(alog_ref[...])

    def chunk(s, i):
        first = jnp.logical_and(step == 0, i == 0)
        tok = pl.ds(pl.multiple_of(i * q, q), q)
        u = u_ref[s, tok, :]
        z = u[:, :W_GROUP]
        conv = _causal_conv(ext_ref, s, u[:, W_GROUP:W_GROUP + SSD_CONV_DIM], cw_ref, cb_ref, first, q)
        xbc = _silu(conv)
        xs = xbc[:, :W_GROUP]
        bm = xbc[:, W_GROUP:W_GROUP + SSD_GROUPS * SSD_N].astype(BF16)
        cm = xbc[:, W_GROUP + SSD_GROUPS * SSD_N:].astype(BF16)
        dt = jax.nn.softplus(u[:, W_GROUP + SSD_CONV_DIM:] + dtb_ref[...])
        acum = _cumsum_rows(dt * a)
        acum_t = acum.T
        a_end = acum[q - 1:q, :]
        ys = []
        for g in range(SSD_GROUPS):
            bg = bm[:, g * SSD_N:(g + 1) * SSD_N]
            cg = cm[:, g * SSD_N:(g + 1) * SSD_N]
            cb = _dot_nt(cg, bg)
            h_prev = h_ref[s, g * hp:(g + 1) * hp, :]
            y_off = _dot_nt(cg, h_prev.astype(BF16))
            for j in range(SSD_HEADS // SSD_GROUPS):
                h = g * (SSD_HEADS // SSD_GROUPS) + j
                dt_h = dt[:, h:h + 1]
                ac_h = acum[:, h:h + 1]
                ae_h = a_end[:, h:h + 1]
                decay = jnp.exp(jnp.where(causal, ac_h - acum_t[h:h + 1, :], -jnp.inf))
                xs_h = xs[:, h * SSD_P:(h + 1) * SSD_P]
                xdt_h = xs_h * dt_h
                y_h = jnp.dot((cb * decay).astype(BF16), xdt_h.astype(BF16), preferred_element_type=F32)
                y_h = y_h + y_off[:, j * SSD_P:(j + 1) * SSD_P] * jnp.exp(ac_h)
                y_h = y_h + dsk_ref[:, h * SSD_P:(h + 1) * SSD_P] * xs_h
                ys.append(y_h)
                w_h = (xdt_h * jnp.exp(ae_h - ac_h)).astype(BF16)
                st_h = _dot_tn(w_h, bg)
                rows = pl.ds(h * SSD_P, SSD_P)
                h_ref[s, rows, :] = jnp.exp(ae_h) * h_ref[s, rows, :] + st_h
        y_ref[s, tok, :] = jnp.concatenate(ys, axis=1) * _silu(z)

    for s in range(gb):
        if tb == q:
            chunk(s, 0)
        else:
            lax.fori_loop(0, tb // q, lambda i, _: chunk(s, i), None)

    @pl.when(step == nsteps - 1)
    def _():
        cnew_ref[...] = ext_ref[:, pl.ds(q + SUBLANES - (CONV_W - 1), CONV_W - 1), :]
        snew_ref[...] = h_ref[...]


def ssd_mixer(u, conv_state, ssm_state, init_layer, p, layer, conv_out, ssm_out, *, gb=1, rows=512):
    b, l, _ = u.shape
    q = min(SSD_CHUNK, l)
    tb = min(rows, l)
    assert l % tb == 0 and tb % q == 0 and b % gb == 0 and q % SUBLANES == 0
    nsteps = l // tb
    nrow = SSD_HEADS * SSD_P
    st_in, st_out = _state_specs(gb, init_layer, layer, (CONV_W - 1, SSD_CONV_DIM), (nrow, SSD_N))
    params = [p['ssd_conv_w'], p['ssd_conv_b'], p['ssd_dt_bias'], p['ssd_a_log'], p['ssd_d']]
    return pl.pallas_call(
        functools.partial(_ssd_body, q=q, nsteps=nsteps),
        out_shape=[jax.ShapeDtypeStruct((b, l, W_GROUP), F32),
                   jax.ShapeDtypeStruct(conv_out.shape, F32), jax.ShapeDtypeStruct(ssm_out.shape, F32)],
        grid=(b // gb, nsteps),
        in_specs=[pl.BlockSpec((gb, tb, PROJ_SSD), lambda i, c: (i, c, 0)), *st_in,
                  *[_lspec(a, layer) for a in params], _any_spec(), _any_spec()],
        out_specs=[pl.BlockSpec((gb, tb, W_GROUP), lambda i, c: (i, c, 0)), *st_out],
        scratch_shapes=[pltpu.VMEM((gb, q + SUBLANES, SSD_CONV_DIM), F32), pltpu.VMEM((gb, nrow, SSD_N), F32)],
        input_output_aliases={8: 1, 9: 2},
        compiler_params=_cparams("arbitrary", "arbitrary"),
        name="ssd_mixer",
    )(u, conv_state, ssm_state, *params, conv_out, ssm_out)


def _linear_scan_rows(a, b):
    n = a.shape[0]
    row = lax.broadcasted_iota(jnp.int32, a.shape, 0)
    d = 1
    while d < n:
        m = row >= d
        b = jnp.where(m, a * pltpu.roll(b, d, 0) + b, b)
        a = jnp.where(m, a * pltpu.roll(a, d, 0), a)
        d *= 2
    return a, b


def _lru_body(u_ref, cs_ref, h0_ref, cw_ref, cb_ref, wri_ref, bri_ref, lam_ref, _c_alias, _h_alias,
              y_ref, cnew_ref, hnew_ref, ext_ref, h_ref, *, q, nsteps):
    step = pl.program_id(1)
    gb, tb, _ = u_ref.shape

    @pl.when(step == 0)
    def _():
        ext_ref[:, pl.ds(SUBLANES - (CONV_W - 1), CONV_W - 1), :] = cs_ref[...]
        h_ref[...] = h0_ref[...]

    sp = jax.nn.softplus(-lam_ref[...])

    def chunk(s, i):
        first = jnp.logical_and(step == 0, i == 0)
        tok = pl.ds(pl.multiple_of(i * q, q), q)
        u = u_ref[s, tok, :]
        xb = _causal_conv(ext_ref, s, u[:, :LRU_W], cw_ref, cb_ref, first, q)
        ri = jnp.dot(xb.astype(BF16), wri_ref[...], preferred_element_type=F32) + bri_ref[...]
        r = jax.nn.sigmoid(ri[:, :LRU_W])
        i = jax.nn.sigmoid(ri[:, LRU_W:])
        log_a = -LRU_C * r * sp
        a = jnp.exp(log_a)
        bterm = jnp.sqrt(-jnp.tanh(log_a) * (a * a + 1.0)) * (i * xb)
        a_cum, h = _linear_scan_rows(a, bterm)
        h = h + a_cum * h_ref[s]
        h_ref[s] = h[q - 1:q, :]
        y_ref[s, tok, :] = h * jax.nn.gelu(u[:, LRU_W:])

    for s in range(gb):
        if tb == q:
            chunk(s, 0)
        else:
            lax.fori_loop(0, tb // q, lambda i, _: chunk(s, i), None)

    @pl.when(step == nsteps - 1)
    def _():
        cnew_ref[...] = ext_ref[:, pl.ds(q + SUBLANES - (CONV_W - 1), CONV_W - 1), :]
        hnew_ref[...] = h_ref[...]


def rglru_mixer(u, conv_state, h0, init_layer, p, layer, conv_out, h_out, *, gb=1, chunk=256, rows=1024):
    b, l, _ = u.shape
    q = min(chunk, l)
    tb = min(rows, l)
    assert l % tb == 0 and tb % q == 0 and b % gb == 0 and q % SUBLANES == 0
    nsteps = l // tb
    st_in, st_out = _state_specs(gb, init_layer, layer, (CONV_W - 1, LRU_W), (1, LRU_W))
    params = [p['lru_conv_w'], p['lru_conv_b'], p['lru_w_ri'], p['lru_b_ri'], p['lru_lambda']]
    return pl.pallas_call(
        functools.partial(_lru_body, q=q, nsteps=nsteps),
        out_shape=[jax.ShapeDtypeStruct((b, l, LRU_W), F32),
                   jax.ShapeDtypeStruct(conv_out.shape, F32), jax.ShapeDtypeStruct(h_out.shape, F32)],
        grid=(b // gb, nsteps),
        in_specs=[pl.BlockSpec((gb, tb, PROJ_LRU), lambda i, c: (i, c, 0)), *st_in,
                  *[_lspec(a, layer) for a in params], _any_spec(), _any_spec()],
        out_specs=[pl.BlockSpec((gb, tb, LRU_W), lambda i, c: (i, c, 0)), *st_out],
        scratch_shapes=[pltpu.VMEM((gb, q + SUBLANES, LRU_W), F32), pltpu.VMEM((gb, 1, LRU_W), F32)],
        input_output_aliases={8: 1, 9: 2},
        compiler_params=_cparams("arbitrary", "arbitrary"),
        name="rglru_mixer",
    )(u, conv_state, h0, *params, conv_out, h_out)


def _hgrn_levels(c):
    ws, w = [], c // 2
    while w >= 1:
        ws.append(w)
        w //= 2
    return ws


def _boundary_rows(g, w):
    c, n = g.shape
    if 2 * w >= SUBLANES:
        gr = g.reshape(c // (2 * w), 2 * w, n)[:, w - 1:w, :]
        return jnp.broadcast_to(gr, (c // (2 * w), 2 * w, n)).reshape(c, n)
    r = lax.broadcasted_iota(jnp.int32, g.shape, 0) % (2 * w)
    out = g
    for delta in range(-(w - 1), w + 1):
        if delta != 0:
            out = jnp.where(r - (w - 1) == delta, pltpu.roll(g, delta % c, 0), out)
    return out


def _hgrn_body(u_ref, st0_ref, lbp_ref, _alias, y_ref, stn_ref, st_ref, *, c, nsteps, layer):
    step = pl.program_id(1)
    gb, tb, _ = u_ref.shape
    nk = HG_HEADS * HG_DK
    hv = lax.broadcasted_iota(jnp.int32, (nk, nk), 0) // HG_DV
    hk = lax.broadcasted_iota(jnp.int32, (nk, nk), 1) // HG_DK
    state_mask = hv == hk

    @pl.when(step == 0)
    def _():
        for s in range(gb):
            t = st0_ref[s].reshape(nk, HG_DV).T
            st_ref[s] = jnp.where(state_mask, jnp.concatenate([t] * HG_HEADS, axis=0), 0.0)

    p = lbp_ref[...]
    e = jnp.exp(p - jnp.max(p, axis=0, keepdims=True))
    lb = jnp.zeros((1, nk), F32)
    for i in range(1, layer + 1):
        lb = lb + e[i:i + 1, :]
    lb = lb / jnp.sum(e, axis=0, keepdims=True)
    log_lb = jnp.log(lb)
    log_1mlb = jnp.log1p(-lb)

    t_idx = lax.broadcasted_iota(jnp.int32, (c, nk), 0)
    row_s = lax.broadcasted_iota(jnp.int32, (HG_HEADS * c, nk), 0)
    lane = lax.broadcasted_iota(jnp.int32, (HG_HEADS * c, nk), 1)
    head_rows = (row_s // c) == (lane // HG_DK)
    t_sc = lax.broadcasted_iota(jnp.int32, (c, HG_HEADS * c), 0)
    s_sc = lax.broadcasted_iota(jnp.int32, (c, HG_HEADS * c), 1) % c

    def tile_heads(x):
        xt = jnp.concatenate([x.astype(BF16)] * HG_HEADS, axis=0)
        return jnp.where(head_rows, xt, jnp.zeros_like(xt))

    def chunk(s, i):
        rows = pl.ds(pl.multiple_of(i * c, c), c)
        u = u_ref[s, rows, :]
        q = _silu(u[:, :nk])
        fz = u[:, nk:2 * nk]
        v = u[:, 2 * nk:3 * nk]
        gate = u[:, 3 * nk:]
        lf = jnp.logaddexp(log_lb, log_1mlb + jax.nn.log_sigmoid(fz))
        k = (1.0 - lb) * jax.nn.sigmoid(-fz)
        g = _cumsum_rows(lf)
        g_end = g[c - 1:c, :]
        st = st_ref[s]
        o = _dot_nt((q * jnp.exp(g)).astype(BF16), st.astype(BF16))
        sc = jnp.where(t_sc == s_sc, _dot_nt(q.astype(BF16), tile_heads(k)), 0.0)
        for w in _hgrn_levels(c):
            gr = _boundary_rows(g, w)
            upper = (t_idx // w) % 2 == 1
            ef = jnp.exp(jnp.where(upper, g - gr, gr - g))
            qt = jnp.where(upper, q * ef, 0.0)
            kt = jnp.where(upper, 0.0, k * ef)
            blk = (t_sc // (2 * w)) == (s_sc // (2 * w))
            sc = sc + jnp.where(blk, _dot_nt(qt.astype(BF16), tile_heads(kt)), 0.0)
        o = o + jnp.dot(sc.astype(BF16), tile_heads(v), preferred_element_type=F32)
        y_ref[s, rows, :] = o * _silu(gate)
        upd = _dot_tn(v.astype(BF16), (k * jnp.exp(g_end - g)).astype(BF16))
        st_ref[s] = st * jnp.exp(g_end) + jnp.where(state_mask, upd, 0.0)

    for s in range(gb):
        if tb == c:
            chunk(s, 0)
        else:
            lax.fori_loop(0, tb // c, lambda i, _: chunk(s, i), None)

    @pl.when(step == nsteps - 1)
    def _():
        for s in range(gb):
            st = jnp.where(state_mask, st_ref[s], 0.0)
            t = st[:HG_DV]
            for h in range(1, HG_HEADS):
                t = t + st[h * HG_DV:(h + 1) * HG_DV]
            stn_ref[s] = t.T.reshape(HG_HEADS, HG_DK, HG_DV)


def hgrn2_mixer(u, state, init_layer, lb_param, layer, state_out, *, gb=1, rows=512):
    b, l, _ = u.shape
    c = min(HG_CHUNK, l)
    tb = min(rows, l)
    assert l % tb == 0 and tb % c == 0 and b % gb == 0
    nsteps = l // tb
    nk = HG_HEADS * HG_DK
    st_in, st_out = _state_specs(gb, init_layer, layer, (HG_HEADS, HG_DK, HG_DV))
    return pl.pallas_call(
        functools.partial(_hgrn_body, c=c, nsteps=nsteps, layer=layer),
        out_shape=[jax.ShapeDtypeStruct((b, l, nk), F32), jax.ShapeDtypeStruct(state_out.shape, F32)],
        grid=(b // gb, nsteps),
        in_specs=[pl.BlockSpec((gb, tb, PROJ_HG), lambda i, j: (i, j, 0)), *st_in,
                  pl.BlockSpec(lb_param.shape, lambda i, j: (0, 0)), _any_spec()],
        out_specs=[pl.BlockSpec((gb, tb, nk), lambda i, j: (i, j, 0)), *st_out],
        scratch_shapes=[pltpu.VMEM((gb, nk, nk), F32)],
        input_output_aliases={3: 1},
        compiler_params=_cparams("arbitrary", "arbitrary"),
        name="hgrn2_mixer",
    )(u, state, lb_param, state_out)


MLA_BLOCK = LANES
MLA_QK = MLA_HEADS * MLA_BLOCK
MLA_NOPE_OFF = MLA_BLOCK - MLA_NOPE


def _segment_ones():
    seg = lambda i: jnp.where(i < MLA_ROPE, 0, jnp.where(i < MLA_NOPE_OFF, 1, 2))
    i = seg(lax.broadcasted_iota(jnp.int32, (MLA_BLOCK, MLA_BLOCK), 0))
    j = seg(lax.broadcasted_iota(jnp.int32, (MLA_BLOCK, MLA_BLOCK), 1))
    return (i == j).astype(BF16)


def _blocked_rms(x, ones, inv_cnt, gain):
    parts = []
    for h in range(MLA_HEADS):
        xb = x[:, h * MLA_BLOCK:(h + 1) * MLA_BLOCK]
        ss = jnp.dot((xb * xb).astype(BF16), ones, preferred_element_type=F32)
        parts.append(xb * lax.rsqrt(ss * inv_cnt + EPS))
    return jnp.concatenate(parts, axis=1) * gain


def _rope(x, cos, sin_lo, sin_hi):
    n = x.shape[-1]
    half = MLA_ROPE // 2
    return x * cos + pltpu.roll(x, n - half, 1) * sin_lo + pltpu.roll(x, half, 1) * sin_hi


def _paged_body(pt_ref, qn_ref, qr_ref, latn_ref, kr4n_ref, wkt_ref, wv_ref, clat_hbm, ckr_hbm, o_ref,
                lat_buf, kr_buf, latbf_ref, s_ref, sem, *, layer, n_pages, chunk_pages):
    b = pl.program_id(0)
    nb = pl.num_programs(0)
    t_new = qn_ref.shape[0]
    hq = MLA_HEADS * t_new
    nk = MLA_HEADS * MLA_NOPE
    n_tok = chunk_pages * PAGE_SIZE
    issue_unroll = 8

    def page_copies(seq, slot, p):
        page = pt_ref[seq, p]
        return (pltpu.make_async_copy(clat_hbm.at[layer, page], lat_buf.at[slot, p], sem.at[slot, 0]),
                pltpu.make_async_copy(ckr_hbm.at[layer, page], kr_buf.at[slot, p], sem.at[slot, 1]))

    def fetch(seq, slot):
        def body(g, _):
            for k in range(issue_unroll):
                for cp in page_copies(seq, slot, g * issue_unroll + k):
                    cp.start()
        lax.fori_loop(0, n_pages // issue_unroll, body, None)

    @pl.when(b == 0)
    def _():
        fetch(0, 0)

    @pl.when(b + 1 < nb)
    def _():
        fetch(b + 1, (b + 1) % 2)

    tile_rows = lambda x: jnp.concatenate([x] * MLA_HEADS, axis=0)
    own_n = (lax.broadcasted_iota(jnp.int32, (hq, nk), 0) // t_new
             == lax.broadcasted_iota(jnp.int32, (hq, nk), 1) // MLA_NOPE)
    own_r = (lax.broadcasted_iota(jnp.int32, (hq, LANES), 0) // t_new
             == lax.broadcasted_iota(jnp.int32, (hq, LANES), 1) // MLA_ROPE)
    qn_sel = jnp.where(own_n, tile_rows(qn_ref[...]), 0.0).astype(BF16)
    wkt = wkt_ref[...]
    q_abs = jnp.dot(qn_sel, wkt, preferred_element_type=F32).astype(BF16)
    qr_sel = jnp.where(own_r, tile_rows(qr_ref[...]), 0.0).astype(BF16)
    lhs = jnp.concatenate([wkt, q_abs], axis=0)

    def scores(lat_bf, kr4_bf):
        n = lat_bf.shape[0]
        both = _dot_nt(lhs, lat_bf)
        sq = both[:nk] * both[:nk]
        ssq = []
        for h in range(MLA_HEADS):
            part = sq[h * MLA_NOPE:(h + 1) * MLA_NOPE].reshape(MLA_NOPE // SUBLANES, SUBLANES, n).sum(axis=0)
            shift = SUBLANES // 2
            while shift >= 1:
                part = part + pltpu.roll(part, shift, 0)
                shift //= 2
            ssq.append(part)
        ssq = jnp.concatenate(ssq, axis=0)
        s_rope = jnp.dot(qr_sel, kr4_bf, preferred_element_type=F32)
        return (both[nk:] * lax.rsqrt(ssq * (1.0 / MLA_NOPE) + EPS) + s_rope) * MLA_SCALE

    latn_bf = latn_ref[...].astype(BF16)
    s_new = scores(latn_bf, kr4n_ref[...].T.astype(BF16))
    t_q = lax.broadcasted_iota(jnp.int32, (hq, t_new), 0) % t_new
    t_k = lax.broadcasted_iota(jnp.int32, (hq, t_new), 1)
    s_new = jnp.where(t_k <= t_q, s_new, -jnp.inf)
    m = jnp.max(s_new, axis=-1, keepdims=True)

    slot = b % 2
    for p in range(n_pages):
        for cp in page_copies(b, slot, p):
            cp.wait()

    for c in range(n_pages // chunk_pages):
        p0 = c * chunk_pages
        lat_bf = lat_buf[slot, p0:p0 + chunk_pages].reshape(n_tok, KV_LORA).astype(BF16)
        latbf_ref[c * n_tok:(c + 1) * n_tok, :] = lat_bf
        kr_t = jnp.concatenate([kr_buf[slot, p0 + i] for i in range(chunk_pages)], axis=1).astype(BF16)
        s = scores(lat_bf, tile_rows(kr_t))
        s_ref[:, c * n_tok:(c + 1) * n_tok] = s
        m = jnp.maximum(m, jnp.max(s, axis=-1, keepdims=True))

    p_old = jnp.exp(s_ref[...] - m)
    p_new = jnp.exp(s_new - m)
    l = jnp.sum(p_old, axis=-1, keepdims=True) + jnp.sum(p_new, axis=-1, keepdims=True)
    acc = jnp.dot(p_old.astype(BF16), latbf_ref[...], preferred_element_type=F32)
    acc = acc + jnp.dot(p_new.astype(BF16), latn_bf, preferred_element_type=F32)
    ov = jnp.dot((acc / l).astype(BF16), wv_ref[...], preferred_element_type=F32)
    out = jnp.zeros((t_new, MLA_HEADS * MLA_V), F32)
    lane_v = lax.broadcasted_iota(jnp.int32, (t_new, MLA_HEADS * MLA_V), 1) // MLA_V
    for h in range(MLA_HEADS):
        out = out + jnp.where(lane_v == h, ov[h * t_new:(h + 1) * t_new, :], 0.0)
    o_ref[...] = out


def mla_sample_attention(qn, qr, lat_all, kr4_new, cache_lat, cache_kr_t, page_table, layer, p, *, chunk_pages=8):
    b, t_new, _ = qn.shape
    n_pages = page_table.shape[1]
    assert n_pages % chunk_pages == 0 and n_pages % 8 == 0
    assert t_new == SUBLANES
    n_past = n_pages * PAGE_SIZE
    seq = lambda n: pl.BlockSpec((None, t_new, n), lambda i, pt: (i, 0, 0))
    lspec = lambda a: pl.BlockSpec((None,) + a.shape[1:], lambda i, pt: (layer, 0, 0))
    return pl.pallas_call(
        functools.partial(_paged_body, layer=layer, n_pages=n_pages, chunk_pages=chunk_pages),
        out_shape=jax.ShapeDtypeStruct((b, t_new, MLA_HEADS * MLA_V), F32),
        grid_spec=pltpu.PrefetchScalarGridSpec(
            num_scalar_prefetch=1,
            grid=(b,),
            in_specs=[seq(MLA_HEADS * MLA_NOPE), seq(LANES),
                      pl.BlockSpec((None, None, t_new, KV_LORA), lambda i, pt: (layer, i, 0, 0)), seq(LANES),
                      lspec(p['mla_wk_t']), lspec(p['mla_wv']), _any_spec(), _any_spec()],
            out_specs=seq(MLA_HEADS * MLA_V),
            scratch_shapes=[pltpu.VMEM((2, n_pages, PAGE_SIZE, KV_LORA), F32),
                            pltpu.VMEM((2, n_pages, MLA_ROPE, PAGE_SIZE), F32),
                            pltpu.VMEM((n_past, KV_LORA), BF16),
                            pltpu.VMEM((MLA_HEADS * t_new, n_past), F32),
                            pltpu.SemaphoreType.DMA((2, 2))],
        ),
        compiler_params=_cparams("arbitrary"),
        name="mla_sample_attention",
    )(page_table, qn, qr, lat_all, kr4_new, p['mla_wk_t'], p['mla_wv'], cache_lat, cache_kr_t)


def _mla_proj_sample_body(u_ref, cos_ref, slo_ref, shi_ref, qnorm_ref, wq_ref, gn_ref, gr_ref, kvnorm_ref, gkr_ref,
                          _lat_alias, _kr_alias, qn_ref, qr_ref, kr4_ref, lat_ref, kr_ref):
    gb, r, _ = u_ref.shape
    m = gb * r
    nk = MLA_HEADS * MLA_NOPE
    u = u_ref[...]
    tab = lambda t_ref: jnp.broadcast_to(t_ref[...], (gb, r, LANES)).reshape(m, LANES)
    cos, slo, shi = tab(cos_ref), tab(slo_ref), tab(shi_ref)
    i = lax.broadcasted_iota(jnp.int32, (LANES, LANES), 0) // MLA_ROPE
    j = lax.broadcasted_iota(jnp.int32, (LANES, LANES), 1) // MLA_ROPE
    ones_r = (i == j).astype(BF16)

    cq = _rms(u[..., :Q_LORA], qnorm_ref[...]).reshape(m, Q_LORA)
    q = jnp.dot(cq.astype(BF16), wq_ref[...], preferred_element_type=F32)
    qn, qr = q[:, :nk], q[:, nk:]
    ssn = jnp.dot((qn * qn).astype(BF16), _head_ones(nk), preferred_element_type=F32)
    qn_ref[...] = (qn * lax.rsqrt(ssn * (1.0 / MLA_NOPE) + EPS) * gn_ref[...]).reshape(gb, r, nk)
    ssr = jnp.dot((qr * qr).astype(BF16), ones_r, preferred_element_type=F32)
    qr = qr * lax.rsqrt(ssr * (1.0 / MLA_ROPE) + EPS) * gr_ref[...]
    qr_ref[...] = _rope(qr, cos, slo, shi).reshape(gb, r, LANES)

    lat_ref[...] = _rms(u[..., Q_LORA:Q_LORA + KV_LORA], kvnorm_ref[...])
    kr = u[..., Q_LORA + KV_LORA:].reshape(m, LANES)
    kr = kr * lax.rsqrt(jnp.mean(kr * kr, axis=-1, keepdims=True) + EPS) * gkr_ref[...]
    kr = _rope(kr, cos, slo, shi)
    kr4_ref[...] = kr.reshape(gb, r, LANES)
    kr_ref[...] = kr[:, :MLA_ROPE].reshape(gb, r, MLA_ROPE)


def _mla_proj_body(u_ref, cos_ref, slo_ref, shi_ref, qnorm_ref, wq_ref, gq_ref, kvnorm_ref, gkr_ref,
                   wkv_ref, gk_ref, _lat_alias, _kr_alias, q_ref, k_ref, v_ref, lat_ref, kr_ref):
    gb, r, _ = u_ref.shape
    m = gb * r
    u = u_ref[...]
    tab = lambda t_ref: jnp.broadcast_to(t_ref[...], (gb, r, MLA_BLOCK)).reshape(m, MLA_BLOCK)
    cos, slo, shi = tab(cos_ref), tab(slo_ref), tab(shi_ref)
    tile = lambda t: jnp.concatenate([t] * MLA_HEADS, axis=1)
    ones = _segment_ones()
    lane = lax.broadcasted_iota(jnp.int32, (1, MLA_BLOCK), 1)
    inv_cnt = jnp.where(lane < MLA_NOPE_OFF, 1.0 / MLA_ROPE, 1.0 / MLA_NOPE)

    cq = _rms(u[..., :Q_LORA], qnorm_ref[...]).reshape(m, Q_LORA)
    q = jnp.dot(cq.astype(BF16), wq_ref[...], preferred_element_type=F32)
    q = _blocked_rms(q, ones, inv_cnt, gq_ref[...])
    q = _rope(q, tile(cos), tile(slo), tile(shi))
    q_ref[...] = q.reshape(gb, r, MLA_QK).astype(q_ref.dtype)

    lat = _rms(u[..., Q_LORA:Q_LORA + KV_LORA], kvnorm_ref[...])
    lat_ref[...] = lat
    kr = u[..., Q_LORA + KV_LORA:].reshape(m, MLA_BLOCK)
    kr = kr * lax.rsqrt(jnp.sum(kr * kr, axis=-1, keepdims=True) * (1.0 / MLA_ROPE) + EPS) * gkr_ref[...]
    kr = _rope(kr, cos, slo, shi)
    kr_ref[...] = kr[:, :MLA_ROPE].reshape(gb, r, MLA_ROPE)

    kv = jnp.dot(lat.reshape(m, KV_LORA).astype(BF16), wkv_ref[...], preferred_element_type=F32)
    k = _blocked_rms(kv[:, :MLA_QK], ones, inv_cnt, gk_ref[...]) + tile(kr)
    k_ref[...] = k.reshape(gb, r, MLA_QK).astype(k_ref.dtype)
    lane_v = lax.broadcasted_iota(jnp.int32, (1, MLA_QK), 1) % MLA_BLOCK
    v = kv[:, MLA_QK:] + jnp.where(lane_v == MLA_V, 1.0, 0.0)
    v_ref[...] = v.reshape(gb, r, MLA_QK).astype(v_ref.dtype)


def _rope_tables(pos0, l, copies):
    half = MLA_ROPE // 2
    inv_freq = ROPE_THETA ** (-jnp.arange(half, dtype=F32) / half)
    ang = (pos0 + jnp.arange(l)).astype(F32)[:, None] * inv_freq
    cos, sin = jnp.cos(ang), jnp.sin(ang)
    rest = LANES - copies * MLA_ROPE
    lay = lambda t, fill: jnp.concatenate([jnp.tile(t, (1, copies)), jnp.full((l, rest), fill, F32)], axis=1)
    zero = jnp.zeros_like(sin)
    return (lay(jnp.concatenate([cos, cos], 1), 1.0), lay(jnp.concatenate([-sin, zero], 1), 0.0),
            lay(jnp.concatenate([zero, sin], 1), 0.0))


def _head_blocked(rope_part, nope_part):
    ref = rope_part if rope_part is not None else nope_part
    lead = ref.shape[:-2]
    cols = []
    for h in range(MLA_HEADS):
        rp = rope_part[..., h, :] if rope_part is not None else jnp.zeros(lead + (MLA_ROPE,), F32)
        npart = nope_part[..., h, :] if nope_part is not None else jnp.zeros(lead + (MLA_NOPE,), F32)
        cols += [rp, jnp.zeros(lead + (MLA_NOPE_OFF - MLA_ROPE,), F32), npart]
    return jnp.concatenate(cols, axis=-1)


def _mla_project_call(body, name, u, tables, params, layer, outs, stacks, rows):
    b, l, _ = u.shape
    t = _Tiling(b, l, rows)
    uspec = lambda n: pl.BlockSpec((t.gb, t.r, n), lambda i: (*t.tok_index(i), 0))
    tspec = pl.BlockSpec((t.r, LANES), lambda i: (i % t.nl, 0))
    sspec = lambda a: pl.BlockSpec((None, t.gb, t.r, a.shape[-1]), lambda i: (layer, *t.tok_index(i), 0))
    n_in = 4 + len(params)
    return pl.pallas_call(
        body,
        out_shape=[jax.ShapeDtypeStruct((b, l, n), dt) for n, dt in outs]
                  + [jax.ShapeDtypeStruct(a.shape, a.dtype) for a in stacks],
        grid=(t.steps,),
        in_specs=[uspec(PROJ_MLA), tspec, tspec, tspec] + [_lspec(a, layer) for a in params]
                 + [_any_spec() for _ in stacks],
        out_specs=[uspec(n) for n, _ in outs] + [sspec(a) for a in stacks],
        input_output_aliases={n_in + k: len(outs) + k for k in range(len(stacks))},
        compiler_params=_cparams("arbitrary"),
        name=name,
    )(u, *tables, *params, *stacks)


def mla_project_prompt(u, tables, p, layer, lat_out, kr_out, *, rows=512):
    params = [p['mla_q_norm'], p['mla_wq_blocked'], p['mla_gq_blocked'], p['mla_kv_norm'], p['mla_gkr_pad'],
              p['mla_wkv_blocked'], p['mla_gk_blocked']]
    return _mla_project_call(_mla_proj_body, "mla_project_prompt", u, tables, params, layer,
                             [(MLA_QK, BF16)] * 3, [lat_out, kr_out], rows)


def mla_project_sample(u, tables, p, layer, lat_out, kr_out, *, rows=512):
    params = [p['mla_q_norm'], p['mla_wq_compact'], p['mla_gn_compact'], p['mla_gr_compact'], p['mla_kv_norm'],
              p['mla_gkr_tiled']]
    return _mla_project_call(_mla_proj_sample_body, "mla_project_sample", u, tables, params, layer,
                             [(MLA_HEADS * MLA_NOPE, F32), (LANES, F32), (LANES, F32)], [lat_out, kr_out], rows)


def _flash_body(q_ref, k_ref, v_ref, o_ref, *, tq):
    i = pl.program_id(1)
    row = lax.broadcasted_iota(jnp.int32, (tq, tq), 0)
    col = lax.broadcasted_iota(jnp.int32, (tq, tq), 1)
    c = MLA_SCALE * math.log2(math.e)
    blk = lambda h: pl.ds(h * MLA_BLOCK, MLA_BLOCK)
    qs = [q_ref[:, blk(h)] for h in range(MLA_HEADS)]

    def update(carry, rows, mask):
        new = []
        for h in range(MLA_HEADS):
            m, acc = carry[h]
            s = _dot_nt(qs[h], k_ref[rows, blk(h)])
            if mask:
                s = jnp.where(row >= col, s, -jnp.inf)
            m_new = jnp.maximum(m, jnp.max(s, axis=-1, keepdims=True))
            alpha = jnp.exp2((m - m_new) * c)
            p = jnp.exp2((s - m_new) * c)
            acc = alpha * acc + jnp.dot(p.astype(BF16), v_ref[rows, blk(h)], preferred_element_type=F32)
            new.append((m_new, acc))
        return tuple(new)

    def past(j, carry):
        return update(carry, pl.ds(pl.multiple_of(j * tq, tq), tq), False)

    carry = tuple((jnp.full((tq, 1), -jnp.inf, F32), jnp.zeros((tq, MLA_BLOCK), F32)) for _ in range(MLA_HEADS))
    carry = lax.fori_loop(0, i, past, carry)
    carry = update(carry, pl.ds(pl.multiple_of(i * tq, tq), tq), True)
    o_ref[...] = jnp.concatenate([acc[:, :MLA_V] / acc[:, MLA_V:MLA_V + 1] for _, acc in carry], axis=1)


def mla_prompt_attention(q, k, v, *, tq=512):
    b, l, _ = q.shape
    tq = min(tq, l)
    assert l % tq == 0
    return pl.pallas_call(
        functools.partial(_flash_body, tq=tq),
        out_shape=jax.ShapeDtypeStruct((b, l, MLA_HEADS * MLA_V), F32),
        grid=(b, l // tq),
        in_specs=[pl.BlockSpec((None, tq, MLA_QK), lambda bi, i: (bi, i, 0)),
                  pl.BlockSpec((None, l, MLA_QK), lambda bi, i: (bi, 0, 0)),
                  pl.BlockSpec((None, l, MLA_QK), lambda bi, i: (bi, 0, 0))],
        out_specs=pl.BlockSpec((None, tq, MLA_HEADS * MLA_V), lambda bi, i: (bi, i, 0)),
        compiler_params=_cparams("arbitrary", "arbitrary"),
        name="mla_prompt_attention",
    )(q, k, v)


def prepare_params(raw):
    depth = raw['w_in'].shape[0]
    row = lambda a: a.reshape(depth, 1, -1)
    pad_heads = lambda a: jnp.pad(a, ((0, 0), (0, LANES - SSD_HEADS))).reshape(depth, 1, LANES)
    eye = jnp.eye(LRU_BLOCKS, dtype=F32)
    block_diag = lambda w: jnp.einsum('lhij,hg->lhigj', w, eye).reshape(depth, LRU_W, LRU_W)
    ones_h = jnp.ones((1, MLA_HEADS, 1), F32)
    per_head = lambda g: ones_h * g[:, None, :]
    tile4 = lambda g: jnp.tile(g, (1, MLA_HEADS)).reshape(depth, 1, -1)
    wq = raw['mla_w_uq'].reshape(depth, Q_LORA, MLA_HEADS, MLA_NOPE + MLA_ROPE)
    wkv = raw['mla_w_ukv'].reshape(depth, KV_LORA, MLA_HEADS, MLA_NOPE + MLA_V)
    w_k = wkv[..., :MLA_NOPE].reshape(depth, KV_LORA, MLA_HEADS * MLA_NOPE)
    w_v = wkv[..., MLA_NOPE:]
    w_v_blocked = jnp.pad(w_v, ((0, 0), (0, 0), (0, 0), (0, MLA_BLOCK - MLA_V))).reshape(depth, KV_LORA, MLA_QK)
    return dict(
        norm_ffn1=row(raw['norm_ffn1']), norm_mix=row(raw['norm_mix']), norm_ffn2=row(raw['norm_ffn2']),
        out_norm=row(raw['out_norm']),
        ffn1_in=raw['w_ffn1_in'].astype(BF16), ffn1_out=raw['w_ffn1_out'].astype(BF16),
        ffn2_in=raw['w_ffn2_in'].astype(BF16), ffn2_out=raw['w_ffn2_out'].astype(BF16),
        w_in=raw['w_in'], w_out=raw['w_out'].astype(BF16),
        ssd_conv_w=raw['ssd_conv_w'], ssd_conv_b=row(raw['ssd_conv_b']), ssd_dt_bias=pad_heads(raw['ssd_dt_bias']),
        ssd_a_log=pad_heads(raw['ssd_a_log']), ssd_d=row(jnp.repeat(raw['ssd_d'], SSD_P, axis=1)),
        lru_conv_w=raw['lru_conv_w'], lru_conv_b=row(raw['lru_conv_b']),
        lru_w_ri=jnp.concatenate([block_diag(raw['lru_w_r']), block_diag(raw['lru_w_i'])], axis=2).astype(BF16),
        lru_b_ri=row(jnp.concatenate([raw['lru_b_r'], raw['lru_b_i']], axis=1)), lru_lambda=row(raw['lru_lambda']),
        hgrn_lb=raw['hgrn_lb'],
        mla_q_norm=row(raw['mla_q_norm']), mla_kv_norm=row(raw['mla_kv_norm']),
        mla_wq_blocked=_head_blocked(wq[..., MLA_NOPE:], wq[..., :MLA_NOPE]).astype(BF16),
        mla_gq_blocked=row(_head_blocked(per_head(raw['mla_qn_rope']), per_head(raw['mla_qn_nope']))),
        mla_gkr_pad=row(jnp.pad(raw['mla_kn_rope'], ((0, 0), (0, MLA_BLOCK - MLA_ROPE)))),
        mla_wkv_blocked=jnp.concatenate([_head_blocked(None, wkv[..., :MLA_NOPE]), w_v_blocked], axis=2).astype(BF16),
        mla_gk_blocked=row(_head_blocked(None, per_head(raw['mla_kn_nope']))),
        mla_wq_compact=jnp.concatenate([wq[..., :MLA_NOPE].reshape(depth, Q_LORA, -1),
                                        wq[..., MLA_NOPE:].reshape(depth, Q_LORA, -1)], axis=2).astype(BF16),
        mla_gn_compact=tile4(raw['mla_qn_nope'] * raw['mla_kn_nope']), mla_gr_compact=tile4(raw['mla_qn_rope']),
        mla_gkr_tiled=tile4(raw['mla_kn_rope']),
        mla_wk_t=jnp.swapaxes(w_k, 1, 2).astype(BF16),
        mla_wv=w_v.reshape(depth, KV_LORA, MLA_HEADS * MLA_V).astype(BF16),
    )


def _layer(x, mods, layer, row0, p, init_layer, init, outs, tables, paged, seq_block):
    conv_a, ssm_a, conv_b, h_b, s_c = init
    lat_o, kr_o, ssm_o, conva_o, h_o, convb_o, s_o = outs
    x = ffn_halfstep(x, mods, layer, row0, 0, p['norm_ffn1'], p['ffn1_in'], p['ffn1_out'])
    kr_copies = 1 if paged is None else LANES // MLA_ROPE
    u_a, u_b, u_c, u_d = mixer_projection(x, mods, layer, row0, p['norm_mix'], p['w_in'], kr_copies)
    y_a, conva_o, ssm_o = ssd_mixer(u_a, conv_a, ssm_a, init_layer, p, layer, conva_o, ssm_o, gb=seq_block)
    y_b, convb_o, h_o = rglru_mixer(u_b, conv_b, h_b, init_layer, p, layer, convb_o, h_o, gb=seq_block)
    y_c, s_o = hgrn2_mixer(u_c, s_c, init_layer, p['hgrn_lb'], layer, s_o, gb=seq_block)
    if paged is None:
        q, k, v, lat_o, kr_o = mla_project_prompt(u_d, tables, p, layer, lat_o, kr_o)
        y_d = mla_prompt_attention(q, k, v)
    else:
        cache_lat, cache_kr_t, page_table = paged
        qn, qr, kr4, lat_o, kr_o = mla_project_sample(u_d, tables, p, layer, lat_o, kr_o)
        y_d = mla_sample_attention(qn, qr, lat_o, kr4, cache_lat, cache_kr_t, page_table, layer, p)
    x = output_merge(x, (y_a, y_b, y_c, y_d), mods, layer, row0, p['out_norm'], p['w_out'])
    x = ffn_halfstep(x, mods, layer, row0, 6, p['norm_ffn2'], p['ffn2_in'], p['ffn2_out'])
    return x, (lat_o, kr_o, ssm_o, conva_o, h_o, convb_o, s_o)


def _result_stacks(b, l):
    z = lambda *s: jnp.zeros((DEPTH, b) + s, F32)
    return (z(l, KV_LORA), z(l, MLA_ROPE), z(SSD_HEADS * SSD_P, SSD_N), z(CONV_W - 1, SSD_CONV_DIM),
            z(1, LRU_W), z(CONV_W - 1, LRU_W), z(HG_HEADS, HG_DK, HG_DV))


def kernel(x_prompt, x_sample, cache_mla_latent, cache_mla_krope, state_ssd, state_ssd_conv, state_lru,
           state_lru_conv, state_hgrn, page_table, c_prompt, c_sample, w_ada, b_ada, norm_ffn1, w_ffn1_in,
           w_ffn1_out, norm_mix, w_in, ssd_conv_w, ssd_conv_b, ssd_dt_bias, ssd_a_log, ssd_d, lru_conv_w,
           lru_conv_b, lru_w_r, lru_b_r, lru_w_i, lru_b_i, lru_lambda, hgrn_lb, mla_q_norm, mla_w_uq,
           mla_kv_norm, mla_w_ukv, mla_qn_nope, mla_qn_rope, mla_kn_nope, mla_kn_rope, out_norm, w_out,
           norm_ffn2, w_ffn2_in, w_ffn2_out):
    p = prepare_params(dict(
        norm_ffn1=norm_ffn1, w_ffn1_in=w_ffn1_in, w_ffn1_out=w_ffn1_out, norm_mix=norm_mix, w_in=w_in,
        ssd_conv_w=ssd_conv_w, ssd_conv_b=ssd_conv_b, ssd_dt_bias=ssd_dt_bias, ssd_a_log=ssd_a_log, ssd_d=ssd_d,
        lru_conv_w=lru_conv_w, lru_conv_b=lru_conv_b, lru_w_r=lru_w_r, lru_b_r=lru_b_r, lru_w_i=lru_w_i,
        lru_b_i=lru_b_i, lru_lambda=lru_lambda, hgrn_lb=hgrn_lb, mla_q_norm=mla_q_norm, mla_w_uq=mla_w_uq,
        mla_kv_norm=mla_kv_norm, mla_w_ukv=mla_w_ukv, mla_qn_nope=mla_qn_nope, mla_qn_rope=mla_qn_rope,
        mla_kn_nope=mla_kn_nope, mla_kn_rope=mla_kn_rope, out_norm=out_norm, w_out=w_out, norm_ffn2=norm_ffn2,
        w_ffn2_in=w_ffn2_in, w_ffn2_out=w_ffn2_out))
    bp, lp = x_prompt.shape[:2]
    bs, ls = x_sample.shape[:2]
    dt = x_prompt.dtype
    mods = ada_modulation(jnp.concatenate([c_sample, c_prompt], axis=0), w_ada, b_ada)
    mods = mods.reshape(DEPTH, bs + bp, 1, N_MOD * D_MODEL)
    cache_kr_t = jnp.swapaxes(cache_mla_krope, 2, 3)
    nrow = SSD_HEADS * SSD_P
    zero = lambda *s: jnp.zeros((1, bp) + s, F32)
    init_p = (zero(CONV_W - 1, SSD_CONV_DIM), zero(nrow, SSD_N), zero(CONV_W - 1, LRU_W), zero(1, LRU_W),
              zero(HG_HEADS, HG_DK, HG_DV))
    init_s = (state_ssd_conv, state_ssd.reshape(DEPTH, bs, nrow, SSD_N), state_lru_conv,
              state_lru.reshape(DEPTH, bs, 1, LRU_W), state_hgrn)
    tables_p = _rope_tables(0, lp, 1)
    tables_s = _rope_tables(PAST_LEN, ls, LANES // MLA_ROPE)
    out_p, out_s = _result_stacks(bp, lp), _result_stacks(bs, ls)
    yp, ys = x_prompt, x_sample
    sample_block = math.gcd(bs, SUBLANES)
    for l in range(DEPTH):
        yp, out_p = _layer(yp, mods, l, bs, p, 0, init_p, out_p, tables_p, None, 1)
        ys, out_s = _layer(ys, mods, l, 0, p, l, init_s, out_s, tables_s,
                           (cache_mla_latent, cache_kr_t, page_table), sample_block)

    def finish(outs, b):
        lat, kr, ssm, conva, h, convb, s = outs
        return (lat, kr, ssm.reshape(DEPTH, b, SSD_HEADS, SSD_P, SSD_N), conva, h.reshape(DEPTH, b, LRU_W), convb, s)

    lat_p, kr_p, ssd_p, ssdc_p, lru_p, lruc_p, hg_p = [a.astype(dt) for a in finish(out_p, bp)]
    lat_s, kr_s, ssd_s, ssdc_s, lru_s, lruc_s, hg_s = [a.astype(dt) for a in finish(out_s, bs)]
    return (yp, ys, lat_p, lat_s, kr_p, kr_s, ssd_p, ssd_s, ssdc_p, ssdc_s, lru_p, lru_s, lruc_p, lruc_s, hg_p, hg_s)
```

```python
import functools
import math

import jax
import jax.numpy as jnp
from jax import lax
from jax.experimental import pallas as pl
from jax.experimental.pallas import tpu as pltpu

F32 = jnp.float32
BF16 = jnp.bfloat16

D_MODEL = 1024
DEPTH = 4
PAST_LEN = 8192
PAGE_SIZE = 128
W_GROUP = 256
HEAD_DIM = 64
SSD_HEADS = 4
SSD_P = 64
SSD_N = 128
SSD_GROUPS = 2
SSD_CHUNK = 128
CONV_W = 4
SSD_CONV_DIM = W_GROUP + 2 * SSD_GROUPS * SSD_N
LRU_W = W_GROUP
LRU_BLOCKS = 4
LRU_BW = LRU_W // LRU_BLOCKS
LRU_C = 8.0
HG_HEADS = 4
HG_DK = 64
HG_DV = 64
HG_CHUNK = 64
MLA_HEADS = 4
MLA_NOPE = 64
MLA_ROPE = 32
MLA_V = 64
Q_LORA = 256
KV_LORA = 128
ROPE_THETA = 10000.0
MLA_SCALE = 1.0 / math.sqrt(MLA_NOPE + MLA_ROPE)
D_FF = 2816
N_MOD = 9
EPS = 1e-6
IN_SSD = W_GROUP + SSD_CONV_DIM + SSD_HEADS
IN_LRU = 2 * LRU_W
IN_HG = 4 * HG_HEADS * HG_DK
IN_MLA = Q_LORA + KV_LORA + MLA_ROPE
OFF_LRU = IN_SSD
OFF_HG = OFF_LRU + IN_LRU
OFF_MLA = OFF_HG + IN_HG
N_IN = OFF_MLA + IN_MLA

LANES = 128
SUBLANES = 8
VMEM_BYTES_V7X = 64 * 1024 * 1024
VMEM_LIMIT = VMEM_BYTES_V7X * 3 // 4

PROJ_SSD = W_GROUP + SSD_CONV_DIM + LANES
PROJ_LRU = IN_LRU
PROJ_HG = IN_HG
PROJ_MLA = Q_LORA + KV_LORA + LANES
PROJ_ALL = PROJ_SSD + PROJ_LRU + PROJ_HG + PROJ_MLA


def _cparams(*sem):
    return pltpu.CompilerParams(dimension_semantics=sem, vmem_limit_bytes=VMEM_LIMIT)


def _silu(x):
    return x * jax.nn.sigmoid(x)


def _rms(x, w):
    return x * lax.rsqrt(jnp.mean(x * x, axis=-1, keepdims=True) + EPS) * w


def _lspec(arr, layer):
    nd = arr.ndim - 1
    return pl.BlockSpec((None,) + arr.shape[1:], lambda *_: (layer,) + (0,) * nd)


def _any_spec():
    return pl.BlockSpec(memory_space=pl.ANY)


def _ada_body(c_ref, w_ref, b_ref, o_ref):
    a = _silu(c_ref[...]).astype(BF16)
    o_ref[...] = jnp.dot(a, w_ref[...].astype(BF16), preferred_element_type=F32) + b_ref[...]


def ada_modulation(c_all, w_ada, b_ada, *, tn=1152):
    r, d = c_all.shape
    depth, _, n = w_ada.shape
    return pl.pallas_call(
        _ada_body,
        out_shape=jax.ShapeDtypeStruct((depth, r, n), F32),
        grid=(depth, n // tn),
        in_specs=[
            pl.BlockSpec((r, d), lambda l, j: (0, 0)),
            pl.BlockSpec((None, d, tn), lambda l, j: (l, 0, j)),
            pl.BlockSpec((None, 1, tn), lambda l, j: (l, 0, j)),
        ],
        out_specs=pl.BlockSpec((None, r, tn), lambda l, j: (l, 0, j)),
        compiler_params=_cparams("arbitrary", "arbitrary"),
        name="ada_modulation",
    )(c_all, w_ada, b_ada.reshape(depth, 1, n))


class _Tiling:
    def __init__(self, b, l, rows):
        if l >= rows:
            assert l % rows == 0
            self.gb, self.r = 1, rows
        else:
            gb = min(b, max(1, rows // l))
            assert b % gb == 0
            self.gb, self.r = gb, l
        self.b, self.l = b, l
        self.nb = b // self.gb
        self.nl = l // self.r
        self.steps = self.nb * self.nl
        self.m = self.gb * self.r

    def tok_index(self, i):
        return (i // self.nl, i % self.nl)


def _mod_spec(t, layer, row0, k):
    assert row0 % t.gb == 0
    return pl.BlockSpec((None, t.gb, 1, D_MODEL), lambda i, *_: (layer, row0 // t.gb + i // t.nl, 0, k))


def _ffn_body(x_ref, sh_ref, sc_ref, gt_ref, nw_ref, w_in_ref, w_out_ref, o_ref):
    gb, r, d = x_ref.shape
    x = x_ref[...]
    h = _rms(x, nw_ref[...]) * (1.0 + sc_ref[...]) + sh_ref[...]
    gu = jnp.dot(h.reshape(gb * r, d).astype(BF16), w_in_ref[...], preferred_element_type=F32)
    a = _silu(gu[:, :D_FF]) * gu[:, D_FF:]
    y = jnp.dot(a.astype(BF16), w_out_ref[...], preferred_element_type=F32)
    o_ref[...] = x + 0.5 * gt_ref[...] * y.reshape(gb, r, d)


def ffn_halfstep(x, mods, layer, row0, k0, norm_w, w_in, w_out, *, rows=512):
    b, l, d = x.shape
    t = _Tiling(b, l, rows)
    xspec = pl.BlockSpec((t.gb, t.r, d), lambda i: (*t.tok_index(i), 0))
    return pl.pallas_call(
        _ffn_body,
        out_shape=jax.ShapeDtypeStruct(x.shape, F32),
        grid=(t.steps,),
        in_specs=[xspec, _mod_spec(t, layer, row0, k0), _mod_spec(t, layer, row0, k0 + 1),
                  _mod_spec(t, layer, row0, k0 + 2), _lspec(norm_w, layer), _lspec(w_in, layer), _lspec(w_out, layer)],
        out_specs=xspec,
        compiler_params=_cparams("arbitrary"),
        name="ffn_halfstep",
    )(x, mods, mods, mods, norm_w, w_in, w_out)


def _pack_w_in(w_ref, wp_ref, kr_copies):
    d = w_ref.shape[0]
    rows = 256
    off_kr = OFF_MLA + Q_LORA + KV_LORA
    dst_mla = PROJ_SSD + PROJ_LRU + PROJ_HG

    def chunk(c, _):
        r = pl.ds(pl.multiple_of(c * rows, rows), rows)
        put = lambda dst, src, n: wp_ref.__setitem__((r, pl.ds(dst, n)), w_ref[r, pl.ds(src, n)].astype(BF16))
        zero = lambda dst, n: wp_ref.__setitem__((r, pl.ds(dst, n)), jnp.zeros((rows, n), BF16))
        put(0, 0, W_GROUP + SSD_CONV_DIM)
        zero(W_GROUP + SSD_CONV_DIM, LANES)
        put(W_GROUP + SSD_CONV_DIM, W_GROUP + SSD_CONV_DIM, SSD_HEADS)
        put(PROJ_SSD, OFF_LRU, IN_LRU)
        put(PROJ_SSD + PROJ_LRU, OFF_HG, IN_HG)
        put(dst_mla, OFF_MLA, Q_LORA + KV_LORA)
        zero(dst_mla + Q_LORA + KV_LORA, LANES)
        for k in range(kr_copies):
            put(dst_mla + Q_LORA + KV_LORA + k * MLA_ROPE, off_kr, MLA_ROPE)

    lax.fori_loop(0, d // rows, chunk, None)


def _proj_body(x_ref, sh_ref, sc_ref, nw_ref, w_ref, ssd_ref, lru_ref, hg_ref, mla_ref, wp_ref, *, kr_copies):
    @pl.when(pl.program_id(0) == 0)
    def _():
        _pack_w_in(w_ref, wp_ref, kr_copies)

    gb, r, d = x_ref.shape
    h = _rms(x_ref[...], nw_ref[...]) * (1.0 + sc_ref[...]) + sh_ref[...]
    h = h.reshape(gb * r, d).astype(BF16)
    off = 0
    for o_ref in (ssd_ref, lru_ref, hg_ref, mla_ref):
        n = o_ref.shape[-1]
        u = jnp.dot(h, wp_ref[:, off:off + n], preferred_element_type=F32)
        o_ref[...] = u.reshape(gb, r, n)
        off += n


def mixer_projection(x, mods, layer, row0, norm_w, w_in, kr_copies, *, rows=512):
    b, l, d = x.shape
    t = _Tiling(b, l, rows)
    widths = (PROJ_SSD, PROJ_LRU, PROJ_HG, PROJ_MLA)
    xspec = pl.BlockSpec((t.gb, t.r, d), lambda i: (*t.tok_index(i), 0))
    w_spec = pl.BlockSpec((None, d, N_IN), lambda i: (layer, 0, 0), pipeline_mode=pl.Buffered(1))
    return pl.pallas_call(
        functools.partial(_proj_body, kr_copies=kr_copies),
        out_shape=[jax.ShapeDtypeStruct((b, l, n), F32) for n in widths],
        grid=(t.steps,),
        in_specs=[xspec, _mod_spec(t, layer, row0, 3), _mod_spec(t, layer, row0, 4), _lspec(norm_w, layer), w_spec],
        out_specs=[pl.BlockSpec((t.gb, t.r, n), lambda i: (*t.tok_index(i), 0)) for n in widths],
        scratch_shapes=[pltpu.VMEM((d, PROJ_ALL), BF16)],
        compiler_params=_cparams("arbitrary"),
        name="mixer_projection",
    )(x, mods, mods, norm_w, w_in)


def _head_ones(n):
    i = lax.broadcasted_iota(jnp.int32, (n, n), 0) // HEAD_DIM
    j = lax.broadcasted_iota(jnp.int32, (n, n), 1) // HEAD_DIM
    return (i == j).astype(BF16)


def _out_body(x_ref, ya_ref, yb_ref, yc_ref, yd_ref, gt_ref, nw_ref, w_ref, o_ref):
    gb, r, d = x_ref.shape
    ones = _head_ones(W_GROUP)
    acc = jnp.zeros((gb * r, d), F32)
    for k, y_ref in enumerate((ya_ref, yb_ref, yc_ref, yd_ref)):
        y = y_ref[...].reshape(gb * r, W_GROUP)
        ss = jnp.dot((y * y).astype(BF16), ones, preferred_element_type=F32)
        yn = y * lax.rsqrt(ss * (1.0 / HEAD_DIM) + EPS) * nw_ref[:, k * W_GROUP:(k + 1) * W_GROUP]
        acc += jnp.dot(yn.astype(BF16), w_ref[k * W_GROUP:(k + 1) * W_GROUP, :], preferred_element_type=F32)
    o_ref[...] = x_ref[...] + gt_ref[...] * acc.reshape(gb, r, d)


def output_merge(x, ys, mods, layer, row0, out_norm, w_out, *, rows=512):
    b, l, d = x.shape
    t = _Tiling(b, l, rows)
    xspec = pl.BlockSpec((t.gb, t.r, d), lambda i: (*t.tok_index(i), 0))
    yspec = pl.BlockSpec((t.gb, t.r, W_GROUP), lambda i: (*t.tok_index(i), 0))
    return pl.pallas_call(
        _out_body,
        out_shape=jax.ShapeDtypeStruct(x.shape, F32),
        grid=(t.steps,),
        in_specs=[xspec, yspec, yspec, yspec, yspec, _mod_spec(t, layer, row0, 5),
                  _lspec(out_norm, layer), _lspec(w_out, layer)],
        out_specs=xspec,
        compiler_params=_cparams("arbitrary"),
        name="output_merge",
    )(x, *ys, mods, out_norm, w_out)


def _cumsum_rows(x):
    n = x.shape[0]
    row = lax.broadcasted_iota(jnp.int32, x.shape, 0)
    d = 1
    while d < n:
        x = x + jnp.where(row >= d, pltpu.roll(x, d, 0), 0.0)
        d *= 2
    return x


def _dot_nt(a, b):
    return lax.dot_general(a, b, (((1,), (1,)), ((), ())), preferred_element_type=F32)


def _dot_tn(a, b):
    return lax.dot_general(a, b, (((0,), (0,)), ((), ())), preferred_element_type=F32)


def _conv_init(ext_ref, cs_ref, q):
    ext_ref[:, pl.ds(q, SUBLANES), :] = jnp.zeros((ext_ref.shape[0], SUBLANES, ext_ref.shape[2]), F32)
    ext_ref[:, pl.ds(q + SUBLANES - (CONV_W - 1), CONV_W - 1), :] = cs_ref[...]


def _causal_conv(ext_ref, s, x, cw_ref, cb_ref, q):
    ext_ref[s, pl.ds(0, SUBLANES), :] = ext_ref[s, pl.ds(q, SUBLANES), :]
    ext_ref[s, pl.ds(SUBLANES, q), :] = x
    out = cb_ref[...]
    for k in range(CONV_W):
        out = out + cw_ref[k:k + 1, :] * ext_ref[s, pl.ds(SUBLANES - (CONV_W - 1) + k, q), :]
    return out


def _state_specs(gb, init_layer, layer, *shapes):
    ins = [pl.BlockSpec((None, gb) + s, lambda i, c, n=len(s): (init_layer, i) + (0,) * n) for s in shapes]
    outs = [pl.BlockSpec((None, gb) + s, lambda i, c, n=len(s): (layer, i) + (0,) * n) for s in shapes]
    return ins, outs


def _ssd_body(u_ref, cs_ref, st_ref, cw_ref, cb_ref, dtb_ref, alog_ref, dsk_ref, _c_alias, _s_alias,
              y_ref, cnew_ref, snew_ref, ext_ref, h_ref, *, q, nsteps):
    step = pl.program_id(1)
    gb, tb, _ = u_ref.shape
    hp = SSD_HEADS // SSD_GROUPS * SSD_P

    @pl.when(step == 0)
    def _():
        _conv_init(ext_ref, cs_ref, q)
        h_ref[...] = st_ref[...]

    row = lax.broadcasted_iota(jnp.int32, (q, q), 0)
    col = lax.broadcasted_iota(jnp.int32, (q, q), 1)
    causal = row >= col
    a = -jnp.exp(alog_ref[...])

    def chunk(s, i):
        tok = pl.ds(pl.multiple_of(i * q, q), q)
        u = u_ref[s, tok, :]
        z = u[:, :W_GROUP]
        conv = _causal_conv(ext_ref, s, u[:, W_GROUP:W_GROUP + SSD_CONV_DIM], cw_ref, cb_ref, q)
        xbc = _silu(conv)
        xs = xbc[:, :W_GROUP]
        bm = xbc[:, W_GROUP:W_GROUP + SSD_GROUPS * SSD_N].astype(BF16)
        cm = xbc[:, W_GROUP + SSD_GROUPS * SSD_N:].astype(BF16)
        dt = jax.nn.softplus(u[:, W_GROUP + SSD_CONV_DIM:] + dtb_ref[...])
        acum = _cumsum_rows(dt * a)
        acum_t = acum.T
        a_end = acum[q - 1:q, :]
        ys = []
        for g in range(SSD_GROUPS):
            bg = bm[:, g * SSD_N:(g + 1) * SSD_N]
            cg = cm[:, g * SSD_N:(g + 1) * SSD_N]
            cb = _dot_nt(cg, bg)
            h_prev = h_ref[s, g * hp:(g + 1) * hp, :]
            y_off = _dot_nt(cg, h_prev.astype(BF16))
            for j in range(SSD_HEADS // SSD_GROUPS):
                h = g * (SSD_HEADS // SSD_GROUPS) + j
                dt_h = dt[:, h:h + 1]
                ac_h = acum[:, h:h + 1]
                ae_h = a_end[:, h:h + 1]
                decay = jnp.exp(jnp.where(causal, ac_h - acum_t[h:h + 1, :], -jnp.inf))
                xs_h = xs[:, h * SSD_P:(h + 1) * SSD_P]
                xdt_h = xs_h * dt_h
                y_h = jnp.dot((cb * decay).astype(BF16), xdt_h.astype(BF16), preferred_element_type=F32)
                y_h = y_h + y_off[:, j * SSD_P:(j + 1) * SSD_P] * jnp.exp(ac_h)
                y_h = y_h + dsk_ref[:, h * SSD_P:(h + 1) * SSD_P] * xs_h
                ys.append(y_h)
                w_h = (xdt_h * jnp.exp(ae_h - ac_h)).astype(BF16)
                st_h = _dot_tn(w_h, bg)
                rows = pl.ds(h * SSD_P, SSD_P)
                h_ref[s, rows, :] = jnp.exp(ae_h) * h_ref[s, rows, :] + st_h
        y_ref[s, tok, :] = jnp.concatenate(ys, axis=1) * _silu(z)

    for s in range(gb):
        if tb == q:
            chunk(s, 0)
        else:
            lax.fori_loop(0, tb // q, lambda i, _: chunk(s, i), None, unroll=2)

    @pl.when(step == nsteps - 1)
    def _():
        cnew_ref[...] = ext_ref[:, pl.ds(q + SUBLANES - (CONV_W - 1), CONV_W - 1), :]
        snew_ref[...] = h_ref[...]


def ssd_mixer(u, conv_state, ssm_state, init_layer, p, layer, conv_out, ssm_out, *, gb=1, rows=512):
    b, l, _ = u.shape
    q = min(SSD_CHUNK, l)
    tb = min(rows, l)
    assert l % tb == 0 and tb % q == 0 and b % gb == 0 and q % SUBLANES == 0
    nsteps = l // tb
    nrow = SSD_HEADS * SSD_P
    st_in, st_out = _state_specs(gb, init_layer, layer, (CONV_W - 1, SSD_CONV_DIM), (nrow, SSD_N))
    params = [p['ssd_conv_w'], p['ssd_conv_b'], p['ssd_dt_bias'], p['ssd_a_log'], p['ssd_d']]
    return pl.pallas_call(
        functools.partial(_ssd_body, q=q, nsteps=nsteps),
        out_shape=[jax.ShapeDtypeStruct((b, l, W_GROUP), F32),
                   jax.ShapeDtypeStruct(conv_out.shape, F32), jax.ShapeDtypeStruct(ssm_out.shape, F32)],
        grid=(b // gb, nsteps),
        in_specs=[pl.BlockSpec((gb, tb, PROJ_SSD), lambda i, c: (i, c, 0)), *st_in,
                  *[_lspec(a, layer) for a in params], _any_spec(), _any_spec()],
        out_specs=[pl.BlockSpec((gb, tb, W_GROUP), lambda i, c: (i, c, 0)), *st_out],
        scratch_shapes=[pltpu.VMEM((gb, q + SUBLANES, SSD_CONV_DIM), F32), pltpu.VMEM((gb, nrow, SSD_N), F32)],
        input_output_aliases={8: 1, 9: 2},
        compiler_params=_cparams("arbitrary", "arbitrary"),
        name="ssd_mixer",
    )(u, conv_state, ssm_state, *params, conv_out, ssm_out)


def _linear_scan_rows(a, b):
    n = a.shape[0]
    row = lax.broadcasted_iota(jnp.int32, a.shape, 0)
    d = 1
    while d < n:
        m = row >= d
        b = jnp.where(m, a * pltpu.roll(b, d, 0) + b, b)
        a = jnp.where(m, a * pltpu.roll(a, d, 0), a)
        d *= 2
    return a, b


def _lru_body(u_ref, cs_ref, h0_ref, cw_ref, cb_ref, wri_ref, bri_ref, lam_ref, _c_alias, _h_alias,
              y_ref, cnew_ref, hnew_ref, ext_ref, h_ref, *, q, nsteps):
    step = pl.program_id(1)
    gb, tb, _ = u_ref.shape

    @pl.when(step == 0)
    def _():
        _conv_init(ext_ref, cs_ref, q)
        h_ref[...] = h0_ref[...]

    sp = jax.nn.softplus(-lam_ref[...])

    def chunk(s, i):
        tok = pl.ds(pl.multiple_of(i * q, q), q)
        u = u_ref[s, tok, :]
        xb = _causal_conv(ext_ref, s, u[:, :LRU_W], cw_ref, cb_ref, q)
        ri = jnp.dot(xb.astype(BF16), wri_ref[...], preferred_element_type=F32) + bri_ref[...]
        r = jax.nn.sigmoid(ri[:, :LRU_W])
        i = jax.nn.sigmoid(ri[:, LRU_W:])
        log_a = -LRU_C * r * sp
        a = jnp.exp(log_a)
        bterm = jnp.sqrt(-jnp.tanh(log_a) * (a * a + 1.0)) * (i * xb)
        a_cum, h = _linear_scan_rows(a, bterm)
        h = h + a_cum * h_ref[s]
        h_ref[s] = h[q - 1:q, :]
        y_ref[s, tok, :] = h * jax.nn.gelu(u[:, LRU_W:])

    for s in range(gb):
        if tb == q:
            chunk(s, 0)
        else:
            lax.fori_loop(0, tb // q, lambda i, _: chunk(s, i), None)

    @pl.when(step == nsteps - 1)
    def _():
        cnew_ref[...] = ext_ref[:, pl.ds(q + SUBLANES - (CONV_W - 1), CONV_W - 1), :]
        hnew_ref[...] = h_ref[...]


def rglru_mixer(u, conv_state, h0, init_layer, p, layer, conv_out, h_out, *, gb=1, chunk=256, rows=1024):
    b, l, _ = u.shape
    q = min(chunk, l)
    tb = min(rows, l)
    assert l % tb == 0 and tb % q == 0 and b % gb == 0 and q % SUBLANES == 0
    nsteps = l // tb
    st_in, st_out = _state_specs(gb, init_layer, layer, (CONV_W - 1, LRU_W), (1, LRU_W))
    params = [p['lru_conv_w'], p['lru_conv_b'], p['lru_w_ri'], p['lru_b_ri'], p['lru_lambda']]
    return pl.pallas_call(
        functools.partial(_lru_body, q=q, nsteps=nsteps),
        out_shape=[jax.ShapeDtypeStruct((b, l, LRU_W), F32),
                   jax.ShapeDtypeStruct(conv_out.shape, F32), jax.ShapeDtypeStruct(h_out.shape, F32)],
        grid=(b // gb, nsteps),
        in_specs=[pl.BlockSpec((gb, tb, PROJ_LRU), lambda i, c: (i, c, 0)), *st_in,
                  *[_lspec(a, layer) for a in params], _any_spec(), _any_spec()],
        out_specs=[pl.BlockSpec((gb, tb, LRU_W), lambda i, c: (i, c, 0)), *st_out],
        scratch_shapes=[pltpu.VMEM((gb, q + SUBLANES, LRU_W), F32), pltpu.VMEM((gb, 1, LRU_W), F32)],
        input_output_aliases={8: 1, 9: 2},
        compiler_params=_cparams("arbitrary", "arbitrary"),
        name="rglru_mixer",
    )(u, conv_state, h0, *params, conv_out, h_out)


def _hgrn_levels(c):
    ws, w = [], c // 2
    while w >= 1:
        ws.append(w)
        w //= 2
    return ws


def _boundary_rows(g, w):
    c, n = g.shape
    if 2 * w >= SUBLANES:
        gr = g.reshape(c // (2 * w), 2 * w, n)[:, w - 1:w, :]
        return jnp.broadcast_to(gr, (c // (2 * w), 2 * w, n)).reshape(c, n)
    r = lax.broadcasted_iota(jnp.int32, g.shape, 0) % (2 * w)
    out = g
    for delta in range(-(w - 1), w + 1):
        if delta != 0:
            out = jnp.where(r - (w - 1) == delta, pltpu.roll(g, delta % c, 0), out)
    return out


def _hgrn_body(u_ref, st0_ref, lbp_ref, _alias, y_ref, stn_ref, st_ref, *, c, nsteps, layer):
    step = pl.program_id(1)
    gb, tb, _ = u_ref.shape
    nk = HG_HEADS * HG_DK
    hv = lax.broadcasted_iota(jnp.int32, (nk, nk), 0) // HG_DV
    hk = lax.broadcasted_iota(jnp.int32, (nk, nk), 1) // HG_DK
    state_mask = hv == hk

    @pl.when(step == 0)
    def _():
        for s in range(gb):
            t = st0_ref[s].reshape(nk, HG_DV).T
            st_ref[s] = jnp.where(state_mask, jnp.concatenate([t] * HG_HEADS, axis=0), 0.0)

    p = lbp_ref[...]
    e = jnp.exp(p - jnp.max(p, axis=0, keepdims=True))
    lb = jnp.zeros((1, nk), F32)
    for i in range(1, layer + 1):
        lb = lb + e[i:i + 1, :]
    lb = lb / jnp.sum(e, axis=0, keepdims=True)
    log_lb = jnp.log(lb)
    log_1mlb = jnp.log1p(-lb)

    t_idx = lax.broadcasted_iota(jnp.int32, (c, nk), 0)
    row_s = lax.broadcasted_iota(jnp.int32, (HG_HEADS * c, nk), 0)
    lane = lax.broadcasted_iota(jnp.int32, (HG_HEADS * c, nk), 1)
    head_rows = (row_s // c) == (lane // HG_DK)
    t_sc = lax.broadcasted_iota(jnp.int32, (c, HG_HEADS * c), 0)
    s_sc = lax.broadcasted_iota(jnp.int32, (c, HG_HEADS * c), 1) % c

    def tile_heads(x):
        xt = jnp.concatenate([x.astype(BF16)] * HG_HEADS, axis=0)
        return jnp.where(head_rows, xt, jnp.zeros_like(xt))

    def chunk(s, i):
        rows = pl.ds(pl.multiple_of(i * c, c), c)
        u = u_ref[s, rows, :]
        q = _silu(u[:, :nk])
        fz = u[:, nk:2 * nk]
        v = u[:, 2 * nk:3 * nk]
        gate = u[:, 3 * nk:]
        lf = jnp.logaddexp(log_lb, log_1mlb + jax.nn.log_sigmoid(fz))
        k = (1.0 - lb) * jax.nn.sigmoid(-fz)
        g = _cumsum_rows(lf)
        g_end = g[c - 1:c, :]
        st = st_ref[s]
        o = _dot_nt((q * jnp.exp(g)).astype(BF16), st.astype(BF16))
        sc = jnp.where(t_sc == s_sc, _dot_nt(q.astype(BF16), tile_heads(k)), 0.0)
        for w in _hgrn_levels(c):
            gr = _boundary_rows(g, w)
            upper = (t_idx // w) % 2 == 1
            ef = jnp.exp(jnp.where(upper, g - gr, gr - g))
            qt = jnp.where(upper, q * ef, 0.0)
            kt = jnp.where(upper, 0.0, k * ef)
            blk = (t_sc // (2 * w)) == (s_sc // (2 * w))
            sc = sc + jnp.where(blk, _dot_nt(qt.astype(BF16), tile_heads(kt)), 0.0)
        o = o + jnp.dot(sc.astype(BF16), tile_heads(v), preferred_element_type=F32)
        y_ref[s, rows, :] = o * _silu(gate)
        upd = _dot_tn(v.astype(BF16), (k * jnp.exp(g_end - g)).astype(BF16))
        st_ref[s] = st * jnp.exp(g_end) + jnp.where(state_mask, upd, 0.0)

    for s in range(gb):
        if tb == c:
            chunk(s, 0)
        else:
            lax.fori_loop(0, tb // c, lambda i, _: chunk(s, i), None, unroll=4)

    @pl.when(step == nsteps - 1)
    def _():
        for s in range(gb):
            st = jnp.where(state_mask, st_ref[s], 0.0)
            t = st[:HG_DV]
            for h in range(1, HG_HEADS):
                t = t + st[h * HG_DV:(h + 1) * HG_DV]
            stn_ref[s] = t.T.reshape(HG_HEADS, HG_DK, HG_DV)


def hgrn2_mixer(u, state, init_layer, lb_param, layer, state_out, *, gb=1, rows=512):
    b, l, _ = u.shape
    c = min(HG_CHUNK, l)
    tb = min(rows, l)
    assert l % tb == 0 and tb % c == 0 and b % gb == 0
    nsteps = l // tb
    nk = HG_HEADS * HG_DK
    st_in, st_out = _state_specs(gb, init_layer, layer, (HG_HEADS, HG_DK, HG_DV))
    return pl.pallas_call(
        functools.partial(_hgrn_body, c=c, nsteps=nsteps, layer=layer),
        out_shape=[jax.ShapeDtypeStruct((b, l, nk), F32), jax.ShapeDtypeStruct(state_out.shape, F32)],
        grid=(b // gb, nsteps),
        in_specs=[pl.BlockSpec((gb, tb, PROJ_HG), lambda i, j: (i, j, 0)), *st_in,
                  pl.BlockSpec(lb_param.shape, lambda i, j: (0, 0)), _any_spec()],
        out_specs=[pl.BlockSpec((gb, tb, nk), lambda i, j: (i, j, 0)), *st_out],
        scratch_shapes=[pltpu.VMEM((gb, nk, nk), F32)],
        input_output_aliases={3: 1},
        compiler_params=_cparams("arbitrary", "arbitrary"),
        name="hgrn2_mixer",
    )(u, state, lb_param, state_out)


MLA_BLOCK = LANES
MLA_QK = MLA_HEADS * MLA_BLOCK
MLA_NOPE_OFF = MLA_BLOCK - MLA_NOPE


def _segment_ones():
    seg = lambda i: jnp.where(i < MLA_ROPE, 0, jnp.where(i < MLA_NOPE_OFF, 1, 2))
    i = seg(lax.broadcasted_iota(jnp.int32, (MLA_BLOCK, MLA_BLOCK), 0))
    j = seg(lax.broadcasted_iota(jnp.int32, (MLA_BLOCK, MLA_BLOCK), 1))
    return (i == j).astype(BF16)


def _blocked_rms(x, ones, inv_cnt, gain):
    parts = []
    for h in range(MLA_HEADS):
        xb = x[:, h * MLA_BLOCK:(h + 1) * MLA_BLOCK]
        ss = jnp.dot((xb * xb).astype(BF16), ones, preferred_element_type=F32)
        parts.append(xb * lax.rsqrt(ss * inv_cnt + EPS))
    return jnp.concatenate(parts, axis=1) * gain


def _rope(x, cos, sin_lo, sin_hi):
    n = x.shape[-1]
    half = MLA_ROPE // 2
    return x * cos + pltpu.roll(x, n - half, 1) * sin_lo + pltpu.roll(x, half, 1) * sin_hi


def _paged_body(pt_ref, qn_ref, qr_ref, latn_ref, kr4n_ref, wkt_ref, wv_ref, clat_hbm, ckr_hbm, o_ref,
                lat_buf, kr_buf, latbf_ref, s_ref, sem, *, layer, n_pages, chunk_pages):
    b = pl.program_id(0)
    nb = pl.num_programs(0)
    t_new = qn_ref.shape[0]
    hq = MLA_HEADS * t_new
    nk = MLA_HEADS * MLA_NOPE
    n_tok = chunk_pages * PAGE_SIZE
    issue_unroll = 8

    def page_copies(seq, slot, p):
        page = pt_ref[seq, p]
        return (pltpu.make_async_copy(clat_hbm.at[layer, page], lat_buf.at[slot, p], sem.at[slot, 0]),
                pltpu.make_async_copy(ckr_hbm.at[layer, page], kr_buf.at[slot, p], sem.at[slot, 1]))

    @pl.when(b == 0)
    def _():
        def body(g, _):
            for k in range(issue_unroll):
                for cp in page_copies(0, 0, g * issue_unroll + k):
                    cp.start()
        lax.fori_loop(0, n_pages // issue_unroll, body, None)

    seq_next = lax.rem(b + 1, nb)
    slot_next = lax.rem(b + 1, 2)

    tile_rows = lambda x: jnp.concatenate([x] * MLA_HEADS, axis=0)
    own_n = (lax.broadcasted_iota(jnp.int32, (hq, nk), 0) // t_new
             == lax.broadcasted_iota(jnp.int32, (hq, nk), 1) // MLA_NOPE)
    own_r = (lax.broadcasted_iota(jnp.int32, (hq, LANES), 0) // t_new
             == lax.broadcasted_iota(jnp.int32, (hq, LANES), 1) // MLA_ROPE)
    qn_sel = jnp.where(own_n, tile_rows(qn_ref[...]), 0.0).astype(BF16)
    wkt = wkt_ref[...]
    q_abs = jnp.dot(qn_sel, wkt, preferred_element_type=F32).astype(BF16)
    qr_sel = jnp.where(own_r, tile_rows(qr_ref[...]), 0.0).astype(BF16)
    lhs = jnp.concatenate([wkt, q_abs], axis=0)

    def scores(lat_bf, kr4_bf):
        n = lat_bf.shape[0]
        both = _dot_nt(lhs, lat_bf)
        sq = both[:nk] * both[:nk]
        ssq = []
        for h in range(MLA_HEADS):
            part = sq[h * MLA_NOPE:(h + 1) * MLA_NOPE].reshape(MLA_NOPE // SUBLANES, SUBLANES, n).sum(axis=0)
            shift = SUBLANES // 2
            while shift >= 1:
                part = part + pltpu.roll(part, shift, 0)
                shift //= 2
            ssq.append(part)
        ssq = jnp.concatenate(ssq, axis=0)
        s_rope = jnp.dot(qr_sel, kr4_bf, preferred_element_type=F32)
        return (both[nk:] * lax.rsqrt(ssq * (1.0 / MLA_NOPE) + EPS) + s_rope) * MLA_SCALE

    latn_bf = latn_ref[...].astype(BF16)
    s_new = scores(latn_bf, kr4n_ref[...].T.astype(BF16))
    t_q = lax.broadcasted_iota(jnp.int32, (hq, t_new), 0) % t_new
    t_k = lax.broadcasted_iota(jnp.int32, (hq, t_new), 1)
    s_new = jnp.where(t_k <= t_q, s_new, -jnp.inf)
    m = jnp.max(s_new, axis=-1, keepdims=True)

    slot = b % 2
    for p in range(n_pages):
        for cp in page_copies(b, slot, p):
            cp.wait()

    for c in range(n_pages // chunk_pages):
        p0 = c * chunk_pages
        for i in range(chunk_pages):
            for cp in page_copies(seq_next, slot_next, p0 + i):
                cp.start()
        lat_bf = lat_buf[slot, p0:p0 + chunk_pages].reshape(n_tok, KV_LORA).astype(BF16)
        latbf_ref[c * n_tok:(c + 1) * n_tok, :] = lat_bf
        kr_t = jnp.concatenate([kr_buf[slot, p0 + i] for i in range(chunk_pages)], axis=1).astype(BF16)
        s = scores(lat_bf, tile_rows(kr_t))
        s_ref[:, c * n_tok:(c + 1) * n_tok] = s
        m = jnp.maximum(m, jnp.max(s, axis=-1, keepdims=True))

    p_old = jnp.exp(s_ref[...] - m)
    p_new = jnp.exp(s_new - m)
    l = jnp.sum(p_old, axis=-1, keepdims=True) + jnp.sum(p_new, axis=-1, keepdims=True)
    acc = jnp.dot(p_old.astype(BF16), latbf_ref[...], preferred_element_type=F32)
    acc = acc + jnp.dot(p_new.astype(BF16), latn_bf, preferred_element_type=F32)
    ov = jnp.dot((acc / l).astype(BF16), wv_ref[...], preferred_element_type=F32)
    out = jnp.zeros((t_new, MLA_HEADS * MLA_V), F32)
    lane_v = lax.broadcasted_iota(jnp.int32, (t_new, MLA_HEADS * MLA_V), 1) // MLA_V
    for h in range(MLA_HEADS):
        out = out + jnp.where(lane_v == h, ov[h * t_new:(h + 1) * t_new, :], 0.0)
    o_ref[...] = out

    @pl.when(b == nb - 1)
    def _():
        for p in range(n_pages):
            for cp in page_copies(seq_next, slot_next, p):
                cp.wait()


def mla_sample_attention(qn, qr, lat_all, kr4_new, cache_lat, cache_kr_t, page_table, layer, p, *, chunk_pages=8):
    b, t_new, _ = qn.shape
    n_pages = page_table.shape[1]
    assert n_pages % chunk_pages == 0 and n_pages % 8 == 0
    assert t_new == SUBLANES
    n_past = n_pages * PAGE_SIZE
    seq = lambda n: pl.BlockSpec((None, t_new, n), lambda i, pt: (i, 0, 0))
    lspec = lambda a: pl.BlockSpec((None,) + a.shape[1:], lambda i, pt: (layer, 0, 0))
    return pl.pallas_call(
        functools.partial(_paged_body, layer=layer, n_pages=n_pages, chunk_pages=chunk_pages),
        out_shape=jax.ShapeDtypeStruct((b, t_new, MLA_HEADS * MLA_V), F32),
        grid_spec=pltpu.PrefetchScalarGridSpec(
            num_scalar_prefetch=1,
            grid=(b,),
            in_specs=[seq(MLA_HEADS * MLA_NOPE), seq(LANES),
                      pl.BlockSpec((None, None, t_new, KV_LORA), lambda i, pt: (layer, i, 0, 0)), seq(LANES),
                      lspec(p['mla_wk_t']), lspec(p['mla_wv']), _any_spec(), _any_spec()],
            out_specs=seq(MLA_HEADS * MLA_V),
            scratch_shapes=[pltpu.VMEM((2, n_pages, PAGE_SIZE, KV_LORA), F32),
                            pltpu.VMEM((2, n_pages, MLA_ROPE, PAGE_SIZE), F32),
                            pltpu.VMEM((n_past, KV_LORA), BF16),
                            pltpu.VMEM((MLA_HEADS * t_new, n_past), F32),
                            pltpu.SemaphoreType.DMA((2, 2))],
        ),
        compiler_params=_cparams("arbitrary"),
        name="mla_sample_attention",
    )(page_table, qn, qr, lat_all, kr4_new, p['mla_wk_t'], p['mla_wv'], cache_lat, cache_kr_t)


def _mla_proj_sample_body(u_ref, cos_ref, slo_ref, shi_ref, qnorm_ref, wq_ref, gn_ref, gr_ref, kvnorm_ref, gkr_ref,
                          _lat_alias, _kr_alias, qn_ref, qr_ref, kr4_ref, lat_ref, kr_ref):
    gb, r, _ = u_ref.shape
    m = gb * r
    nk = MLA_HEADS * MLA_NOPE
    u = u_ref[...]
    tab = lambda t_ref: jnp.broadcast_to(t_ref[...], (gb, r, LANES)).reshape(m, LANES)
    cos, slo, shi = tab(cos_ref), tab(slo_ref), tab(shi_ref)
    i = lax.broadcasted_iota(jnp.int32, (LANES, LANES), 0) // MLA_ROPE
    j = lax.broadcasted_iota(jnp.int32, (LANES, LANES), 1) // MLA_ROPE
    ones_r = (i == j).astype(BF16)

    cq = _rms(u[..., :Q_LORA], qnorm_ref[...]).reshape(m, Q_LORA)
    q = jnp.dot(cq.astype(BF16), wq_ref[...], preferred_element_type=F32)
    qn, qr = q[:, :nk], q[:, nk:]
    ssn = jnp.dot((qn * qn).astype(BF16), _head_ones(nk), preferred_element_type=F32)
    qn_ref[...] = (qn * lax.rsqrt(ssn * (1.0 / MLA_NOPE) + EPS) * gn_ref[...]).reshape(gb, r, nk)
    ssr = jnp.dot((qr * qr).astype(BF16), ones_r, preferred_element_type=F32)
    qr = qr * lax.rsqrt(ssr * (1.0 / MLA_ROPE) + EPS) * gr_ref[...]
    qr_ref[...] = _rope(qr, cos, slo, shi).reshape(gb, r, LANES)

    lat_ref[...] = _rms(u[..., Q_LORA:Q_LORA + KV_LORA], kvnorm_ref[...])
    kr = u[..., Q_LORA + KV_LORA:].reshape(m, LANES)
    kr = kr * lax.rsqrt(jnp.mean(kr * kr, axis=-1, keepdims=True) + EPS) * gkr_ref[...]
    kr = _rope(kr, cos, slo, shi)
    kr4_ref[...] = kr.reshape(gb, r, LANES)
    kr_ref[...] = kr[:, :MLA_ROPE].reshape(gb, r, MLA_ROPE)


def _mla_proj_body(u_ref, cos_ref, slo_ref, shi_ref, qnorm_ref, wq_ref, gq_ref, kvnorm_ref, gkr_ref,
                   wkv_ref, gk_ref, _lat_alias, _kr_alias, q_ref, k_ref, v_ref, lat_ref, kr_ref):
    gb, r, _ = u_ref.shape
    m = gb * r
    u = u_ref[...]
    tab = lambda t_ref: jnp.broadcast_to(t_ref[...], (gb, r, MLA_BLOCK)).reshape(m, MLA_BLOCK)
    cos, slo, shi = tab(cos_ref), tab(slo_ref), tab(shi_ref)
    tile = lambda t: jnp.concatenate([t] * MLA_HEADS, axis=1)
    ones = _segment_ones()
    lane = lax.broadcasted_iota(jnp.int32, (1, MLA_BLOCK), 1)
    inv_cnt = jnp.where(lane < MLA_NOPE_OFF, 1.0 / MLA_ROPE, 1.0 / MLA_NOPE)

    cq = _rms(u[..., :Q_LORA], qnorm_ref[...]).reshape(m, Q_LORA)
    q = jnp.dot(cq.astype(BF16), wq_ref[...], preferred_element_type=F32)
    q = _blocked_rms(q, ones, inv_cnt, gq_ref[...])
    q = _rope(q, tile(cos), tile(slo), tile(shi))
    q_ref[...] = q.reshape(gb, r, MLA_QK).astype(q_ref.dtype)

    lat = _rms(u[..., Q_LORA:Q_LORA + KV_LORA], kvnorm_ref[...])
    lat_ref[...] = lat
    kr = u[..., Q_LORA + KV_LORA:].reshape(m, MLA_BLOCK)
    kr = kr * lax.rsqrt(jnp.sum(kr * kr, axis=-1, keepdims=True) * (1.0 / MLA_ROPE) + EPS) * gkr_ref[...]
    kr = _rope(kr, cos, slo, shi)
    kr_ref[...] = kr[:, :MLA_ROPE].reshape(gb, r, MLA_ROPE)

    kv = jnp.dot(lat.reshape(m, KV_LORA).astype(BF16), wkv_ref[...], preferred_element_type=F32)
    k = _blocked_rms(kv[:, :MLA_QK], ones, inv_cnt, gk_ref[...]) + tile(kr)
    k_ref[...] = k.reshape(gb, r, MLA_QK).astype(k_ref.dtype)
    lane_v = lax.broadcasted_iota(jnp.int32, (1, MLA_QK), 1) % MLA_BLOCK
    v = kv[:, MLA_QK:] + jnp.where(lane_v == MLA_V, 1.0, 0.0)
    v_ref[...] = v.reshape(gb, r, MLA_QK).astype(v_ref.dtype)


def _rope_tables(pos0, l, copies):
    half = MLA_ROPE // 2
    inv_freq = ROPE_THETA ** (-jnp.arange(half, dtype=F32) / half)
    ang = (pos0 + jnp.arange(l)).astype(F32)[:, None] * inv_freq
    cos, sin = jnp.cos(ang), jnp.sin(ang)
    rest = LANES - copies * MLA_ROPE
    lay = lambda t, fill: jnp.concatenate([jnp.tile(t, (1, copies)), jnp.full((l, rest), fill, F32)], axis=1)
    zero = jnp.zeros_like(sin)
    return (lay(jnp.concatenate([cos, cos], 1), 1.0), lay(jnp.concatenate([-sin, zero], 1), 0.0),
            lay(jnp.concatenate([zero, sin], 1), 0.0))


def _head_blocked(rope_part, nope_part):
    ref = rope_part if rope_part is not None else nope_part
    lead = ref.shape[:-2]
    cols = []
    for h in range(MLA_HEADS):
        rp = rope_part[..., h, :] if rope_part is not None else jnp.zeros(lead + (MLA_ROPE,), F32)
        npart = nope_part[..., h, :] if nope_part is not None else jnp.zeros(lead + (MLA_NOPE,), F32)
        cols += [rp, jnp.zeros(lead + (MLA_NOPE_OFF - MLA_ROPE,), F32), npart]
    return jnp.concatenate(cols, axis=-1)


def _mla_project_call(body, name, u, tables, params, layer, outs, stacks, rows):
    b, l, _ = u.shape
    t = _Tiling(b, l, rows)
    uspec = lambda n: pl.BlockSpec((t.gb, t.r, n), lambda i: (*t.tok_index(i), 0))
    tspec = pl.BlockSpec((t.r, LANES), lambda i: (i % t.nl, 0))
    sspec = lambda a: pl.BlockSpec((None, t.gb, t.r, a.shape[-1]), lambda i: (layer, *t.tok_index(i), 0))
    n_in = 4 + len(params)
    return pl.pallas_call(
        body,
        out_shape=[jax.ShapeDtypeStruct((b, l, n), dt) for n, dt in outs]
                  + [jax.ShapeDtypeStruct(a.shape, a.dtype) for a in stacks],
        grid=(t.steps,),
        in_specs=[uspec(PROJ_MLA), tspec, tspec, tspec] + [_lspec(a, layer) for a in params]
                 + [_any_spec() for _ in stacks],
        out_specs=[uspec(n) for n, _ in outs] + [sspec(a) for a in stacks],
        input_output_aliases={n_in + k: len(outs) + k for k in range(len(stacks))},
        compiler_params=_cparams("arbitrary"),
        name=name,
    )(u, *tables, *params, *stacks)


def mla_project_prompt(u, tables, p, layer, lat_out, kr_out, *, rows=512):
    params = [p['mla_q_norm'], p['mla_wq_blocked'], p['mla_gq_blocked'], p['mla_kv_norm'], p['mla_gkr_pad'],
              p['mla_wkv_blocked'], p['mla_gk_blocked']]
    return _mla_project_call(_mla_proj_body, "mla_project_prompt", u, tables, params, layer,
                             [(MLA_QK, BF16)] * 3, [lat_out, kr_out], rows)


def mla_project_sample(u, tables, p, layer, lat_out, kr_out, *, rows=512):
    params = [p['mla_q_norm'], p['mla_wq_compact'], p['mla_gn_compact'], p['mla_gr_compact'], p['mla_kv_norm'],
              p['mla_gkr_tiled']]
    return _mla_project_call(_mla_proj_sample_body, "mla_project_sample", u, tables, params, layer,
                             [(MLA_HEADS * MLA_NOPE, F32), (LANES, F32), (LANES, F32)], [lat_out, kr_out], rows)


def _flash_body(q_ref, k_ref, v_ref, o_ref, *, tq, nq):
    i = pl.program_id(1)
    row = lax.broadcasted_iota(jnp.int32, (tq, tq), 0)
    col = lax.broadcasted_iota(jnp.int32, (tq, tq), 1)
    c = MLA_SCALE * math.log2(math.e)
    blk = lambda h: pl.ds(h * MLA_BLOCK, MLA_BLOCK)
    qs = [q_ref[:, blk(h)] for h in range(MLA_HEADS)]

    def update(carry, rows, mask):
        new = []
        for h in range(MLA_HEADS):
            m, acc = carry[h]
            s = _dot_nt(qs[h], k_ref[rows, blk(h)])
            if mask:
                s = jnp.where(row >= col, s, -jnp.inf)
            m_new = jnp.maximum(m, jnp.max(s, axis=-1, keepdims=True))
            alpha = jnp.exp2((m - m_new) * c)
            p = jnp.exp2((s - m_new) * c)
            acc = alpha * acc + jnp.dot(p.astype(BF16), v_ref[rows, blk(h)], preferred_element_type=F32)
            new.append((m_new, acc))
        return tuple(new)

    def run(n_past):
        carry = tuple((jnp.full((tq, 1), -jnp.inf, F32), jnp.zeros((tq, MLA_BLOCK), F32)) for _ in range(MLA_HEADS))
        for j in range(n_past):
            carry = update(carry, pl.ds(j * tq, tq), False)
        carry = update(carry, pl.ds(n_past * tq, tq), True)
        o_ref[...] = jnp.concatenate([acc[:, :MLA_V] / acc[:, MLA_V:MLA_V + 1] for _, acc in carry], axis=1)

    for n_past in range(nq):
        pl.when(i == n_past)(functools.partial(run, n_past))


def mla_prompt_attention(q, k, v, *, tq=512):
    b, l, _ = q.shape
    tq = min(tq, l)
    assert l % tq == 0
    return pl.pallas_call(
        functools.partial(_flash_body, tq=tq, nq=l // tq),
        out_shape=jax.ShapeDtypeStruct((b, l, MLA_HEADS * MLA_V), F32),
        grid=(b, l // tq),
        in_specs=[pl.BlockSpec((None, tq, MLA_QK), lambda bi, i: (bi, i, 0)),
                  pl.BlockSpec((None, l, MLA_QK), lambda bi, i: (bi, 0, 0)),
                  pl.BlockSpec((None, l, MLA_QK), lambda bi, i: (bi, 0, 0))],
        out_specs=pl.BlockSpec((None, tq, MLA_HEADS * MLA_V), lambda bi, i: (bi, i, 0)),
        compiler_params=_cparams("arbitrary", "arbitrary"),
        name="mla_prompt_attention",
    )(q, k, v)


def prepare_params(raw):
    depth = raw['w_in'].shape[0]
    row = lambda a: a.reshape(depth, 1, -1)
    pad_heads = lambda a: jnp.pad(a, ((0, 0), (0, LANES - SSD_HEADS))).reshape(depth, 1, LANES)
    eye = jnp.eye(LRU_BLOCKS, dtype=F32)
    block_diag = lambda w: jnp.einsum('lhij,hg->lhigj', w, eye).reshape(depth, LRU_W, LRU_W)
    ones_h = jnp.ones((1, MLA_HEADS, 1), F32)
    per_head = lambda g: ones_h * g[:, None, :]
    tile4 = lambda g: jnp.tile(g, (1, MLA_HEADS)).reshape(depth, 1, -1)
    wq = raw['mla_w_uq'].reshape(depth, Q_LORA, MLA_HEADS, MLA_NOPE + MLA_ROPE)
    wkv = raw['mla_w_ukv'].reshape(depth, KV_LORA, MLA_HEADS, MLA_NOPE + MLA_V)
    w_k = wkv[..., :MLA_NOPE].reshape(depth, KV_LORA, MLA_HEADS * MLA_NOPE)
    w_v = wkv[..., MLA_NOPE:]
    w_v_blocked = jnp.pad(w_v, ((0, 0), (0, 0), (0, 0), (0, MLA_BLOCK - MLA_V))).reshape(depth, KV_LORA, MLA_QK)
    return dict(
        norm_ffn1=row(raw['norm_ffn1']), norm_mix=row(raw['norm_mix']), norm_ffn2=row(raw['norm_ffn2']),
        out_norm=row(raw['out_norm']),
        ffn1_in=raw['w_ffn1_in'].astype(BF16), ffn1_out=raw['w_ffn1_out'].astype(BF16),
        ffn2_in=raw['w_ffn2_in'].astype(BF16), ffn2_out=raw['w_ffn2_out'].astype(BF16),
        w_in=raw['w_in'], w_out=raw['w_out'].astype(BF16),
        ssd_conv_w=raw['ssd_conv_w'], ssd_conv_b=row(raw['ssd_conv_b']), ssd_dt_bias=pad_heads(raw['ssd_dt_bias']),
        ssd_a_log=pad_heads(raw['ssd_a_log']), ssd_d=row(jnp.repeat(raw['ssd_d'], SSD_P, axis=1)),
        lru_conv_w=raw['lru_conv_w'], lru_conv_b=row(raw['lru_conv_b']),
        lru_w_ri=jnp.concatenate([block_diag(raw['lru_w_r']), block_diag(raw['lru_w_i'])], axis=2).astype(BF16),
        lru_b_ri=row(jnp.concatenate([raw['lru_b_r'], raw['lru_b_i']], axis=1)), lru_lambda=row(raw['lru_lambda']),
        hgrn_lb=raw['hgrn_lb'],
        mla_q_norm=row(raw['mla_q_norm']), mla_kv_norm=row(raw['mla_kv_norm']),
        mla_wq_blocked=_head_blocked(wq[..., MLA_NOPE:], wq[..., :MLA_NOPE]).astype(BF16),
        mla_gq_blocked=row(_head_blocked(per_head(raw['mla_qn_rope']), per_head(raw['mla_qn_nope']))),
        mla_gkr_pad=row(jnp.pad(raw['mla_kn_rope'], ((0, 0), (0, MLA_BLOCK - MLA_ROPE)))),
        mla_wkv_blocked=jnp.concatenate([_head_blocked(None, wkv[..., :MLA_NOPE]), w_v_blocked], axis=2).astype(BF16),
        mla_gk_blocked=row(_head_blocked(None, per_head(raw['mla_kn_nope']))),
        mla_wq_compact=jnp.concatenate([wq[..., :MLA_NOPE].reshape(depth, Q_LORA, -1),
                                        wq[..., MLA_NOPE:].reshape(depth, Q_LORA, -1)], axis=2).astype(BF16),
        mla_gn_compact=tile4(raw['mla_qn_nope'] * raw['mla_kn_nope']), mla_gr_compact=tile4(raw['mla_qn_rope']),
        mla_gkr_tiled=tile4(raw['mla_kn_rope']),
        mla_wk_t=jnp.swapaxes(w_k, 1, 2).astype(BF16),
        mla_wv=w_v.reshape(depth, KV_LORA, MLA_HEADS * MLA_V).astype(BF16),
    )


def _layer(x, mods, layer, row0, p, init_layer, init, outs, tables, paged, seq_block):
    conv_a, ssm_a, conv_b, h_b, s_c = init
    lat_o, kr_o, ssm_o, conva_o, h_o, convb_o, s_o = outs
    x = ffn_halfstep(x, mods, layer, row0, 0, p['norm_ffn1'], p['ffn1_in'], p['ffn1_out'])
    kr_copies = 1 if paged is None else LANES // MLA_ROPE
    u_a, u_b, u_c, u_d = mixer_projection(x, mods, layer, row0, p['norm_mix'], p['w_in'], kr_copies)
    y_a, conva_o, ssm_o = ssd_mixer(u_a, conv_a, ssm_a, init_layer, p, layer, conva_o, ssm_o, gb=seq_block)
    y_b, convb_o, h_o = rglru_mixer(u_b, conv_b, h_b, init_layer, p, layer, convb_o, h_o, gb=seq_block)
    y_c, s_o = hgrn2_mixer(u_c, s_c, init_layer, p['hgrn_lb'], layer, s_o, gb=seq_block)
    if paged is None:
        q, k, v, lat_o, kr_o = mla_project_prompt(u_d, tables, p, layer, lat_o, kr_o)
        y_d = mla_prompt_attention(q, k, v)
    else:
        cache_lat, cache_kr_t, page_table = paged
        qn, qr, kr4, lat_o, kr_o = mla_project_sample(u_d, tables, p, layer, lat_o, kr_o)
        y_d = mla_sample_attention(qn, qr, lat_o, kr4, cache_lat, cache_kr_t, page_table, layer, p)
    x = output_merge(x, (y_a, y_b, y_c, y_d), mods, layer, row0, p['out_norm'], p['w_out'])
    x = ffn_halfstep(x, mods, layer, row0, 6, p['norm_ffn2'], p['ffn2_in'], p['ffn2_out'])
    return x, (lat_o, kr_o, ssm_o, conva_o, h_o, convb_o, s_o)


def _result_stacks(b, l):
    z = lambda *s: jnp.zeros((DEPTH, b) + s, F32)
    return (z(l, KV_LORA), z(l, MLA_ROPE), z(SSD_HEADS * SSD_P, SSD_N), z(CONV_W - 1, SSD_CONV_DIM),
            z(1, LRU_W), z(CONV_W - 1, LRU_W), z(HG_HEADS, HG_DK, HG_DV))


def kernel(x_prompt, x_sample, cache_mla_latent, cache_mla_krope, state_ssd, state_ssd_conv, state_lru,
           state_lru_conv, state_hgrn, page_table, c_prompt, c_sample, w_ada, b_ada, norm_ffn1, w_ffn1_in,
           w_ffn1_out, norm_mix, w_in, ssd_conv_w, ssd_conv_b, ssd_dt_bias, ssd_a_log, ssd_d, lru_conv_w,
           lru_conv_b, lru_w_r, lru_b_r, lru_w_i, lru_b_i, lru_lambda, hgrn_lb, mla_q_norm, mla_w_uq,
           mla_kv_norm, mla_w_ukv, mla_qn_nope, mla_qn_rope, mla_kn_nope, mla_kn_rope, out_norm, w_out,
           norm_ffn2, w_ffn2_in, w_ffn2_out):
    p = prepare_params(dict(
        norm_ffn1=norm_ffn1, w_ffn1_in=w_ffn1_in, w_ffn1_out=w_ffn1_out, norm_mix=norm_mix, w_in=w_in,
        ssd_conv_w=ssd_conv_w, ssd_conv_b=ssd_conv_b, ssd_dt_bias=ssd_dt_bias, ssd_a_log=ssd_a_log, ssd_d=ssd_d,
        lru_conv_w=lru_conv_w, lru_conv_b=lru_conv_b, lru_w_r=lru_w_r, lru_b_r=lru_b_r, lru_w_i=lru_w_i,
        lru_b_i=lru_b_i, lru_lambda=lru_lambda, hgrn_lb=hgrn_lb, mla_q_norm=mla_q_norm, mla_w_uq=mla_w_uq,
        mla_kv_norm=mla_kv_norm, mla_w_ukv=mla_w_ukv, mla_qn_nope=mla_qn_nope, mla_qn_rope=mla_qn_rope,
        mla_kn_nope=mla_kn_nope, mla_kn_rope=mla_kn_rope, out_norm=out_norm, w_out=w_out, norm_ffn2=norm_ffn2,
        w_ffn2_in=w_ffn2_in, w_ffn2_out=w_ffn2_out))
    bp, lp = x_prompt.shape[:2]
    bs, ls = x_sample.shape[:2]
    dt = x_prompt.dtype
    mods = ada_modulation(jnp.concatenate([c_sample, c_prompt], axis=0), w_ada, b_ada)
    mods = mods.reshape(DEPTH, bs + bp, 1, N_MOD * D_MODEL)
    cache_kr_t = jnp.swapaxes(cache_mla_krope, 2, 3)
    nrow = SSD_HEADS * SSD_P
    zero = lambda *s: jnp.zeros((1, bp) + s, F32)
    init_p = (zero(CONV_W - 1, SSD_CONV_DIM), zero(nrow, SSD_N), zero(CONV_W - 1, LRU_W), zero(1, LRU_W),
              zero(HG_HEADS, HG_DK, HG_DV))
    init_s = (state_ssd_conv, state_ssd.reshape(DEPTH, bs, nrow, SSD_N), state_lru_conv,
              state_lru.reshape(DEPTH, bs, 1, LRU_W), state_hgrn)
    tables_p = _rope_tables(0, lp, 1)
    tables_s = _rope_tables(PAST_LEN, ls, LANES // MLA_ROPE)
    out_p, out_s = _result_stacks(bp, lp), _result_stacks(bs, ls)
    yp, ys = x_prompt, x_sample
    sample_block = math.gcd(bs, SUBLANES)
    for l in range(DEPTH):
        yp, out_p = _layer(yp, mods, l, bs, p, 0, init_p, out_p, tables_p, None, 1)
        ys, out_s = _layer(ys, mods, l, 0, p, l, init_s, out_s, tables_s,
                           (cache_mla_latent, cache_kr_t, page_table), sample_block)

    def finish(outs, b):
        lat, kr, ssm, conva, h, convb, s = outs
        return (lat, kr, ssm.reshape(DEPTH, b, SSD_HEADS, SSD_P, SSD_N), conva, h.reshape(DEPTH, b, LRU_W), convb, s)

    lat_p, kr_p, ssd_p, ssdc_p, lru_p, lruc_p, hg_p = [a.astype(dt) for a in finish(out_p, bp)]
    lat_s, kr_s, ssd_s, ssdc_s, lru_s, lruc_s, hg_s = [a.astype(dt) for a in finish(out_s, bs)]
    return (yp, ys, lat_p, lat_s, kr_p, kr_s, ssd_p, ssd_s, ssdc_p, ssdc_s, lru_p, lru_s, lruc_p, lruc_s, hg_p, hg_s)
```

```python
import functools
import math

import jax
import jax.numpy as jnp
from jax import lax
from jax.experimental import pallas as pl
from jax.experimental.pallas import tpu as pltpu

F32 = jnp.float32
BF16 = jnp.bfloat16

D_MODEL = 1024
DEPTH = 4
PAST_LEN = 8192
PAGE_SIZE = 128
W_GROUP = 256
HEAD_DIM = 64
SSD_HEADS = 4
SSD_P = 64
SSD_N = 128
SSD_GROUPS = 2
SSD_CHUNK = 128
CONV_W = 4
SSD_CONV_DIM = W_GROUP + 2 * SSD_GROUPS * SSD_N
LRU_W = W_GROUP
LRU_BLOCKS = 4
LRU_BW = LRU_W // LRU_BLOCKS
LRU_C = 8.0
HG_HEADS = 4
HG_DK = 64
HG_DV = 64
HG_CHUNK = 64
MLA_HEADS = 4
MLA_NOPE = 64
MLA_ROPE = 32
MLA_V = 64
Q_LORA = 256
KV_LORA = 128
ROPE_THETA = 10000.0
MLA_SCALE = 1.0 / math.sqrt(MLA_NOPE + MLA_ROPE)
D_FF = 2816
N_MOD = 9
EPS = 1e-6
IN_SSD = W_GROUP + SSD_CONV_DIM + SSD_HEADS
IN_LRU = 2 * LRU_W
IN_HG = 4 * HG_HEADS * HG_DK
IN_MLA = Q_LORA + KV_LORA + MLA_ROPE
OFF_LRU = IN_SSD
OFF_HG = OFF_LRU + IN_LRU
OFF_MLA = OFF_HG + IN_HG
N_IN = OFF_MLA + IN_MLA

LANES = 128
SUBLANES = 8
VMEM_BYTES_V7X = 64 * 1024 * 1024
VMEM_LIMIT = VMEM_BYTES_V7X * 3 // 4

PROJ_SSD = W_GROUP + SSD_CONV_DIM + LANES
PROJ_LRU = IN_LRU
PROJ_HG = IN_HG
PROJ_MLA = Q_LORA + KV_LORA + LANES
PROJ_ALL = PROJ_SSD + PROJ_LRU + PROJ_HG + PROJ_MLA


def _cparams(*sem):
    return pltpu.CompilerParams(dimension_semantics=sem, vmem_limit_bytes=VMEM_LIMIT)


def _silu(x):
    return x * jax.nn.sigmoid(x)


def _rms(x, w):
    return x * lax.rsqrt(jnp.mean(x * x, axis=-1, keepdims=True) + EPS) * w


def _lspec(arr, layer):
    nd = arr.ndim - 1
    return pl.BlockSpec((None,) + arr.shape[1:], lambda *_: (layer,) + (0,) * nd)


def _any_spec():
    return pl.BlockSpec(memory_space=pl.ANY)


def _mixer_out_dtype(l):
    return BF16 if l % (2 * SUBLANES) == 0 else F32


def _ada_body(c_ref, w_ref, b_ref, o_ref):
    a = _silu(c_ref[...]).astype(BF16)
    o_ref[...] = jnp.dot(a, w_ref[...].astype(BF16), preferred_element_type=F32) + b_ref[...]


def ada_modulation(c_all, w_ada, b_ada, *, tn=1152):
    r, d = c_all.shape
    depth, _, n = w_ada.shape
    return pl.pallas_call(
        _ada_body,
        out_shape=jax.ShapeDtypeStruct((depth, r, n), F32),
        grid=(depth, n // tn),
        in_specs=[
            pl.BlockSpec((r, d), lambda l, j: (0, 0)),
            pl.BlockSpec((None, d, tn), lambda l, j: (l, 0, j)),
            pl.BlockSpec((None, 1, tn), lambda l, j: (l, 0, j)),
        ],
        out_specs=pl.BlockSpec((None, r, tn), lambda l, j: (l, 0, j)),
        compiler_params=_cparams("arbitrary", "arbitrary"),
        name="ada_modulation",
    )(c_all, w_ada, b_ada.reshape(depth, 1, n))


class _Tiling:
    def __init__(self, b, l, rows):
        if l >= rows:
            assert l % rows == 0
            self.gb, self.r = 1, rows
        else:
            gb = min(b, max(1, rows // l))
            assert b % gb == 0
            self.gb, self.r = gb, l
        self.b, self.l = b, l
        self.nb = b // self.gb
        self.nl = l // self.r
        self.steps = self.nb * self.nl
        self.m = self.gb * self.r

    def tok_index(self, i):
        return (i // self.nl, i % self.nl)


def _mod_spec(t, layer, row0, k):
    assert row0 % t.gb == 0
    return pl.BlockSpec((None, t.gb, 1, D_MODEL), lambda i, *_: (layer, row0 // t.gb + i // t.nl, 0, k))


def _ffn_body(x_ref, sh_ref, sc_ref, gt_ref, nw_ref, w_in_ref, w_out_ref, o_ref):
    gb, r, d = x_ref.shape
    x = x_ref[...]
    h = _rms(x, nw_ref[...]) * (1.0 + sc_ref[...]) + sh_ref[...]
    gu = jnp.dot(h.reshape(gb * r, d).astype(BF16), w_in_ref[...], preferred_element_type=F32)
    a = _silu(gu[:, :D_FF]) * gu[:, D_FF:]
    y = jnp.dot(a.astype(BF16), w_out_ref[...], preferred_element_type=F32)
    o_ref[...] = x + 0.5 * gt_ref[...] * y.reshape(gb, r, d)


def ffn_halfstep(x, mods, layer, row0, k0, norm_w, w_in, w_out, *, rows=512):
    b, l, d = x.shape
    t = _Tiling(b, l, rows)
    xspec = pl.BlockSpec((t.gb, t.r, d), lambda i: (*t.tok_index(i), 0))
    return pl.pallas_call(
        _ffn_body,
        out_shape=jax.ShapeDtypeStruct(x.shape, F32),
        grid=(t.steps,),
        in_specs=[xspec, _mod_spec(t, layer, row0, k0), _mod_spec(t, layer, row0, k0 + 1),
                  _mod_spec(t, layer, row0, k0 + 2), _lspec(norm_w, layer), _lspec(w_in, layer), _lspec(w_out, layer)],
        out_specs=xspec,
        compiler_params=_cparams("arbitrary"),
        name="ffn_halfstep",
    )(x, mods, mods, mods, norm_w, w_in, w_out)


def _pack_w_in(w_ref, wp_ref, kr_copies):
    d = w_ref.shape[0]
    rows = 256
    off_kr = OFF_MLA + Q_LORA + KV_LORA
    dst_mla = PROJ_SSD + PROJ_LRU + PROJ_HG

    def chunk(c, _):
        r = pl.ds(pl.multiple_of(c * rows, rows), rows)
        put = lambda dst, src, n: wp_ref.__setitem__((r, pl.ds(dst, n)), w_ref[r, pl.ds(src, n)].astype(BF16))
        zero = lambda dst, n: wp_ref.__setitem__((r, pl.ds(dst, n)), jnp.zeros((rows, n), BF16))
        put(0, 0, W_GROUP + SSD_CONV_DIM)
        zero(W_GROUP + SSD_CONV_DIM, LANES)
        put(W_GROUP + SSD_CONV_DIM, W_GROUP + SSD_CONV_DIM, SSD_HEADS)
        put(PROJ_SSD, OFF_LRU, IN_LRU)
        put(PROJ_SSD + PROJ_LRU, OFF_HG, IN_HG)
        put(dst_mla, OFF_MLA, Q_LORA + KV_LORA)
        zero(dst_mla + Q_LORA + KV_LORA, LANES)
        for k in range(kr_copies):
            put(dst_mla + Q_LORA + KV_LORA + k * MLA_ROPE, off_kr, MLA_ROPE)

    lax.fori_loop(0, d // rows, chunk, None)


def _proj_body(x_ref, sh_ref, sc_ref, nw_ref, w_ref, ssd_ref, lru_ref, hg_ref, mla_ref, wp_ref, *, kr_copies):
    @pl.when(pl.program_id(0) == 0)
    def _():
        _pack_w_in(w_ref, wp_ref, kr_copies)

    gb, r, d = x_ref.shape
    h = _rms(x_ref[...], nw_ref[...]) * (1.0 + sc_ref[...]) + sh_ref[...]
    h = h.reshape(gb * r, d).astype(BF16)
    off = 0
    for o_ref in (ssd_ref, lru_ref, hg_ref, mla_ref):
        n = o_ref.shape[-1]
        u = jnp.dot(h, wp_ref[:, off:off + n], preferred_element_type=F32)
        o_ref[...] = u.reshape(gb, r, n)
        off += n


def mixer_projection(x, mods, layer, row0, norm_w, w_in, kr_copies, *, rows=512):
    b, l, d = x.shape
    t = _Tiling(b, l, rows)
    widths = (PROJ_SSD, PROJ_LRU, PROJ_HG, PROJ_MLA)
    xspec = pl.BlockSpec((t.gb, t.r, d), lambda i: (*t.tok_index(i), 0))
    w_spec = pl.BlockSpec((None, d, N_IN), lambda i: (layer, 0, 0), pipeline_mode=pl.Buffered(1))
    return pl.pallas_call(
        functools.partial(_proj_body, kr_copies=kr_copies),
        out_shape=[jax.ShapeDtypeStruct((b, l, n), F32) for n in widths],
        grid=(t.steps,),
        in_specs=[xspec, _mod_spec(t, layer, row0, 3), _mod_spec(t, layer, row0, 4), _lspec(norm_w, layer), w_spec],
        out_specs=[pl.BlockSpec((t.gb, t.r, n), lambda i: (*t.tok_index(i), 0)) for n in widths],
        scratch_shapes=[pltpu.VMEM((d, PROJ_ALL), BF16)],
        compiler_params=_cparams("arbitrary"),
        name="mixer_projection",
    )(x, mods, mods, norm_w, w_in)


def _head_ones(n):
    i = lax.broadcasted_iota(jnp.int32, (n, n), 0) // HEAD_DIM
    j = lax.broadcasted_iota(jnp.int32, (n, n), 1) // HEAD_DIM
    return (i == j).astype(BF16)


def _out_body(x_ref, ya_ref, yb_ref, yc_ref, yd_ref, gt_ref, nw_ref, w_ref, o_ref):
    gb, r, d = x_ref.shape
    ones = _head_ones(W_GROUP)
    acc = jnp.zeros((gb * r, d), F32)
    for k, y_ref in enumerate((ya_ref, yb_ref, yc_ref, yd_ref)):
        y = y_ref[...].astype(F32).reshape(gb * r, W_GROUP)
        ss = jnp.dot((y * y).astype(BF16), ones, preferred_element_type=F32)
        yn = y * lax.rsqrt(ss * (1.0 / HEAD_DIM) + EPS) * nw_ref[:, k * W_GROUP:(k + 1) * W_GROUP]
        acc += jnp.dot(yn.astype(BF16), w_ref[k * W_GROUP:(k + 1) * W_GROUP, :], preferred_element_type=F32)
    o_ref[...] = x_ref[...] + gt_ref[...] * acc.reshape(gb, r, d)


def output_merge(x, ys, mods, layer, row0, out_norm, w_out, *, rows=512):
    b, l, d = x.shape
    t = _Tiling(b, l, rows)
    xspec = pl.BlockSpec((t.gb, t.r, d), lambda i: (*t.tok_index(i), 0))
    yspec = pl.BlockSpec((t.gb, t.r, W_GROUP), lambda i: (*t.tok_index(i), 0))
    return pl.pallas_call(
        _out_body,
        out_shape=jax.ShapeDtypeStruct(x.shape, F32),
        grid=(t.steps,),
        in_specs=[xspec, yspec, yspec, yspec, yspec, _mod_spec(t, layer, row0, 5),
                  _lspec(out_norm, layer), _lspec(w_out, layer)],
        out_specs=xspec,
        compiler_params=_cparams("arbitrary"),
        name="output_merge",
    )(x, *ys, mods, out_norm, w_out)


def _cumsum_rows(x):
    n = x.shape[0]
    row = lax.broadcasted_iota(jnp.int32, x.shape, 0)
    d = 1
    while d < n:
        x = x + jnp.where(row >= d, pltpu.roll(x, d, 0), 0.0)
        d *= 2
    return x


def _dot_nt(a, b):
    return lax.dot_general(a, b, (((1,), (1,)), ((), ())), preferred_element_type=F32)


def _dot_tn(a, b):
    return lax.dot_general(a, b, (((0,), (0,)), ((), ())), preferred_element_type=F32)


def _conv_init(ext_ref, cs_ref, q):
    ext_ref[:, pl.ds(q, SUBLANES), :] = jnp.zeros((ext_ref.shape[0], SUBLANES, ext_ref.shape[2]), F32)
    ext_ref[:, pl.ds(q + SUBLANES - (CONV_W - 1), CONV_W - 1), :] = cs_ref[...]


def _causal_conv(ext_ref, s, x, cw_ref, cb_ref, q):
    ext_ref[s, pl.ds(0, SUBLANES), :] = ext_ref[s, pl.ds(q, SUBLANES), :]
    ext_ref[s, pl.ds(SUBLANES, q), :] = x
    out = cb_ref[...]
    for k in range(CONV_W):
        out = out + cw_ref[k:k + 1, :] * ext_ref[s, pl.ds(SUBLANES - (CONV_W - 1) + k, q), :]
    return out


def _state_specs(gb, init_layer, layer, *shapes):
    ins = [pl.BlockSpec((None, gb) + s, lambda i, c, n=len(s): (init_layer, i) + (0,) * n) for s in shapes]
    outs = [pl.BlockSpec((None, gb) + s, lambda i, c, n=len(s): (layer, i) + (0,) * n) for s in shapes]
    return ins, outs


def _ssd_body(u_ref, cs_ref, st_ref, cw_ref, cb_ref, dtb_ref, alog_ref, dsk_ref, _c_alias, _s_alias,
              y_ref, cnew_ref, snew_ref, ext_ref, h_ref, *, q, nsteps):
    step = pl.program_id(1)
    gb, tb, _ = u_ref.shape
    hp = SSD_HEADS // SSD_GROUPS * SSD_P

    @pl.when(step == 0)
    def _():
        _conv_init(ext_ref, cs_ref, q)
        h_ref[...] = st_ref[...]

    row = lax.broadcasted_iota(jnp.int32, (q, q), 0)
    col = lax.broadcasted_iota(jnp.int32, (q, q), 1)
    causal = row >= col
    a = -jnp.exp(alog_ref[...])

    def chunk(s, i):
        tok = pl.ds(pl.multiple_of(i * q, q), q)
        u = u_ref[s, tok, :]
        z = u[:, :W_GROUP]
        conv = _causal_conv(ext_ref, s, u[:, W_GROUP:W_GROUP + SSD_CONV_DIM], cw_ref, cb_ref, q)
        xbc = _silu(conv)
        xs = xbc[:, :W_GROUP]
        bm = xbc[:, W_GROUP:W_GROUP + SSD_GROUPS * SSD_N].astype(BF16)
        cm = xbc[:, W_GROUP + SSD_GROUPS * SSD_N:].astype(BF16)
        dt = jax.nn.softplus(u[:, W_GROUP + SSD_CONV_DIM:] + dtb_ref[...])
        acum = _cumsum_rows(dt * a)
        acum_t = acum.T
        a_end = acum[q - 1:q, :]
        ys = []
        for g in range(SSD_GROUPS):
            bg = bm[:, g * SSD_N:(g + 1) * SSD_N]
            cg = cm[:, g * SSD_N:(g + 1) * SSD_N]
            cb = _dot_nt(cg, bg)
            h_prev = h_ref[s, g * hp:(g + 1) * hp, :]
            y_off = _dot_nt(cg, h_prev.astype(BF16))
            for j in range(SSD_HEADS // SSD_GROUPS):
                h = g * (SSD_HEADS // SSD_GROUPS) + j
                dt_h = dt[:, h:h + 1]
                ac_h = acum[:, h:h + 1]
                ae_h = a_end[:, h:h + 1]
                decay = jnp.exp(jnp.where(causal, ac_h - acum_t[h:h + 1, :], -jnp.inf))
                xs_h = xs[:, h * SSD_P:(h + 1) * SSD_P]
                xdt_h = xs_h * dt_h
                y_h = jnp.dot((cb * decay).astype(BF16), xdt_h.astype(BF16), preferred_element_type=F32)
                y_h = y_h + y_off[:, j * SSD_P:(j + 1) * SSD_P] * jnp.exp(ac_h)
                y_h = y_h + dsk_ref[:, h * SSD_P:(h + 1) * SSD_P] * xs_h
                ys.append(y_h)
                w_h = (xdt_h * jnp.exp(ae_h - ac_h)).astype(BF16)
                st_h = _dot_tn(w_h, bg)
                rows = pl.ds(h * SSD_P, SSD_P)
                h_ref[s, rows, :] = jnp.exp(ae_h) * h_ref[s, rows, :] + st_h
        y_ref[s, tok, :] = (jnp.concatenate(ys, axis=1) * _silu(z)).astype(y_ref.dtype)

    for s in range(gb):
        if tb == q:
            chunk(s, 0)
        else:
            lax.fori_loop(0, tb // q, lambda i, _: chunk(s, i), None, unroll=2)

    @pl.when(step == nsteps - 1)
    def _():
        cnew_ref[...] = ext_ref[:, pl.ds(q + SUBLANES - (CONV_W - 1), CONV_W - 1), :]
        snew_ref[...] = h_ref[...]


def ssd_mixer(u, conv_state, ssm_state, init_layer, p, layer, conv_out, ssm_out, *, gb=1, rows=512):
    b, l, _ = u.shape
    q = min(SSD_CHUNK, l)
    tb = min(rows, l)
    assert l % tb == 0 and tb % q == 0 and b % gb == 0 and q % SUBLANES == 0
    nsteps = l // tb
    nrow = SSD_HEADS * SSD_P
    st_in, st_out = _state_specs(gb, init_layer, layer, (CONV_W - 1, SSD_CONV_DIM), (nrow, SSD_N))
    params = [p['ssd_conv_w'], p['ssd_conv_b'], p['ssd_dt_bias'], p['ssd_a_log'], p['ssd_d']]
    return pl.pallas_call(
        functools.partial(_ssd_body, q=q, nsteps=nsteps),
        out_shape=[jax.ShapeDtypeStruct((b, l, W_GROUP), _mixer_out_dtype(l)),
                   jax.ShapeDtypeStruct(conv_out.shape, F32), jax.ShapeDtypeStruct(ssm_out.shape, F32)],
        grid=(b // gb, nsteps),
        in_specs=[pl.BlockSpec((gb, tb, PROJ_SSD), lambda i, c: (i, c, 0)), *st_in,
                  *[_lspec(a, layer) for a in params], _any_spec(), _any_spec()],
        out_specs=[pl.BlockSpec((gb, tb, W_GROUP), lambda i, c: (i, c, 0)), *st_out],
        scratch_shapes=[pltpu.VMEM((gb, q + SUBLANES, SSD_CONV_DIM), F32), pltpu.VMEM((gb, nrow, SSD_N), F32)],
        input_output_aliases={8: 1, 9: 2},
        compiler_params=_cparams("arbitrary", "arbitrary"),
        name="ssd_mixer",
    )(u, conv_state, ssm_state, *params, conv_out, ssm_out)


def _linear_scan_rows(a, b, h_prev):
    n, w = a.shape
    seg = min(n, SUBLANES)
    pos = lax.broadcasted_iota(jnp.int32, a.shape, 0) % seg
    d = 1
    while d < seg:
        m = pos >= d
        b = jnp.where(m, a * pltpu.roll(b, d, 0) + b, b)
        a = jnp.where(m, a * pltpu.roll(a, d, 0), a)
        d *= 2
    hs, carry = [], h_prev
    for k in range(n // seg):
        h_k = b[k * seg:(k + 1) * seg] + a[k * seg:(k + 1) * seg] * carry
        hs.append(h_k)
        carry = h_k[seg - 1:seg, :]
    return jnp.concatenate(hs, axis=0)


def _lru_body(u_ref, cs_ref, h0_ref, cw_ref, cb_ref, wri_ref, bri_ref, lam_ref, _c_alias, _h_alias,
              y_ref, cnew_ref, hnew_ref, ext_ref, h_ref, *, q, nsteps):
    step = pl.program_id(1)
    gb, tb, _ = u_ref.shape

    @pl.when(step == 0)
    def _():
        _conv_init(ext_ref, cs_ref, q)
        h_ref[...] = h0_ref[...]

    sp = jax.nn.softplus(-lam_ref[...])

    def chunk(s, i):
        tok = pl.ds(pl.multiple_of(i * q, q), q)
        u = u_ref[s, tok, :]
        xb = _causal_conv(ext_ref, s, u[:, :LRU_W], cw_ref, cb_ref, q)
        ri = jnp.dot(xb.astype(BF16), wri_ref[...], preferred_element_type=F32) + bri_ref[...]
        r = jax.nn.sigmoid(ri[:, :LRU_W])
        i = jax.nn.sigmoid(ri[:, LRU_W:])
        log_a = -LRU_C * r * sp
        a = jnp.exp(log_a)
        bterm = jnp.sqrt(-jnp.tanh(log_a) * (a * a + 1.0)) * (i * xb)
        h = _linear_scan_rows(a, bterm, h_ref[s])
        h_ref[s] = h[q - 1:q, :]
        y_ref[s, tok, :] = (h * jax.nn.gelu(u[:, LRU_W:])).astype(y_ref.dtype)

    for s in range(gb):
        if tb == q:
            chunk(s, 0)
        else:
            lax.fori_loop(0, tb // q, lambda i, _: chunk(s, i), None)

    @pl.when(step == nsteps - 1)
    def _():
        cnew_ref[...] = ext_ref[:, pl.ds(q + SUBLANES - (CONV_W - 1), CONV_W - 1), :]
        hnew_ref[...] = h_ref[...]


def rglru_mixer(u, conv_state, h0, init_layer, p, layer, conv_out, h_out, *, gb=1, chunk=256, rows=1024):
    b, l, _ = u.shape
    q = min(chunk, l)
    tb = min(rows, l)
    assert l % tb == 0 and tb % q == 0 and b % gb == 0 and q % SUBLANES == 0
    nsteps = l // tb
    st_in, st_out = _state_specs(gb, init_layer, layer, (CONV_W - 1, LRU_W), (1, LRU_W))
    params = [p['lru_conv_w'], p['lru_conv_b'], p['lru_w_ri'], p['lru_b_ri'], p['lru_lambda']]
    return pl.pallas_call(
        functools.partial(_lru_body, q=q, nsteps=nsteps),
        out_shape=[jax.ShapeDtypeStruct((b, l, LRU_W), _mixer_out_dtype(l)),
                   jax.ShapeDtypeStruct(conv_out.shape, F32), jax.ShapeDtypeStruct(h_out.shape, F32)],
        grid=(b // gb, nsteps),
        in_specs=[pl.BlockSpec((gb, tb, PROJ_LRU), lambda i, c: (i, c, 0)), *st_in,
                  *[_lspec(a, layer) for a in params], _any_spec(), _any_spec()],
        out_specs=[pl.BlockSpec((gb, tb, LRU_W), lambda i, c: (i, c, 0)), *st_out],
        scratch_shapes=[pltpu.VMEM((gb, q + SUBLANES, LRU_W), F32), pltpu.VMEM((gb, 1, LRU_W), F32)],
        input_output_aliases={8: 1, 9: 2},
        compiler_params=_cparams("arbitrary", "arbitrary"),
        name="rglru_mixer",
    )(u, conv_state, h0, *params, conv_out, h_out)


def _hgrn_levels(c):
    ws, w = [], c // 2
    while w >= 1:
        ws.append(w)
        w //= 2
    return ws


def _boundary_rows(g, w):
    c, n = g.shape
    if 2 * w >= SUBLANES:
        gr = g.reshape(c // (2 * w), 2 * w, n)[:, w - 1:w, :]
        return jnp.broadcast_to(gr, (c // (2 * w), 2 * w, n)).reshape(c, n)
    r = lax.broadcasted_iota(jnp.int32, g.shape, 0) % (2 * w)
    out = g
    for delta in range(-(w - 1), w + 1):
        if delta != 0:
            out = jnp.where(r - (w - 1) == delta, pltpu.roll(g, delta % c, 0), out)
    return out


def _hgrn_body(u_ref, st0_ref, lbp_ref, _alias, y_ref, stn_ref, st_ref, *, c, nsteps, layer):
    step = pl.program_id(1)
    gb, tb, _ = u_ref.shape
    nk = HG_HEADS * HG_DK
    hv = lax.broadcasted_iota(jnp.int32, (nk, nk), 0) // HG_DV
    hk = lax.broadcasted_iota(jnp.int32, (nk, nk), 1) // HG_DK
    state_mask = hv == hk

    @pl.when(step == 0)
    def _():
        for s in range(gb):
            t = st0_ref[s].reshape(nk, HG_DV).T
            st_ref[s] = jnp.where(state_mask, jnp.concatenate([t] * HG_HEADS, axis=0), 0.0)

    p = lbp_ref[...]
    e = jnp.exp(p - jnp.max(p, axis=0, keepdims=True))
    lb = jnp.zeros((1, nk), F32)
    for i in range(1, layer + 1):
        lb = lb + e[i:i + 1, :]
    lb = lb / jnp.sum(e, axis=0, keepdims=True)
    log_lb = jnp.log(lb)
    log_1mlb = jnp.log1p(-lb)

    t_idx = lax.broadcasted_iota(jnp.int32, (c, nk), 0)
    row_s = lax.broadcasted_iota(jnp.int32, (HG_HEADS * c, nk), 0)
    lane = lax.broadcasted_iota(jnp.int32, (HG_HEADS * c, nk), 1)
    head_rows = (row_s // c) == (lane // HG_DK)
    t_sc = lax.broadcasted_iota(jnp.int32, (c, HG_HEADS * c), 0)
    s_sc = lax.broadcasted_iota(jnp.int32, (c, HG_HEADS * c), 1) % c

    def tile_heads(x):
        xt = jnp.concatenate([x.astype(BF16)] * HG_HEADS, axis=0)
        return jnp.where(head_rows, xt, jnp.zeros_like(xt))

    def chunk(s, i):
        rows = pl.ds(pl.multiple_of(i * c, c), c)
        u = u_ref[s, rows, :]
        q = _silu(u[:, :nk])
        fz = u[:, nk:2 * nk]
        v = u[:, 2 * nk:3 * nk]
        gate = u[:, 3 * nk:]
        lf = jnp.logaddexp(log_lb, log_1mlb + jax.nn.log_sigmoid(fz))
        k = (1.0 - lb) * jax.nn.sigmoid(-fz)
        g = _cumsum_rows(lf)
        g_end = g[c - 1:c, :]
        st = st_ref[s]
        o = _dot_nt((q * jnp.exp(g)).astype(BF16), st.astype(BF16))
        sc = jnp.where(t_sc == s_sc, _dot_nt(q.astype(BF16), tile_heads(k)), 0.0)
        for w in _hgrn_levels(c):
            gr = _boundary_rows(g, w)
            upper = (t_idx // w) % 2 == 1
            ef = jnp.exp(jnp.where(upper, g - gr, gr - g))
            qt = jnp.where(upper, q * ef, 0.0)
            kt = jnp.where(upper, 0.0, k * ef)
            blk = (t_sc // (2 * w)) == (s_sc // (2 * w))
            sc = sc + jnp.where(blk, _dot_nt(qt.astype(BF16), tile_heads(kt)), 0.0)
        o = o + jnp.dot(sc.astype(BF16), tile_heads(v), preferred_element_type=F32)
        y_ref[s, rows, :] = (o * _silu(gate)).astype(y_ref.dtype)
        upd = _dot_tn(v.astype(BF16), (k * jnp.exp(g_end - g)).astype(BF16))
        st_ref[s] = st * jnp.exp(g_end) + jnp.where(state_mask, upd, 0.0)

    for s in range(gb):
        if tb == c:
            chunk(s, 0)
        else:
            lax.fori_loop(0, tb // c, lambda i, _: chunk(s, i), None, unroll=4)

    @pl.when(step == nsteps - 1)
    def _():
        for s in range(gb):
            st = jnp.where(state_mask, st_ref[s], 0.0)
            t = st[:HG_DV]
            for h in range(1, HG_HEADS):
                t = t + st[h * HG_DV:(h + 1) * HG_DV]
            stn_ref[s] = t.T.reshape(HG_HEADS, HG_DK, HG_DV)


def hgrn2_mixer(u, state, init_layer, lb_param, layer, state_out, *, gb=1, rows=512):
    b, l, _ = u.shape
    c = min(HG_CHUNK, l)
    tb = min(rows, l)
    assert l % tb == 0 and tb % c == 0 and b % gb == 0
    nsteps = l // tb
    nk = HG_HEADS * HG_DK
    st_in, st_out = _state_specs(gb, init_layer, layer, (HG_HEADS, HG_DK, HG_DV))
    return pl.pallas_call(
        functools.partial(_hgrn_body, c=c, nsteps=nsteps, layer=layer),
        out_shape=[jax.ShapeDtypeStruct((b, l, nk), _mixer_out_dtype(l)), jax.ShapeDtypeStruct(state_out.shape, F32)],
        grid=(b // gb, nsteps),
        in_specs=[pl.BlockSpec((gb, tb, PROJ_HG), lambda i, j: (i, j, 0)), *st_in,
                  pl.BlockSpec(lb_param.shape, lambda i, j: (0, 0)), _any_spec()],
        out_specs=[pl.BlockSpec((gb, tb, nk), lambda i, j: (i, j, 0)), *st_out],
        scratch_shapes=[pltpu.VMEM((gb, nk, nk), F32)],
        input_output_aliases={3: 1},
        compiler_params=_cparams("arbitrary", "arbitrary"),
        name="hgrn2_mixer",
    )(u, state, lb_param, state_out)


MLA_BLOCK = LANES
MLA_QK = MLA_HEADS * MLA_BLOCK
MLA_NOPE_OFF = MLA_BLOCK - MLA_NOPE


def _segment_ones():
    seg = lambda i: jnp.where(i < MLA_ROPE, 0, jnp.where(i < MLA_NOPE_OFF, 1, 2))
    i = seg(lax.broadcasted_iota(jnp.int32, (MLA_BLOCK, MLA_BLOCK), 0))
    j = seg(lax.broadcasted_iota(jnp.int32, (MLA_BLOCK, MLA_BLOCK), 1))
    return (i == j).astype(BF16)


def _blocked_rms(x, ones, inv_cnt, gain):
    parts = []
    for h in range(MLA_HEADS):
        xb = x[:, h * MLA_BLOCK:(h + 1) * MLA_BLOCK]
        ss = jnp.dot((xb * xb).astype(BF16), ones, preferred_element_type=F32)
        parts.append(xb * lax.rsqrt(ss * inv_cnt + EPS))
    return jnp.concatenate(parts, axis=1) * gain


def _rope(x, cos, sin_lo, sin_hi):
    n = x.shape[-1]
    half = MLA_ROPE // 2
    return x * cos + pltpu.roll(x, n - half, 1) * sin_lo + pltpu.roll(x, half, 1) * sin_hi


def _paged_body(pt_ref, qn_ref, qr_ref, latn_ref, kr4n_ref, wkt_ref, wv_ref, clat_hbm, ckr_hbm, o_ref,
                lat_buf, kr_buf, latbf_ref, s_ref, sem, *, layer, n_pages, chunk_pages):
    b = pl.program_id(0)
    nb = pl.num_programs(0)
    ng, t_new, _ = qn_ref.shape
    hq = MLA_HEADS * t_new
    nk = MLA_HEADS * MLA_NOPE
    n_tok = chunk_pages * PAGE_SIZE
    issue_unroll = 8

    def page_copies(step, slot, k, p):
        page = pt_ref[step * ng + k, p]
        return (pltpu.make_async_copy(clat_hbm.at[layer, page], lat_buf.at[slot, k, p], sem.at[slot, 0]),
                pltpu.make_async_copy(ckr_hbm.at[layer, page], kr_buf.at[slot, k, p], sem.at[slot, 1]))

    @pl.when(b == 0)
    def _():
        def body(g, _):
            for k in range(ng):
                for i in range(issue_unroll):
                    for cp in page_copies(0, 0, k, g * issue_unroll + i):
                        cp.start()
        lax.fori_loop(0, n_pages // issue_unroll, body, None)

    step_next = lax.rem(b + 1, nb)
    slot_next = lax.rem(b + 1, 2)

    tile_rows = lambda x: jnp.concatenate([x] * MLA_HEADS, axis=0)
    own_n = (lax.broadcasted_iota(jnp.int32, (hq, nk), 0) // t_new
             == lax.broadcasted_iota(jnp.int32, (hq, nk), 1) // MLA_NOPE)
    own_r = (lax.broadcasted_iota(jnp.int32, (hq, LANES), 0) // t_new
             == lax.broadcasted_iota(jnp.int32, (hq, LANES), 1) // MLA_ROPE)
    wkt = wkt_ref[...]

    def query_operands(k):
        qn_sel = jnp.where(own_n, tile_rows(qn_ref[k]), 0.0).astype(BF16)
        q_abs = jnp.dot(qn_sel, wkt, preferred_element_type=F32).astype(BF16)
        qr_sel = jnp.where(own_r, tile_rows(qr_ref[k]), 0.0).astype(BF16)
        return jnp.concatenate([wkt, q_abs], axis=0), qr_sel

    def scores(operands, lat_bf, kr4_bf):
        lhs, qr_sel = operands
        n = lat_bf.shape[0]
        both = _dot_nt(lhs, lat_bf)
        sq = both[:nk] * both[:nk]
        ssq = []
        for h in range(MLA_HEADS):
            part = sq[h * MLA_NOPE:(h + 1) * MLA_NOPE].reshape(MLA_NOPE // SUBLANES, SUBLANES, n).sum(axis=0)
            shift = SUBLANES // 2
            while shift >= 1:
                part = part + pltpu.roll(part, shift, 0)
                shift //= 2
            ssq.append(part)
        ssq = jnp.concatenate(ssq, axis=0)
        s_rope = jnp.dot(qr_sel, kr4_bf, preferred_element_type=F32)
        return (both[nk:] * lax.rsqrt(ssq * (1.0 / MLA_NOPE) + EPS) + s_rope) * MLA_SCALE

    t_q = lax.broadcasted_iota(jnp.int32, (hq, t_new), 0) % t_new
    t_k = lax.broadcasted_iota(jnp.int32, (hq, t_new), 1)
    ops, latn_bf, s_new, m = [], [], [], []
    for k in range(ng):
        ops.append(query_operands(k))
        latn_bf.append(latn_ref[k].astype(BF16))
        s = scores(ops[k], latn_bf[k], kr4n_ref[k].T.astype(BF16))
        s_new.append(jnp.where(t_k <= t_q, s, -jnp.inf))
        m.append(jnp.max(s_new[k], axis=-1, keepdims=True))

    slot = b % 2
    for k in range(ng):
        for p in range(n_pages):
            for cp in page_copies(b, slot, k, p):
                cp.wait()

    for c in range(n_pages // chunk_pages):
        p0 = c * chunk_pages
        for k in range(ng):
            for i in range(chunk_pages):
                for cp in page_copies(step_next, slot_next, k, p0 + i):
                    cp.start()
            lat_bf = lat_buf[slot, k, p0:p0 + chunk_pages].reshape(n_tok, KV_LORA).astype(BF16)
            latbf_ref[k, c * n_tok:(c + 1) * n_tok, :] = lat_bf
            kr_t = jnp.concatenate([kr_buf[slot, k, p0 + i] for i in range(chunk_pages)], axis=1)
            s = scores(ops[k], lat_bf, tile_rows(kr_t.astype(BF16)))
            s_ref[k, :, c * n_tok:(c + 1) * n_tok] = s
            m[k] = jnp.maximum(m[k], jnp.max(s, axis=-1, keepdims=True))

    lane_v = lax.broadcasted_iota(jnp.int32, (t_new, MLA_HEADS * MLA_V), 1) // MLA_V
    for k in range(ng):
        p_old = jnp.exp(s_ref[k] - m[k])
        p_new = jnp.exp(s_new[k] - m[k])
        l = jnp.sum(p_old, axis=-1, keepdims=True) + jnp.sum(p_new, axis=-1, keepdims=True)
        acc = jnp.dot(p_old.astype(BF16), latbf_ref[k], preferred_element_type=F32)
        acc = acc + jnp.dot(p_new.astype(BF16), latn_bf[k], preferred_element_type=F32)
        ov = jnp.dot((acc / l).astype(BF16), wv_ref[...], preferred_element_type=F32)
        out = jnp.zeros((t_new, MLA_HEADS * MLA_V), F32)
        for h in range(MLA_HEADS):
            out = out + jnp.where(lane_v == h, ov[h * t_new:(h + 1) * t_new, :], 0.0)
        o_ref[k] = out

    @pl.when(b == nb - 1)
    def _():
        for k in range(ng):
            for p in range(n_pages):
                for cp in page_copies(step_next, slot_next, k, p):
                    cp.wait()


def mla_sample_attention(qn, qr, lat_all, kr4_new, cache_lat, cache_kr_t, page_table, layer, p, *, chunk_pages=8):
    b, t_new, _ = qn.shape
    n_pages = page_table.shape[1]
    ng = 2 if b % 2 == 0 else 1
    assert n_pages % chunk_pages == 0 and n_pages % 8 == 0
    assert t_new == SUBLANES
    n_past = n_pages * PAGE_SIZE
    seq = lambda n: pl.BlockSpec((ng, t_new, n), lambda i, pt: (i, 0, 0))
    lspec = lambda a: pl.BlockSpec((None,) + a.shape[1:], lambda i, pt: (layer, 0, 0))
    return pl.pallas_call(
        functools.partial(_paged_body, layer=layer, n_pages=n_pages, chunk_pages=chunk_pages),
        out_shape=jax.ShapeDtypeStruct((b, t_new, MLA_HEADS * MLA_V), F32),
        grid_spec=pltpu.PrefetchScalarGridSpec(
            num_scalar_prefetch=1,
            grid=(b // ng,),
            in_specs=[seq(MLA_HEADS * MLA_NOPE), seq(LANES),
                      pl.BlockSpec((None, ng, t_new, KV_LORA), lambda i, pt: (layer, i, 0, 0)), seq(LANES),
                      lspec(p['mla_wk_t']), lspec(p['mla_wv']), _any_spec(), _any_spec()],
            out_specs=seq(MLA_HEADS * MLA_V),
            scratch_shapes=[pltpu.VMEM((2, ng, n_pages, PAGE_SIZE, KV_LORA), F32),
                            pltpu.VMEM((2, ng, n_pages, MLA_ROPE, PAGE_SIZE), F32),
                            pltpu.VMEM((ng, n_past, KV_LORA), BF16),
                            pltpu.VMEM((ng, MLA_HEADS * t_new, n_past), F32),
                            pltpu.SemaphoreType.DMA((2, 2))],
        ),
        compiler_params=_cparams("arbitrary"),
        name="mla_sample_attention",
    )(page_table, qn, qr, lat_all, kr4_new, p['mla_wk_t'], p['mla_wv'], cache_lat, cache_kr_t)


def _mla_proj_sample_body(u_ref, cos_ref, slo_ref, shi_ref, qnorm_ref, wq_ref, gn_ref, gr_ref, kvnorm_ref, gkr_ref,
                          _lat_alias, _kr_alias, qn_ref, qr_ref, kr4_ref, lat_ref, kr_ref):
    gb, r, _ = u_ref.shape
    m = gb * r
    nk = MLA_HEADS * MLA_NOPE
    u = u_ref[...]
    tab = lambda t_ref: jnp.broadcast_to(t_ref[...], (gb, r, LANES)).reshape(m, LANES)
    cos, slo, shi = tab(cos_ref), tab(slo_ref), tab(shi_ref)
    i = lax.broadcasted_iota(jnp.int32, (LANES, LANES), 0) // MLA_ROPE
    j = lax.broadcasted_iota(jnp.int32, (LANES, LANES), 1) // MLA_ROPE
    ones_r = (i == j).astype(BF16)

    cq = _rms(u[..., :Q_LORA], qnorm_ref[...]).reshape(m, Q_LORA)
    q = jnp.dot(cq.astype(BF16), wq_ref[...], preferred_element_type=F32)
    qn, qr = q[:, :nk], q[:, nk:]
    ssn = jnp.dot((qn * qn).astype(BF16), _head_ones(nk), preferred_element_type=F32)
    qn_ref[...] = (qn * lax.rsqrt(ssn * (1.0 / MLA_NOPE) + EPS) * gn_ref[...]).reshape(gb, r, nk)
    ssr = jnp.dot((qr * qr).astype(BF16), ones_r, preferred_element_type=F32)
    qr = qr * lax.rsqrt(ssr * (1.0 / MLA_ROPE) + EPS) * gr_ref[...]
    qr_ref[...] = _rope(qr, cos, slo, shi).reshape(gb, r, LANES)

    lat_ref[...] = _rms(u[..., Q_LORA:Q_LORA + KV_LORA], kvnorm_ref[...])
    kr = u[..., Q_LORA + KV_LORA:].reshape(m, LANES)
    kr = kr * lax.rsqrt(jnp.mean(kr * kr, axis=-1, keepdims=True) + EPS) * gkr_ref[...]
    kr = _rope(kr, cos, slo, shi)
    kr4_ref[...] = kr.reshape(gb, r, LANES)
    kr_ref[...] = kr[:, :MLA_ROPE].reshape(gb, r, MLA_ROPE)


def _mla_proj_body(u_ref, cos_ref, slo_ref, shi_ref, qnorm_ref, wq_ref, gq_ref, kvnorm_ref, gkr_ref,
                   wkv_ref, gk_ref, _lat_alias, _kr_alias, q_ref, k_ref, v_ref, lat_ref, kr_ref):
    gb, r, _ = u_ref.shape
    m = gb * r
    u = u_ref[...]
    tab = lambda t_ref: jnp.broadcast_to(t_ref[...], (gb, r, MLA_BLOCK)).reshape(m, MLA_BLOCK)
    cos, slo, shi = tab(cos_ref), tab(slo_ref), tab(shi_ref)
    tile = lambda t: jnp.concatenate([t] * MLA_HEADS, axis=1)
    ones = _segment_ones()
    lane = lax.broadcasted_iota(jnp.int32, (1, MLA_BLOCK), 1)
    inv_cnt = jnp.where(lane < MLA_NOPE_OFF, 1.0 / MLA_ROPE, 1.0 / MLA_NOPE)

    cq = _rms(u[..., :Q_LORA], qnorm_ref[...]).reshape(m, Q_LORA)
    q = jnp.dot(cq.astype(BF16), wq_ref[...], preferred_element_type=F32)
    q = _blocked_rms(q, ones, inv_cnt, gq_ref[...])
    q = _rope(q, tile(cos), tile(slo), tile(shi))
    q_ref[...] = q.reshape(gb, r, MLA_QK).astype(q_ref.dtype)

    lat = _rms(u[..., Q_LORA:Q_LORA + KV_LORA], kvnorm_ref[...])
    lat_ref[...] = lat
    kr = u[..., Q_LORA + KV_LORA:].reshape(m, MLA_BLOCK)
    kr = kr * lax.rsqrt(jnp.sum(kr * kr, axis=-1, keepdims=True) * (1.0 / MLA_ROPE) + EPS) * gkr_ref[...]
    kr = _rope(kr, cos, slo, shi)
    kr_ref[...] = kr[:, :MLA_ROPE].reshape(gb, r, MLA_ROPE)

    kv = jnp.dot(lat.reshape(m, KV_LORA).astype(BF16), wkv_ref[...], preferred_element_type=F32)
    k = _blocked_rms(kv[:, :MLA_QK], ones, inv_cnt, gk_ref[...]) + tile(kr)
    k_ref[...] = k.reshape(gb, r, MLA_QK).astype(k_ref.dtype)
    lane_v = lax.broadcasted_iota(jnp.int32, (1, MLA_QK), 1) % MLA_BLOCK
    v = kv[:, MLA_QK:] + jnp.where(lane_v == MLA_V, 1.0, 0.0)
    v_ref[...] = v.reshape(gb, r, MLA_QK).astype(v_ref.dtype)


def _rope_tables(pos0, l, copies):
    half = MLA_ROPE // 2
    inv_freq = ROPE_THETA ** (-jnp.arange(half, dtype=F32) / half)
    ang = (pos0 + jnp.arange(l)).astype(F32)[:, None] * inv_freq
    cos, sin = jnp.cos(ang), jnp.sin(ang)
    rest = LANES - copies * MLA_ROPE
    lay = lambda t, fill: jnp.concatenate([jnp.tile(t, (1, copies)), jnp.full((l, rest), fill, F32)], axis=1)
    zero = jnp.zeros_like(sin)
    return (lay(jnp.concatenate([cos, cos], 1), 1.0), lay(jnp.concatenate([-sin, zero], 1), 0.0),
            lay(jnp.concatenate([zero, sin], 1), 0.0))


def _head_blocked(rope_part, nope_part):
    ref = rope_part if rope_part is not None else nope_part
    lead = ref.shape[:-2]
    cols = []
    for h in range(MLA_HEADS):
        rp = rope_part[..., h, :] if rope_part is not None else jnp.zeros(lead + (MLA_ROPE,), F32)
        npart = nope_part[..., h, :] if nope_part is not None else jnp.zeros(lead + (MLA_NOPE,), F32)
        cols += [rp, jnp.zeros(lead + (MLA_NOPE_OFF - MLA_ROPE,), F32), npart]
    return jnp.concatenate(cols, axis=-1)


def _mla_project_call(body, name, u, tables, params, layer, outs, stacks, rows):
    b, l, _ = u.shape
    t = _Tiling(b, l, rows)
    uspec = lambda n: pl.BlockSpec((t.gb, t.r, n), lambda i: (*t.tok_index(i), 0))
    tspec = pl.BlockSpec((t.r, LANES), lambda i: (i % t.nl, 0))
    sspec = lambda a: pl.BlockSpec((None, t.gb, t.r, a.shape[-1]), lambda i: (layer, *t.tok_index(i), 0))
    n_in = 4 + len(params)
    return pl.pallas_call(
        body,
        out_shape=[jax.ShapeDtypeStruct((b, l, n), dt) for n, dt in outs]
                  + [jax.ShapeDtypeStruct(a.shape, a.dtype) for a in stacks],
        grid=(t.steps,),
        in_specs=[uspec(PROJ_MLA), tspec, tspec, tspec] + [_lspec(a, layer) for a in params]
                 + [_any_spec() for _ in stacks],
        out_specs=[uspec(n) for n, _ in outs] + [sspec(a) for a in stacks],
        input_output_aliases={n_in + k: len(outs) + k for k in range(len(stacks))},
        compiler_params=_cparams("arbitrary"),
        name=name,
    )(u, *tables, *params, *stacks)


def mla_project_prompt(u, tables, p, layer, lat_out, kr_out, *, rows=512):
    params = [p['mla_q_norm'], p['mla_wq_blocked'], p['mla_gq_blocked'], p['mla_kv_norm'], p['mla_gkr_pad'],
              p['mla_wkv_blocked'], p['mla_gk_blocked']]
    return _mla_project_call(_mla_proj_body, "mla_project_prompt", u, tables, params, layer,
                             [(MLA_QK, BF16)] * 3, [lat_out, kr_out], rows)


def mla_project_sample(u, tables, p, layer, lat_out, kr_out, *, rows=512):
    params = [p['mla_q_norm'], p['mla_wq_compact'], p['mla_gn_compact'], p['mla_gr_compact'], p['mla_kv_norm'],
              p['mla_gkr_tiled']]
    return _mla_project_call(_mla_proj_sample_body, "mla_project_sample", u, tables, params, layer,
                             [(MLA_HEADS * MLA_NOPE, F32), (LANES, F32), (LANES, F32)], [lat_out, kr_out], rows)


def _flash_body(q_ref, k_ref, v_ref, o_ref, *, tq, nq):
    i = pl.program_id(1)
    row = lax.broadcasted_iota(jnp.int32, (tq, tq), 0)
    col = lax.broadcasted_iota(jnp.int32, (tq, tq), 1)
    c = MLA_SCALE * math.log2(math.e)
    blk = lambda h: pl.ds(h * MLA_BLOCK, MLA_BLOCK)
    qs = [q_ref[:, blk(h)] for h in range(MLA_HEADS)]

    def update(carry, rows, mask):
        new = []
        for h in range(MLA_HEADS):
            m, acc = carry[h]
            s = _dot_nt(qs[h], k_ref[rows, blk(h)])
            if mask:
                s = jnp.where(row >= col, s, -jnp.inf)
            m_new = jnp.maximum(m, jnp.max(s, axis=-1, keepdims=True))
            alpha = jnp.exp2((m - m_new) * c)
            p = jnp.exp2((s - m_new) * c)
            acc = alpha * acc + jnp.dot(p.astype(BF16), v_ref[rows, blk(h)], preferred_element_type=F32)
            new.append((m_new, acc))
        return tuple(new)

    def run(n_past):
        carry = tuple((jnp.full((tq, 1), -jnp.inf, F32), jnp.zeros((tq, MLA_BLOCK), F32)) for _ in range(MLA_HEADS))
        for j in range(n_past):
            carry = update(carry, pl.ds(j * tq, tq), False)
        carry = update(carry, pl.ds(n_past * tq, tq), True)
        out = jnp.concatenate([acc[:, :MLA_V] / acc[:, MLA_V:MLA_V + 1] for _, acc in carry], axis=1)
        o_ref[...] = out.astype(o_ref.dtype)

    for n_past in range(nq):
        pl.when(i == n_past)(functools.partial(run, n_past))


def mla_prompt_attention(q, k, v, *, tq=512):
    b, l, _ = q.shape
    tq = min(tq, l)
    assert l % tq == 0
    return pl.pallas_call(
        functools.partial(_flash_body, tq=tq, nq=l // tq),
        out_shape=jax.ShapeDtypeStruct((b, l, MLA_HEADS * MLA_V), _mixer_out_dtype(l)),
        grid=(b, l // tq),
        in_specs=[pl.BlockSpec((None, tq, MLA_QK), lambda bi, i: (bi, i, 0)),
                  pl.BlockSpec((None, l, MLA_QK), lambda bi, i: (bi, 0, 0)),
                  pl.BlockSpec((None, l, MLA_QK), lambda bi, i: (bi, 0, 0))],
        out_specs=pl.BlockSpec((None, tq, MLA_HEADS * MLA_V), lambda bi, i: (bi, i, 0)),
        compiler_params=_cparams("arbitrary", "arbitrary"),
        name="mla_prompt_attention",
    )(q, k, v)


def prepare_params(raw):
    depth = raw['w_in'].shape[0]
    row = lambda a: a.reshape(depth, 1, -1)
    pad_heads = lambda a: jnp.pad(a, ((0, 0), (0, LANES - SSD_HEADS))).reshape(depth, 1, LANES)
    eye = jnp.eye(LRU_BLOCKS, dtype=F32)
    block_diag = lambda w: jnp.einsum('lhij,hg->lhigj', w, eye).reshape(depth, LRU_W, LRU_W)
    ones_h = jnp.ones((1, MLA_HEADS, 1), F32)
    per_head = lambda g: ones_h * g[:, None, :]
    tile4 = lambda g: jnp.tile(g, (1, MLA_HEADS)).reshape(depth, 1, -1)
    wq = raw['mla_w_uq'].reshape(depth, Q_LORA, MLA_HEADS, MLA_NOPE + MLA_ROPE)
    wkv = raw['mla_w_ukv'].reshape(depth, KV_LORA, MLA_HEADS, MLA_NOPE + MLA_V)
    w_k = wkv[..., :MLA_NOPE].reshape(depth, KV_LORA, MLA_HEADS * MLA_NOPE)
    w_v = wkv[..., MLA_NOPE:]
    w_v_blocked = jnp.pad(w_v, ((0, 0), (0, 0), (0, 0), (0, MLA_BLOCK - MLA_V))).reshape(depth, KV_LORA, MLA_QK)
    return dict(
        norm_ffn1=row(raw['norm_ffn1']), norm_mix=row(raw['norm_mix']), norm_ffn2=row(raw['norm_ffn2']),
        out_norm=row(raw['out_norm']),
        ffn1_in=raw['w_ffn1_in'].astype(BF16), ffn1_out=raw['w_ffn1_out'].astype(BF16),
        ffn2_in=raw['w_ffn2_in'].astype(BF16), ffn2_out=raw['w_ffn2_out'].astype(BF16),
        w_in=raw['w_in'], w_out=raw['w_out'].astype(BF16),
        ssd_conv_w=raw['ssd_conv_w'], ssd_conv_b=row(raw['ssd_conv_b']), ssd_dt_bias=pad_heads(raw['ssd_dt_bias']),
        ssd_a_log=pad_heads(raw['ssd_a_log']), ssd_d=row(jnp.repeat(raw['ssd_d'], SSD_P, axis=1)),
        lru_conv_w=raw['lru_conv_w'], lru_conv_b=row(raw['lru_conv_b']),
        lru_w_ri=jnp.concatenate([block_diag(raw['lru_w_r']), block_diag(raw['lru_w_i'])], axis=2).astype(BF16),
        lru_b_ri=row(jnp.concatenate([raw['lru_b_r'], raw['lru_b_i']], axis=1)), lru_lambda=row(raw['lru_lambda']),
        hgrn_lb=raw['hgrn_lb'],
        mla_q_norm=row(raw['mla_q_norm']), mla_kv_norm=row(raw['mla_kv_norm']),
        mla_wq_blocked=_head_blocked(wq[..., MLA_NOPE:], wq[..., :MLA_NOPE]).astype(BF16),
        mla_gq_blocked=row(_head_blocked(per_head(raw['mla_qn_rope']), per_head(raw['mla_qn_nope']))),
        mla_gkr_pad=row(jnp.pad(raw['mla_kn_rope'], ((0, 0), (0, MLA_BLOCK - MLA_ROPE)))),
        mla_wkv_blocked=jnp.concatenate([_head_blocked(None, wkv[..., :MLA_NOPE]), w_v_blocked], axis=2).astype(BF16),
        mla_gk_blocked=row(_head_blocked(None, per_head(raw['mla_kn_nope']))),
        mla_wq_compact=jnp.concatenate([wq[..., :MLA_NOPE].reshape(depth, Q_LORA, -1),
                                        wq[..., MLA_NOPE:].reshape(depth, Q_LORA, -1)], axis=2).astype(BF16),
        mla_gn_compact=tile4(raw['mla_qn_nope'] * raw['mla_kn_nope']), mla_gr_compact=tile4(raw['mla_qn_rope']),
        mla_gkr_tiled=tile4(raw['mla_kn_rope']),
        mla_wk_t=jnp.swapaxes(w_k, 1, 2).astype(BF16),
        mla_wv=w_v.reshape(depth, KV_LORA, MLA_HEADS * MLA_V).astype(BF16),
    )


def _layer(x, mods, layer, row0, p, init_layer, init, outs, tables, paged, seq_block):
    conv_a, ssm_a, conv_b, h_b, s_c = init
    lat_o, kr_o, ssm_o, conva_o, h_o, convb_o, s_o = outs
    x = ffn_halfstep(x, mods, layer, row0, 0, p['norm_ffn1'], p['ffn1_in'], p['ffn1_out'])
    kr_copies = 1 if paged is None else LANES // MLA_ROPE
    u_a, u_b, u_c, u_d = mixer_projection(x, mods, layer, row0, p['norm_mix'], p['w_in'], kr_copies)
    y_a, conva_o, ssm_o = ssd_mixer(u_a, conv_a, ssm_a, init_layer, p, layer, conva_o, ssm_o, gb=seq_block)
    y_b, convb_o, h_o = rglru_mixer(u_b, conv_b, h_b, init_layer, p, layer, convb_o, h_o, gb=seq_block)
    y_c, s_o = hgrn2_mixer(u_c, s_c, init_layer, p['hgrn_lb'], layer, s_o, gb=seq_block)
    if paged is None:
        q, k, v, lat_o, kr_o = mla_project_prompt(u_d, tables, p, layer, lat_o, kr_o)
        y_d = mla_prompt_attention(q, k, v)
    else:
        cache_lat, cache_kr_t, page_table = paged
        qn, qr, kr4, lat_o, kr_o = mla_project_sample(u_d, tables, p, layer, lat_o, kr_o)
        y_d = mla_sample_attention(qn, qr, lat_o, kr4, cache_lat, cache_kr_t, page_table, layer, p)
    x = output_merge(x, (y_a, y_b, y_c, y_d), mods, layer, row0, p['out_norm'], p['w_out'])
    x = ffn_halfstep(x, mods, layer, row0, 6, p['norm_ffn2'], p['ffn2_in'], p['ffn2_out'])
    return x, (lat_o, kr_o, ssm_o, conva_o, h_o, convb_o, s_o)


def _result_stacks(b, l):
    z = lambda *s: jnp.zeros((DEPTH, b) + s, F32)
    return (z(l, KV_LORA), z(l, MLA_ROPE), z(SSD_HEADS * SSD_P, SSD_N), z(CONV_W - 1, SSD_CONV_DIM),
            z(1, LRU_W), z(CONV_W - 1, LRU_W), z(HG_HEADS, HG_DK, HG_DV))


def kernel(x_prompt, x_sample, cache_mla_latent, cache_mla_krope, state_ssd, state_ssd_conv, state_lru,
           state_lru_conv, state_hgrn, page_table, c_prompt, c_sample, w_ada, b_ada, norm_ffn1, w_ffn1_in,
           w_ffn1_out, norm_mix, w_in, ssd_conv_w, ssd_conv_b, ssd_dt_bias, ssd_a_log, ssd_d, lru_conv_w,
           lru_conv_b, lru_w_r, lru_b_r, lru_w_i, lru_b_i, lru_lambda, hgrn_lb, mla_q_norm, mla_w_uq,
           mla_kv_norm, mla_w_ukv, mla_qn_nope, mla_qn_rope, mla_kn_nope, mla_kn_rope, out_norm, w_out,
           norm_ffn2, w_ffn2_in, w_ffn2_out):
    p = prepare_params(dict(
        norm_ffn1=norm_ffn1, w_ffn1_in=w_ffn1_in, w_ffn1_out=w_ffn1_out, norm_mix=norm_mix, w_in=w_in,
        ssd_conv_w=ssd_conv_w, ssd_conv_b=ssd_conv_b, ssd_dt_bias=ssd_dt_bias, ssd_a_log=ssd_a_log, ssd_d=ssd_d,
        lru_conv_w=lru_conv_w, lru_conv_b=lru_conv_b, lru_w_r=lru_w_r, lru_b_r=lru_b_r, lru_w_i=lru_w_i,
        lru_b_i=lru_b_i, lru_lambda=lru_lambda, hgrn_lb=hgrn_lb, mla_q_norm=mla_q_norm, mla_w_uq=mla_w_uq,
        mla_kv_norm=mla_kv_norm, mla_w_ukv=mla_w_ukv, mla_qn_nope=mla_qn_nope, mla_qn_rope=mla_qn_rope,
        mla_kn_nope=mla_kn_nope, mla_kn_rope=mla_kn_rope, out_norm=out_norm, w_out=w_out, norm_ffn2=norm_ffn2,
        w_ffn2_in=w_ffn2_in, w_ffn2_out=w_ffn2_out))
    bp, lp = x_prompt.shape[:2]
    bs, ls = x_sample.shape[:2]
    dt = x_prompt.dtype
    mods = ada_modulation(jnp.concatenate([c_sample, c_prompt], axis=0), w_ada, b_ada)
    mods = mods.reshape(DEPTH, bs + bp, 1, N_MOD * D_MODEL)
    cache_kr_t = jnp.swapaxes(cache_mla_krope, 2, 3)
    nrow = SSD_HEADS * SSD_P
    zero = lambda *s: jnp.zeros((1, bp) + s, F32)
    init_p = (zero(CONV_W - 1, SSD_CONV_DIM), zero(nrow, SSD_N), zero(CONV_W - 1, LRU_W), zero(1, LRU_W),
              zero(HG_HEADS, HG_DK, HG_DV))
    init_s = (state_ssd_conv, state_ssd.reshape(DEPTH, bs, nrow, SSD_N), state_lru_conv,
              state_lru.reshape(DEPTH, bs, 1, LRU_W), state_hgrn)
    tables_p = _rope_tables(0, lp, 1)
    tables_s = _rope_tables(PAST_LEN, ls, LANES // MLA_ROPE)
    out_p, out_s = _result_stacks(bp, lp), _result_stacks(bs, ls)
    yp, ys = x_prompt, x_sample
    sample_block = math.gcd(bs, SUBLANES)
    for l in range(DEPTH):
        yp, out_p = _layer(yp, mods, l, bs, p, 0, init_p, out_p, tables_p, None, 1)
        ys, out_s = _layer(ys, mods, l, 0, p, l, init_s, out_s, tables_s,
                           (cache_mla_latent, cache_kr_t, page_table), sample_block)

    def finish(outs, b):
        lat, kr, ssm, conva, h, convb, s = outs
        return (lat, kr, ssm.reshape(DEPTH, b, SSD_HEADS, SSD_P, SSD_N), conva, h.reshape(DEPTH, b, LRU_W), convb, s)

    lat_p, kr_p, ssd_p, ssdc_p, lru_p, lruc_p, hg_p = [a.astype(dt) for a in finish(out_p, bp)]
    lat_s, kr_s, ssd_s, ssdc_s, lru_s, lruc_s, hg_s = [a.astype(dt) for a in finish(out_s, bs)]
    return (yp, ys, lat_p, lat_s, kr_p, kr_s, ssd_p, ssd_s, ssdc_p, ssdc_s, lru_p, lru_s, lruc_p, lruc_s, hg_p, hg_s)
```

```python
import functools
import math

import jax
import jax.numpy as jnp
from jax import lax
from jax.experimental import pallas as pl
from jax.experimental.pallas import tpu as pltpu

F32 = jnp.float32
BF16 = jnp.bfloat16

D_MODEL = 1024
DEPTH = 4
PAST_LEN = 8192
PAGE_SIZE = 128
W_GROUP = 256
HEAD_DIM = 64
SSD_HEADS = 4
SSD_P = 64
SSD_N = 128
SSD_GROUPS = 2
SSD_CHUNK = 128
CONV_W = 4
SSD_CONV_DIM = W_GROUP + 2 * SSD_GROUPS * SSD_N
LRU_W = W_GROUP
LRU_BLOCKS = 4
LRU_BW = LRU_W // LRU_BLOCKS
LRU_C = 8.0
HG_HEADS = 4
HG_DK = 64
HG_DV = 64
HG_CHUNK = 64
MLA_HEADS = 4
MLA_NOPE = 64
MLA_ROPE = 32
MLA_V = 64
Q_LORA = 256
KV_LORA = 128
ROPE_THETA = 10000.0
MLA_SCALE = 1.0 / math.sqrt(MLA_NOPE + MLA_ROPE)
D_FF = 2816
N_MOD = 9
EPS = 1e-6
IN_SSD = W_GROUP + SSD_CONV_DIM + SSD_HEADS
IN_LRU = 2 * LRU_W
IN_HG = 4 * HG_HEADS * HG_DK
IN_MLA = Q_LORA + KV_LORA + MLA_ROPE
OFF_LRU = IN_SSD
OFF_HG = OFF_LRU + IN_LRU
OFF_MLA = OFF_HG + IN_HG
N_IN = OFF_MLA + IN_MLA

LANES = 128
SUBLANES = 8
VMEM_BYTES_V7X = 64 * 1024 * 1024
VMEM_LIMIT = VMEM_BYTES_V7X * 3 // 4

PROJ_SSD = W_GROUP + SSD_CONV_DIM + LANES
PROJ_LRU = IN_LRU
PROJ_HG = IN_HG
PROJ_MLA = Q_LORA + KV_LORA + LANES
PROJ_ALL = PROJ_SSD + PROJ_LRU + PROJ_HG + PROJ_MLA


def _cparams(*sem):
    return pltpu.CompilerParams(dimension_semantics=sem, vmem_limit_bytes=VMEM_LIMIT)


def _silu(x):
    return x * jax.nn.sigmoid(x)


def _rms(x, w):
    return x * lax.rsqrt(jnp.mean(x * x, axis=-1, keepdims=True) + EPS) * w


def _lspec(arr, layer):
    nd = arr.ndim - 1
    return pl.BlockSpec((None,) + arr.shape[1:], lambda *_: (layer,) + (0,) * nd)


def _any_spec():
    return pl.BlockSpec(memory_space=pl.ANY)


def _mixer_out_dtype(l):
    return BF16 if l % (2 * SUBLANES) == 0 else F32


def _ada_body(c_ref, w_ref, b_ref, o_ref):
    a = _silu(c_ref[...]).astype(BF16)
    o_ref[...] = jnp.dot(a, w_ref[...].astype(BF16), preferred_element_type=F32) + b_ref[...]


def ada_modulation(c_all, w_ada, b_ada, *, tn=1152):
    r, d = c_all.shape
    depth, _, n = w_ada.shape
    return pl.pallas_call(
        _ada_body,
        out_shape=jax.ShapeDtypeStruct((depth, r, n), F32),
        grid=(depth, n // tn),
        in_specs=[
            pl.BlockSpec((r, d), lambda l, j: (0, 0)),
            pl.BlockSpec((None, d, tn), lambda l, j: (l, 0, j)),
            pl.BlockSpec((None, 1, tn), lambda l, j: (l, 0, j)),
        ],
        out_specs=pl.BlockSpec((None, r, tn), lambda l, j: (l, 0, j)),
        compiler_params=_cparams("arbitrary", "arbitrary"),
        name="ada_modulation",
    )(c_all, w_ada, b_ada.reshape(depth, 1, n))


class _Tiling:
    def __init__(self, b, l, rows):
        if l >= rows:
            assert l % rows == 0
            self.gb, self.r = 1, rows
        else:
            gb = min(b, max(1, rows // l))
            assert b % gb == 0
            self.gb, self.r = gb, l
        self.b, self.l = b, l
        self.nb = b // self.gb
        self.nl = l // self.r
        self.steps = self.nb * self.nl
        self.m = self.gb * self.r

    def tok_index(self, i):
        return (i // self.nl, i % self.nl)


def _mod_spec(t, layer, row0, k):
    assert row0 % t.gb == 0
    return pl.BlockSpec((None, t.gb, 1, D_MODEL), lambda i, *_: (layer, row0 // t.gb + i // t.nl, 0, k))


def _ffn_body(x_ref, sh_ref, sc_ref, gt_ref, nw_ref, w_in_ref, w_out_ref, o_ref):
    gb, r, d = x_ref.shape
    x = x_ref[...]
    h = _rms(x, nw_ref[...]) * (1.0 + sc_ref[...]) + sh_ref[...]
    gu = jnp.dot(h.reshape(gb * r, d).astype(BF16), w_in_ref[...], preferred_element_type=F32)
    a = _silu(gu[:, :D_FF]) * gu[:, D_FF:]
    y = jnp.dot(a.astype(BF16), w_out_ref[...], preferred_element_type=F32)
    o_ref[...] = x + 0.5 * gt_ref[...] * y.reshape(gb, r, d)


def ffn_halfstep(x, mods, layer, row0, k0, norm_w, w_in, w_out, *, rows=512):
    b, l, d = x.shape
    t = _Tiling(b, l, rows)
    xspec = pl.BlockSpec((t.gb, t.r, d), lambda i: (*t.tok_index(i), 0))
    return pl.pallas_call(
        _ffn_body,
        out_shape=jax.ShapeDtypeStruct(x.shape, F32),
        grid=(t.steps,),
        in_specs=[xspec, _mod_spec(t, layer, row0, k0), _mod_spec(t, layer, row0, k0 + 1),
                  _mod_spec(t, layer, row0, k0 + 2), _lspec(norm_w, layer), _lspec(w_in, layer), _lspec(w_out, layer)],
        out_specs=xspec,
        compiler_params=_cparams("arbitrary"),
        name="ffn_halfstep",
    )(x, mods, mods, mods, norm_w, w_in, w_out)


def _pack_w_in(w_ref, wp_ref, kr_copies):
    d = w_ref.shape[0]
    rows = 256
    off_kr = OFF_MLA + Q_LORA + KV_LORA
    dst_mla = PROJ_SSD + PROJ_LRU + PROJ_HG

    def chunk(c, _):
        r = pl.ds(pl.multiple_of(c * rows, rows), rows)
        put = lambda dst, src, n: wp_ref.__setitem__((r, pl.ds(dst, n)), w_ref[r, pl.ds(src, n)].astype(BF16))
        zero = lambda dst, n: wp_ref.__setitem__((r, pl.ds(dst, n)), jnp.zeros((rows, n), BF16))
        put(0, 0, W_GROUP + SSD_CONV_DIM)
        zero(W_GROUP + SSD_CONV_DIM, LANES)
        put(W_GROUP + SSD_CONV_DIM, W_GROUP + SSD_CONV_DIM, SSD_HEADS)
        put(PROJ_SSD, OFF_LRU, IN_LRU)
        put(PROJ_SSD + PROJ_LRU, OFF_HG, IN_HG)
        put(dst_mla, OFF_MLA, Q_LORA + KV_LORA)
        zero(dst_mla + Q_LORA + KV_LORA, LANES)
        for k in range(kr_copies):
            put(dst_mla + Q_LORA + KV_LORA + k * MLA_ROPE, off_kr, MLA_ROPE)

    lax.fori_loop(0, d // rows, chunk, None)


def _proj_body(x_ref, sh_ref, sc_ref, nw_ref, w_ref, ssd_ref, lru_ref, hg_ref, mla_ref, wp_ref, *, kr_copies):
    @pl.when(pl.program_id(0) == 0)
    def _():
        _pack_w_in(w_ref, wp_ref, kr_copies)

    gb, r, d = x_ref.shape
    h = _rms(x_ref[...], nw_ref[...]) * (1.0 + sc_ref[...]) + sh_ref[...]
    h = h.reshape(gb * r, d).astype(BF16)
    off = 0
    for o_ref in (ssd_ref, lru_ref, hg_ref, mla_ref):
        n = o_ref.shape[-1]
        u = jnp.dot(h, wp_ref[:, off:off + n], preferred_element_type=F32)
        o_ref[...] = u.reshape(gb, r, n)
        off += n


def mixer_projection(x, mods, layer, row0, norm_w, w_in, kr_copies, *, rows=512):
    b, l, d = x.shape
    t = _Tiling(b, l, rows)
    widths = (PROJ_SSD, PROJ_LRU, PROJ_HG, PROJ_MLA)
    xspec = pl.BlockSpec((t.gb, t.r, d), lambda i: (*t.tok_index(i), 0))
    w_spec = pl.BlockSpec((None, d, N_IN), lambda i: (layer, 0, 0), pipeline_mode=pl.Buffered(1))
    return pl.pallas_call(
        functools.partial(_proj_body, kr_copies=kr_copies),
        out_shape=[jax.ShapeDtypeStruct((b, l, n), F32) for n in widths],
        grid=(t.steps,),
        in_specs=[xspec, _mod_spec(t, layer, row0, 3), _mod_spec(t, layer, row0, 4), _lspec(norm_w, layer), w_spec],
        out_specs=[pl.BlockSpec((t.gb, t.r, n), lambda i: (*t.tok_index(i), 0)) for n in widths],
        scratch_shapes=[pltpu.VMEM((d, PROJ_ALL), BF16)],
        compiler_params=_cparams("arbitrary"),
        name="mixer_projection",
    )(x, mods, mods, norm_w, w_in)


def _head_ones(n):
    i = lax.broadcasted_iota(jnp.int32, (n, n), 0) // HEAD_DIM
    j = lax.broadcasted_iota(jnp.int32, (n, n), 1) // HEAD_DIM
    return (i == j).astype(BF16)


def _merge_ffn_body(x_ref, ya_ref, yb_ref, yc_ref, yd_ref, g2_ref, sh_ref, sc_ref, g3_ref, onw_ref, wo_ref,
                    nw_ref, w_in_ref, w_out_ref, o_ref):
    gb, r, d = x_ref.shape
    ones = _head_ones(W_GROUP)
    acc = jnp.zeros((gb * r, d), F32)
    for k, y_ref in enumerate((ya_ref, yb_ref, yc_ref, yd_ref)):
        y = y_ref[...].astype(F32).reshape(gb * r, W_GROUP)
        ss = jnp.dot((y * y).astype(BF16), ones, preferred_element_type=F32)
        yn = y * lax.rsqrt(ss * (1.0 / HEAD_DIM) + EPS) * onw_ref[:, k * W_GROUP:(k + 1) * W_GROUP]
        acc += jnp.dot(yn.astype(BF16), wo_ref[k * W_GROUP:(k + 1) * W_GROUP, :], preferred_element_type=F32)
    x1 = x_ref[...] + g2_ref[...] * acc.reshape(gb, r, d)
    h = _rms(x1, nw_ref[...]) * (1.0 + sc_ref[...]) + sh_ref[...]
    gu = jnp.dot(h.reshape(gb * r, d).astype(BF16), w_in_ref[...], preferred_element_type=F32)
    a = _silu(gu[:, :D_FF]) * gu[:, D_FF:]
    y2 = jnp.dot(a.astype(BF16), w_out_ref[...], preferred_element_type=F32)
    o_ref[...] = x1 + 0.5 * g3_ref[...] * y2.reshape(gb, r, d)


def merge_ffn_halfstep(x, ys, mods, layer, row0, out_norm, w_out, norm_w, ffn_in, ffn_out, *, rows=512):
    b, l, d = x.shape
    t = _Tiling(b, l, rows)
    xspec = pl.BlockSpec((t.gb, t.r, d), lambda i: (*t.tok_index(i), 0))
    yspec = pl.BlockSpec((t.gb, t.r, W_GROUP), lambda i: (*t.tok_index(i), 0))
    return pl.pallas_call(
        _merge_ffn_body,
        out_shape=jax.ShapeDtypeStruct(x.shape, F32),
        grid=(t.steps,),
        in_specs=[xspec, yspec, yspec, yspec, yspec,
                  *[_mod_spec(t, layer, row0, k) for k in (5, 6, 7, 8)],
                  _lspec(out_norm, layer), _lspec(w_out, layer), _lspec(norm_w, layer),
                  _lspec(ffn_in, layer), _lspec(ffn_out, layer)],
        out_specs=xspec,
        compiler_params=_cparams("arbitrary"),
        name="merge_ffn_halfstep",
    )(x, *ys, mods, mods, mods, mods, out_norm, w_out, norm_w, ffn_in, ffn_out)


def _cumsum_rows(x):
    n = x.shape[0]
    row = lax.broadcasted_iota(jnp.int32, x.shape, 0)
    d = 1
    while d < n:
        x = x + jnp.where(row >= d, pltpu.roll(x, d, 0), 0.0)
        d *= 2
    return x


def _dot_nt(a, b):
    return lax.dot_general(a, b, (((1,), (1,)), ((), ())), preferred_element_type=F32)


def _dot_tn(a, b):
    return lax.dot_general(a, b, (((0,), (0,)), ((), ())), preferred_element_type=F32)


def _conv_init(ext_ref, cs_ref, q):
    ext_ref[:, pl.ds(q, SUBLANES), :] = jnp.zeros((ext_ref.shape[0], SUBLANES, ext_ref.shape[2]), F32)
    ext_ref[:, pl.ds(q + SUBLANES - (CONV_W - 1), CONV_W - 1), :] = cs_ref[...]


def _causal_conv(ext_ref, s, x, cw_ref, cb_ref, q):
    ext_ref[s, pl.ds(0, SUBLANES), :] = ext_ref[s, pl.ds(q, SUBLANES), :]
    ext_ref[s, pl.ds(SUBLANES, q), :] = x
    out = cb_ref[...]
    for k in range(CONV_W):
        out = out + cw_ref[k:k + 1, :] * ext_ref[s, pl.ds(SUBLANES - (CONV_W - 1) + k, q), :]
    return out


def _state_specs(gb, init_layer, layer, *shapes):
    ins = [pl.BlockSpec((None, gb) + s, lambda i, c, n=len(s): (init_layer, i) + (0,) * n) for s in shapes]
    outs = [pl.BlockSpec((None, gb) + s, lambda i, c, n=len(s): (layer, i) + (0,) * n) for s in shapes]
    return ins, outs


def _ssd_body(u_ref, cs_ref, st_ref, cw_ref, cb_ref, dtb_ref, alog_ref, dsk_ref, _c_alias, _s_alias,
              y_ref, cnew_ref, snew_ref, ext_ref, h_ref, *, q, nsteps):
    step = pl.program_id(1)
    gb, tb, _ = u_ref.shape
    hp = SSD_HEADS // SSD_GROUPS * SSD_P

    @pl.when(step == 0)
    def _():
        _conv_init(ext_ref, cs_ref, q)
        h_ref[...] = st_ref[...]

    row = lax.broadcasted_iota(jnp.int32, (q, q), 0)
    col = lax.broadcasted_iota(jnp.int32, (q, q), 1)
    causal = row >= col
    a = -jnp.exp(alog_ref[...])

    def chunk(s, i):
        tok = pl.ds(pl.multiple_of(i * q, q), q)
        u = u_ref[s, tok, :]
        z = u[:, :W_GROUP]
        conv = _causal_conv(ext_ref, s, u[:, W_GROUP:W_GROUP + SSD_CONV_DIM], cw_ref, cb_ref, q)
        xbc = _silu(conv)
        xs = xbc[:, :W_GROUP]
        bm = xbc[:, W_GROUP:W_GROUP + SSD_GROUPS * SSD_N].astype(BF16)
        cm = xbc[:, W_GROUP + SSD_GROUPS * SSD_N:].astype(BF16)
        dt = jax.nn.softplus(u[:, W_GROUP + SSD_CONV_DIM:] + dtb_ref[...])
        acum = _cumsum_rows(dt * a)
        acum_t = acum.T
        a_end = acum[q - 1:q, :]
        ys = []
        for g in range(SSD_GROUPS):
            bg = bm[:, g * SSD_N:(g + 1) * SSD_N]
            cg = cm[:, g * SSD_N:(g + 1) * SSD_N]
            cb = _dot_nt(cg, bg)
            h_prev = h_ref[s, g * hp:(g + 1) * hp, :]
            y_off = _dot_nt(cg, h_prev.astype(BF16))
            for j in range(SSD_HEADS // SSD_GROUPS):
                h = g * (SSD_HEADS // SSD_GROUPS) + j
                dt_h = dt[:, h:h + 1]
                ac_h = acum[:, h:h + 1]
                ae_h = a_end[:, h:h + 1]
                decay = jnp.exp(jnp.where(causal, ac_h - acum_t[h:h + 1, :], -jnp.inf))
                xs_h = xs[:, h * SSD_P:(h + 1) * SSD_P]
                xdt_h = xs_h * dt_h
                y_h = jnp.dot((cb * decay).astype(BF16), xdt_h.astype(BF16), preferred_element_type=F32)
                y_h = y_h + y_off[:, j * SSD_P:(j + 1) * SSD_P] * jnp.exp(ac_h)
                y_h = y_h + dsk_ref[:, h * SSD_P:(h + 1) * SSD_P] * xs_h
                ys.append(y_h)
                w_h = (xdt_h * jnp.exp(ae_h - ac_h)).astype(BF16)
                st_h = _dot_tn(w_h, bg)
                rows = pl.ds(h * SSD_P, SSD_P)
                h_ref[s, rows, :] = jnp.exp(ae_h) * h_ref[s, rows, :] + st_h
        y_ref[s, tok, :] = (jnp.concatenate(ys, axis=1) * _silu(z)).astype(y_ref.dtype)

    for s in range(gb):
        if tb == q:
            chunk(s, 0)
        else:
            lax.fori_loop(0, tb // q, lambda i, _: chunk(s, i), None, unroll=2)

    @pl.when(step == nsteps - 1)
    def _():
        cnew_ref[...] = ext_ref[:, pl.ds(q + SUBLANES - (CONV_W - 1), CONV_W - 1), :]
        snew_ref[...] = h_ref[...]


def ssd_mixer(u, conv_state, ssm_state, init_layer, p, layer, conv_out, ssm_out, *, gb=1, rows=512):
    b, l, _ = u.shape
    q = min(SSD_CHUNK, l)
    tb = min(rows, l)
    assert l % tb == 0 and tb % q == 0 and b % gb == 0 and q % SUBLANES == 0
    nsteps = l // tb
    nrow = SSD_HEADS * SSD_P
    st_in, st_out = _state_specs(gb, init_layer, layer, (CONV_W - 1, SSD_CONV_DIM), (nrow, SSD_N))
    params = [p['ssd_conv_w'], p['ssd_conv_b'], p['ssd_dt_bias'], p['ssd_a_log'], p['ssd_d']]
    return pl.pallas_call(
        functools.partial(_ssd_body, q=q, nsteps=nsteps),
        out_shape=[jax.ShapeDtypeStruct((b, l, W_GROUP), _mixer_out_dtype(l)),
                   jax.ShapeDtypeStruct(conv_out.shape, F32), jax.ShapeDtypeStruct(ssm_out.shape, F32)],
        grid=(b // gb, nsteps),
        in_specs=[pl.BlockSpec((gb, tb, PROJ_SSD), lambda i, c: (i, c, 0)), *st_in,
                  *[_lspec(a, layer) for a in params], _any_spec(), _any_spec()],
        out_specs=[pl.BlockSpec((gb, tb, W_GROUP), lambda i, c: (i, c, 0)), *st_out],
        scratch_shapes=[pltpu.VMEM((gb, q + SUBLANES, SSD_CONV_DIM), F32), pltpu.VMEM((gb, nrow, SSD_N), F32)],
        input_output_aliases={8: 1, 9: 2},
        compiler_params=_cparams("arbitrary", "arbitrary"),
        name="ssd_mixer",
    )(u, conv_state, ssm_state, *params, conv_out, ssm_out)


def _linear_scan_rows(a, b, h_prev):
    n, w = a.shape
    seg = min(n, SUBLANES)
    pos = lax.broadcasted_iota(jnp.int32, a.shape, 0) % seg
    d = 1
    while d < seg:
        m = pos >= d
        b = jnp.where(m, a * pltpu.roll(b, d, 0) + b, b)
        a = jnp.where(m, a * pltpu.roll(a, d, 0), a)
        d *= 2
    hs, carry = [], h_prev
    for k in range(n // seg):
        h_k = b[k * seg:(k + 1) * seg] + a[k * seg:(k + 1) * seg] * carry
        hs.append(h_k)
        carry = h_k[seg - 1:seg, :]
    return jnp.concatenate(hs, axis=0)


def _lru_body(u_ref, cs_ref, h0_ref, cw_ref, cb_ref, wri_ref, bri_ref, lam_ref, _c_alias, _h_alias,
              y_ref, cnew_ref, hnew_ref, ext_ref, h_ref, *, q, nsteps):
    step = pl.program_id(1)
    gb, tb, _ = u_ref.shape

    @pl.when(step == 0)
    def _():
        _conv_init(ext_ref, cs_ref, q)
        h_ref[...] = h0_ref[...]

    sp = jax.nn.softplus(-lam_ref[...])

    def chunk(s, i):
        tok = pl.ds(pl.multiple_of(i * q, q), q)
        u = u_ref[s, tok, :]
        xb = _causal_conv(ext_ref, s, u[:, :LRU_W], cw_ref, cb_ref, q)
        ri = jnp.dot(xb.astype(BF16), wri_ref[...], preferred_element_type=F32) + bri_ref[...]
        r = jax.nn.sigmoid(ri[:, :LRU_W])
        i = jax.nn.sigmoid(ri[:, LRU_W:])
        log_a = -LRU_C * r * sp
        a = jnp.exp(log_a)
        bterm = jnp.sqrt(-jnp.tanh(log_a) * (a * a + 1.0)) * (i * xb)
        h = _linear_scan_rows(a, bterm, h_ref[s])
        h_ref[s] = h[q - 1:q, :]
        y_ref[s, tok, :] = (h * jax.nn.gelu(u[:, LRU_W:])).astype(y_ref.dtype)

    for s in range(gb):
        if tb == q:
            chunk(s, 0)
        else:
            lax.fori_loop(0, tb // q, lambda i, _: chunk(s, i), None)

    @pl.when(step == nsteps - 1)
    def _():
        cnew_ref[...] = ext_ref[:, pl.ds(q + SUBLANES - (CONV_W - 1), CONV_W - 1), :]
        hnew_ref[...] = h_ref[...]


def rglru_mixer(u, conv_state, h0, init_layer, p, layer, conv_out, h_out, *, gb=1, chunk=256, rows=1024):
    b, l, _ = u.shape
    q = min(chunk, l)
    tb = min(rows, l)
    assert l % tb == 0 and tb % q == 0 and b % gb == 0 and q % SUBLANES == 0
    nsteps = l // tb
    st_in, st_out = _state_specs(gb, init_layer, layer, (CONV_W - 1, LRU_W), (1, LRU_W))
    params = [p['lru_conv_w'], p['lru_conv_b'], p['lru_w_ri'], p['lru_b_ri'], p['lru_lambda']]
    return pl.pallas_call(
        functools.partial(_lru_body, q=q, nsteps=nsteps),
        out_shape=[jax.ShapeDtypeStruct((b, l, LRU_W), _mixer_out_dtype(l)),
                   jax.ShapeDtypeStruct(conv_out.shape, F32), jax.ShapeDtypeStruct(h_out.shape, F32)],
        grid=(b // gb, nsteps),
        in_specs=[pl.BlockSpec((gb, tb, PROJ_LRU), lambda i, c: (i, c, 0)), *st_in,
                  *[_lspec(a, layer) for a in params], _any_spec(), _any_spec()],
        out_specs=[pl.BlockSpec((gb, tb, LRU_W), lambda i, c: (i, c, 0)), *st_out],
        scratch_shapes=[pltpu.VMEM((gb, q + SUBLANES, LRU_W), F32), pltpu.VMEM((gb, 1, LRU_W), F32)],
        input_output_aliases={8: 1, 9: 2},
        compiler_params=_cparams("arbitrary", "arbitrary"),
        name="rglru_mixer",
    )(u, conv_state, h0, *params, conv_out, h_out)


def _hgrn_levels(c):
    ws, w = [], c // 2
    while w >= 1:
        ws.append(w)
        w //= 2
    return ws


def _boundary_rows(g, w):
    c, n = g.shape
    if 2 * w >= SUBLANES:
        gr = g.reshape(c // (2 * w), 2 * w, n)[:, w - 1:w, :]
        return jnp.broadcast_to(gr, (c // (2 * w), 2 * w, n)).reshape(c, n)
    r = lax.broadcasted_iota(jnp.int32, g.shape, 0) % (2 * w)
    out = g
    for delta in range(-(w - 1), w + 1):
        if delta != 0:
            out = jnp.where(r - (w - 1) == delta, pltpu.roll(g, delta % c, 0), out)
    return out


def _hgrn_body(u_ref, st0_ref, lbp_ref, _alias, y_ref, stn_ref, st_ref, *, c, nsteps, layer):
    step = pl.program_id(1)
    gb, tb, _ = u_ref.shape
    nk = HG_HEADS * HG_DK
    hv = lax.broadcasted_iota(jnp.int32, (nk, nk), 0) // HG_DV
    hk = lax.broadcasted_iota(jnp.int32, (nk, nk), 1) // HG_DK
    state_mask = hv == hk

    @pl.when(step == 0)
    def _():
        for s in range(gb):
            t = st0_ref[s].reshape(nk, HG_DV).T
            st_ref[s] = jnp.where(state_mask, jnp.concatenate([t] * HG_HEADS, axis=0), 0.0)

    p = lbp_ref[...]
    e = jnp.exp(p - jnp.max(p, axis=0, keepdims=True))
    lb = jnp.zeros((1, nk), F32)
    for i in range(1, layer + 1):
        lb = lb + e[i:i + 1, :]
    lb = lb / jnp.sum(e, axis=0, keepdims=True)
    log_lb = jnp.log(lb)
    log_1mlb = jnp.log1p(-lb)

    t_idx = lax.broadcasted_iota(jnp.int32, (c, nk), 0)
    row_s = lax.broadcasted_iota(jnp.int32, (HG_HEADS * c, nk), 0)
    lane = lax.broadcasted_iota(jnp.int32, (HG_HEADS * c, nk), 1)
    head_rows = (row_s // c) == (lane // HG_DK)
    t_sc = lax.broadcasted_iota(jnp.int32, (c, HG_HEADS * c), 0)
    s_sc = lax.broadcasted_iota(jnp.int32, (c, HG_HEADS * c), 1) % c

    def tile_heads(x):
        xt = jnp.concatenate([x.astype(BF16)] * HG_HEADS, axis=0)
        return jnp.where(head_rows, xt, jnp.zeros_like(xt))

    def chunk(s, i):
        rows = pl.ds(pl.multiple_of(i * c, c), c)
        u = u_ref[s, rows, :]
        q = _silu(u[:, :nk])
        fz = u[:, nk:2 * nk]
        v = u[:, 2 * nk:3 * nk]
        gate = u[:, 3 * nk:]
        lf = jnp.logaddexp(log_lb, log_1mlb + jax.nn.log_sigmoid(fz))
        k = (1.0 - lb) * jax.nn.sigmoid(-fz)
        g = _cumsum_rows(lf)
        g_end = g[c - 1:c, :]
        st = st_ref[s]
        o = _dot_nt((q * jnp.exp(g)).astype(BF16), st.astype(BF16))
        sc = jnp.where(t_sc == s_sc, _dot_nt(q.astype(BF16), tile_heads(k)), 0.0)
        for w in _hgrn_levels(c):
            gr = _boundary_rows(g, w)
            upper = (t_idx // w) % 2 == 1
            ef = jnp.exp(jnp.where(upper, g - gr, gr - g))
            qt = jnp.where(upper, q * ef, 0.0)
            kt = jnp.where(upper, 0.0, k * ef)
            blk = (t_sc // (2 * w)) == (s_sc // (2 * w))
            sc = sc + jnp.where(blk, _dot_nt(qt.astype(BF16), tile_heads(kt)), 0.0)
        o = o + jnp.dot(sc.astype(BF16), tile_heads(v), preferred_element_type=F32)
        y_ref[s, rows, :] = (o * _silu(gate)).astype(y_ref.dtype)
        upd = _dot_tn(v.astype(BF16), (k * jnp.exp(g_end - g)).astype(BF16))
        st_ref[s] = st * jnp.exp(g_end) + jnp.where(state_mask, upd, 0.0)

    for s in range(gb):
        if tb == c:
            chunk(s, 0)
        else:
            lax.fori_loop(0, tb // c, lambda i, _: chunk(s, i), None, unroll=4)

    @pl.when(step == nsteps - 1)
    def _():
        for s in range(gb):
            st = jnp.where(state_mask, st_ref[s], 0.0)
            t = st[:HG_DV]
            for h in range(1, HG_HEADS):
                t = t + st[h * HG_DV:(h + 1) * HG_DV]
            stn_ref[s] = t.T.reshape(HG_HEADS, HG_DK, HG_DV)


def hgrn2_mixer(u, state, init_layer, lb_param, layer, state_out, *, gb=1, rows=512):
    b, l, _ = u.shape
    c = min(HG_CHUNK, l)
    tb = min(rows, l)
    assert l % tb == 0 and tb % c == 0 and b % gb == 0
    nsteps = l // tb
    nk = HG_HEADS * HG_DK
    st_in, st_out = _state_specs(gb, init_layer, layer, (HG_HEADS, HG_DK, HG_DV))
    return pl.pallas_call(
        functools.partial(_hgrn_body, c=c, nsteps=nsteps, layer=layer),
        out_shape=[jax.ShapeDtypeStruct((b, l, nk), _mixer_out_dtype(l)), jax.ShapeDtypeStruct(state_out.shape, F32)],
        grid=(b // gb, nsteps),
        in_specs=[pl.BlockSpec((gb, tb, PROJ_HG), lambda i, j: (i, j, 0)), *st_in,
                  pl.BlockSpec(lb_param.shape, lambda i, j: (0, 0)), _any_spec()],
        out_specs=[pl.BlockSpec((gb, tb, nk), lambda i, j: (i, j, 0)), *st_out],
        scratch_shapes=[pltpu.VMEM((gb, nk, nk), F32)],
        input_output_aliases={3: 1},
        compiler_params=_cparams("arbitrary", "arbitrary"),
        name="hgrn2_mixer",
    )(u, state, lb_param, state_out)


MLA_BLOCK = LANES
MLA_QK = MLA_HEADS * MLA_BLOCK
MLA_NOPE_OFF = MLA_BLOCK - MLA_NOPE


def _segment_ones():
    seg = lambda i: jnp.where(i < MLA_ROPE, 0, jnp.where(i < MLA_NOPE_OFF, 1, 2))
    i = seg(lax.broadcasted_iota(jnp.int32, (MLA_BLOCK, MLA_BLOCK), 0))
    j = seg(lax.broadcasted_iota(jnp.int32, (MLA_BLOCK, MLA_BLOCK), 1))
    return (i == j).astype(BF16)


def _blocked_rms(x, ones, inv_cnt, gain):
    parts = []
    for h in range(MLA_HEADS):
        xb = x[:, h * MLA_BLOCK:(h + 1) * MLA_BLOCK]
        ss = jnp.dot((xb * xb).astype(BF16), ones, preferred_element_type=F32)
        parts.append(xb * lax.rsqrt(ss * inv_cnt + EPS))
    return jnp.concatenate(parts, axis=1) * gain


def _rope(x, cos, sin_lo, sin_hi):
    n = x.shape[-1]
    half = MLA_ROPE // 2
    return x * cos + pltpu.roll(x, n - half, 1) * sin_lo + pltpu.roll(x, half, 1) * sin_hi


def _paged_body(pt_ref, qn_ref, qr_ref, latn_ref, kr4n_ref, wkt_ref, wv_ref, clat_hbm, ckr_hbm, o_ref,
                lat_buf, kr_buf, latbf_ref, s_ref, sem, *, layer, n_pages, chunk_pages):
    b = pl.program_id(0)
    nb = pl.num_programs(0)
    ng, t_new, _ = qn_ref.shape
    hq = MLA_HEADS * t_new
    nk = MLA_HEADS * MLA_NOPE
    n_tok = chunk_pages * PAGE_SIZE
    issue_unroll = 8

    def page_copies(step, slot, k, p):
        page = pt_ref[step * ng + k, p]
        return (pltpu.make_async_copy(clat_hbm.at[layer, page], lat_buf.at[slot, k, p], sem.at[slot, 0]),
                pltpu.make_async_copy(ckr_hbm.at[layer, page], kr_buf.at[slot, k, p], sem.at[slot, 1]))

    @pl.when(b == 0)
    def _():
        def body(g, _):
            for k in range(ng):
                for i in range(issue_unroll):
                    for cp in page_copies(0, 0, k, g * issue_unroll + i):
                        cp.start()
        lax.fori_loop(0, n_pages // issue_unroll, body, None)

    step_next = lax.rem(b + 1, nb)
    slot_next = lax.rem(b + 1, 2)

    tile_rows = lambda x: jnp.concatenate([x] * MLA_HEADS, axis=0)
    own_n = (lax.broadcasted_iota(jnp.int32, (hq, nk), 0) // t_new
             == lax.broadcasted_iota(jnp.int32, (hq, nk), 1) // MLA_NOPE)
    own_r = (lax.broadcasted_iota(jnp.int32, (hq, LANES), 0) // t_new
             == lax.broadcasted_iota(jnp.int32, (hq, LANES), 1) // MLA_ROPE)
    wkt = wkt_ref[...]

    def query_operands(k):
        qn_sel = jnp.where(own_n, tile_rows(qn_ref[k]), 0.0).astype(BF16)
        q_abs = jnp.dot(qn_sel, wkt, preferred_element_type=F32).astype(BF16)
        qr_sel = jnp.where(own_r, tile_rows(qr_ref[k]), 0.0).astype(BF16)
        return jnp.concatenate([wkt, q_abs], axis=0), qr_sel

    def scores(operands, lat_bf, kr4_bf):
        lhs, qr_sel = operands
        n = lat_bf.shape[0]
        both = _dot_nt(lhs, lat_bf)
        sq = both[:nk] * both[:nk]
        ssq = []
        for h in range(MLA_HEADS):
            part = sq[h * MLA_NOPE:(h + 1) * MLA_NOPE].reshape(MLA_NOPE // SUBLANES, SUBLANES, n).sum(axis=0)
            shift = SUBLANES // 2
            while shift >= 1:
                part = part + pltpu.roll(part, shift, 0)
                shift //= 2
            ssq.append(part)
        ssq = jnp.concatenate(ssq, axis=0)
        s_rope = jnp.dot(qr_sel, kr4_bf, preferred_element_type=F32)
        return (both[nk:] * lax.rsqrt(ssq * (1.0 / MLA_NOPE) + EPS) + s_rope) * MLA_SCALE

    t_q = lax.broadcasted_iota(jnp.int32, (hq, t_new), 0) % t_new
    t_k = lax.broadcasted_iota(jnp.int32, (hq, t_new), 1)
    ops, latn_bf, s_new, m = [], [], [], []
    for k in range(ng):
        ops.append(query_operands(k))
        latn_bf.append(latn_ref[k].astype(BF16))
        s = scores(ops[k], latn_bf[k], kr4n_ref[k].T.astype(BF16))
        s_new.append(jnp.where(t_k <= t_q, s, -jnp.inf))
        m.append(jnp.max(s_new[k], axis=-1, keepdims=True))

    slot = b % 2
    for k in range(ng):
        for p in range(n_pages):
            for cp in page_copies(b, slot, k, p):
                cp.wait()

    for c in range(n_pages // chunk_pages):
        p0 = c * chunk_pages
        for k in range(ng):
            for i in range(chunk_pages):
                for cp in page_copies(step_next, slot_next, k, p0 + i):
                    cp.start()
            lat_bf = lat_buf[slot, k, p0:p0 + chunk_pages].reshape(n_tok, KV_LORA).astype(BF16)
            latbf_ref[k, c * n_tok:(c + 1) * n_tok, :] = lat_bf
            kr_t = jnp.concatenate([kr_buf[slot, k, p0 + i] for i in range(chunk_pages)], axis=1)
            s = scores(ops[k], lat_bf, tile_rows(kr_t.astype(BF16)))
            s_ref[k, :, c * n_tok:(c + 1) * n_tok] = s
            m[k] = jnp.maximum(m[k], jnp.max(s, axis=-1, keepdims=True))

    lane_v = lax.broadcasted_iota(jnp.int32, (t_new, MLA_HEADS * MLA_V), 1) // MLA_V
    for k in range(ng):
        p_old = jnp.exp(s_ref[k] - m[k])
        p_new = jnp.exp(s_new[k] - m[k])
        l = jnp.sum(p_old, axis=-1, keepdims=True) + jnp.sum(p_new, axis=-1, keepdims=True)
        acc = jnp.dot(p_old.astype(BF16), latbf_ref[k], preferred_element_type=F32)
        acc = acc + jnp.dot(p_new.astype(BF16), latn_bf[k], preferred_element_type=F32)
        ov = jnp.dot((acc / l).astype(BF16), wv_ref[...], preferred_element_type=F32)
        out = jnp.zeros((t_new, MLA_HEADS * MLA_V), F32)
        for h in range(MLA_HEADS):
            out = out + jnp.where(lane_v == h, ov[h * t_new:(h + 1) * t_new, :], 0.0)
        o_ref[k] = out

    @pl.when(b == nb - 1)
    def _():
        for k in range(ng):
            for p in range(n_pages):
                for cp in page_copies(step_next, slot_next, k, p):
                    cp.wait()


def mla_sample_attention(qn, qr, lat_all, kr4_new, cache_lat, cache_kr_t, page_table, layer, p, *, chunk_pages=8):
    b, t_new, _ = qn.shape
    n_pages = page_table.shape[1]
    ng = 2 if b % 2 == 0 else 1
    assert n_pages % chunk_pages == 0 and n_pages % 8 == 0
    assert t_new == SUBLANES
    n_past = n_pages * PAGE_SIZE
    seq = lambda n: pl.BlockSpec((ng, t_new, n), lambda i, pt: (i, 0, 0))
    lspec = lambda a: pl.BlockSpec((None,) + a.shape[1:], lambda i, pt: (layer, 0, 0))
    return pl.pallas_call(
        functools.partial(_paged_body, layer=layer, n_pages=n_pages, chunk_pages=chunk_pages),
        out_shape=jax.ShapeDtypeStruct((b, t_new, MLA_HEADS * MLA_V), F32),
        grid_spec=pltpu.PrefetchScalarGridSpec(
            num_scalar_prefetch=1,
            grid=(b // ng,),
            in_specs=[seq(MLA_HEADS * MLA_NOPE), seq(LANES),
                      pl.BlockSpec((None, ng, t_new, KV_LORA), lambda i, pt: (layer, i, 0, 0)), seq(LANES),
                      lspec(p['mla_wk_t']), lspec(p['mla_wv']), _any_spec(), _any_spec()],
            out_specs=seq(MLA_HEADS * MLA_V),
            scratch_shapes=[pltpu.VMEM((2, ng, n_pages, PAGE_SIZE, KV_LORA), F32),
                            pltpu.VMEM((2, ng, n_pages, MLA_ROPE, PAGE_SIZE), F32),
                            pltpu.VMEM((ng, n_past, KV_LORA), BF16),
                            pltpu.VMEM((ng, MLA_HEADS * t_new, n_past), F32),
                            pltpu.SemaphoreType.DMA((2, 2))],
        ),
        compiler_params=_cparams("arbitrary"),
        name="mla_sample_attention",
    )(page_table, qn, qr, lat_all, kr4_new, p['mla_wk_t'], p['mla_wv'], cache_lat, cache_kr_t)


def _mla_proj_sample_body(u_ref, cos_ref, slo_ref, shi_ref, qnorm_ref, wq_ref, gn_ref, gr_ref, kvnorm_ref, gkr_ref,
                          _lat_alias, _kr_alias, qn_ref, qr_ref, kr4_ref, lat_ref, kr_ref):
    gb, r, _ = u_ref.shape
    m = gb * r
    nk = MLA_HEADS * MLA_NOPE
    u = u_ref[...]
    tab = lambda t_ref: jnp.broadcast_to(t_ref[...], (gb, r, LANES)).reshape(m, LANES)
    cos, slo, shi = tab(cos_ref), tab(slo_ref), tab(shi_ref)
    i = lax.broadcasted_iota(jnp.int32, (LANES, LANES), 0) // MLA_ROPE
    j = lax.broadcasted_iota(jnp.int32, (LANES, LANES), 1) // MLA_ROPE
    ones_r = (i == j).astype(BF16)

    cq = _rms(u[..., :Q_LORA], qnorm_ref[...]).reshape(m, Q_LORA)
    q = jnp.dot(cq.astype(BF16), wq_ref[...], preferred_element_type=F32)
    qn, qr = q[:, :nk], q[:, nk:]
    ssn = jnp.dot((qn * qn).astype(BF16), _head_ones(nk), preferred_element_type=F32)
    qn_ref[...] = (qn * lax.rsqrt(ssn * (1.0 / MLA_NOPE) + EPS) * gn_ref[...]).reshape(gb, r, nk)
    ssr = jnp.dot((qr * qr).astype(BF16), ones_r, preferred_element_type=F32)
    qr = qr * lax.rsqrt(ssr * (1.0 / MLA_ROPE) + EPS) * gr_ref[...]
    qr_ref[...] = _rope(qr, cos, slo, shi).reshape(gb, r, LANES)

    lat_ref[...] = _rms(u[..., Q_LORA:Q_LORA + KV_LORA], kvnorm_ref[...])
    kr = u[..., Q_LORA + KV_LORA:].reshape(m, LANES)
    kr = kr * lax.rsqrt(jnp.mean(kr * kr, axis=-1, keepdims=True) + EPS) * gkr_ref[...]
    kr = _rope(kr, cos, slo, shi)
    kr4_ref[...] = kr.reshape(gb, r, LANES)
    kr_ref[...] = kr[:, :MLA_ROPE].reshape(gb, r, MLA_ROPE)


def _mla_proj_body(u_ref, cos_ref, slo_ref, shi_ref, qnorm_ref, wq_ref, gq_ref, kvnorm_ref, gkr_ref,
                   wkv_ref, gk_ref, _lat_alias, _kr_alias, q_ref, k_ref, v_ref, lat_ref, kr_ref):
    gb, r, _ = u_ref.shape
    m = gb * r
    u = u_ref[...]
    tab = lambda t_ref: jnp.broadcast_to(t_ref[...], (gb, r, MLA_BLOCK)).reshape(m, MLA_BLOCK)
    cos, slo, shi = tab(cos_ref), tab(slo_ref), tab(shi_ref)
    tile = lambda t: jnp.concatenate([t] * MLA_HEADS, axis=1)
    ones = _segment_ones()
    lane = lax.broadcasted_iota(jnp.int32, (1, MLA_BLOCK), 1)
    inv_cnt = jnp.where(lane < MLA_NOPE_OFF, 1.0 / MLA_ROPE, 1.0 / MLA_NOPE)

    cq = _rms(u[..., :Q_LORA], qnorm_ref[...]).reshape(m, Q_LORA)
    q = jnp.dot(cq.astype(BF16), wq_ref[...], preferred_element_type=F32)
    q = _blocked_rms(q, ones, inv_cnt, gq_ref[...])
    q = _rope(q, tile(cos), tile(slo), tile(shi))
    q_ref[...] = q.reshape(gb, r, MLA_QK).astype(q_ref.dtype)

    lat = _rms(u[..., Q_LORA:Q_LORA + KV_LORA], kvnorm_ref[...])
    lat_ref[...] = lat
    kr = u[..., Q_LORA + KV_LORA:].reshape(m, MLA_BLOCK)
    kr = kr * lax.rsqrt(jnp.sum(kr * kr, axis=-1, keepdims=True) * (1.0 / MLA_ROPE) + EPS) * gkr_ref[...]
    kr = _rope(kr, cos, slo, shi)
    kr_ref[...] = kr[:, :MLA_ROPE].reshape(gb, r, MLA_ROPE)

    kv = jnp.dot(lat.reshape(m, KV_LORA).astype(BF16), wkv_ref[...], preferred_element_type=F32)
    k = _blocked_rms(kv[:, :MLA_QK], ones, inv_cnt, gk_ref[...]) + tile(kr)
    k_ref[...] = k.reshape(gb, r, MLA_QK).astype(k_ref.dtype)
    lane_v = lax.broadcasted_iota(jnp.int32, (1, MLA_QK), 1) % MLA_BLOCK
    v = kv[:, MLA_QK:] + jnp.where(lane_v == MLA_V, 1.0, 0.0)
    v_ref[...] = v.reshape(gb, r, MLA_QK).astype(v_ref.dtype)


def _rope_tables(pos0, l, copies):
    half = MLA_ROPE // 2
    inv_freq = ROPE_THETA ** (-jnp.arange(half, dtype=F32) / half)
    ang = (pos0 + jnp.arange(l)).astype(F32)[:, None] * inv_freq
    cos, sin = jnp.cos(ang), jnp.sin(ang)
    rest = LANES - copies * MLA_ROPE
    lay = lambda t, fill: jnp.concatenate([jnp.tile(t, (1, copies)), jnp.full((l, rest), fill, F32)], axis=1)
    zero = jnp.zeros_like(sin)
    return (lay(jnp.concatenate([cos, cos], 1), 1.0), lay(jnp.concatenate([-sin, zero], 1), 0.0),
            lay(jnp.concatenate([zero, sin], 1), 0.0))


def _head_blocked(rope_part, nope_part):
    ref = rope_part if rope_part is not None else nope_part
    lead = ref.shape[:-2]
    cols = []
    for h in range(MLA_HEADS):
        rp = rope_part[..., h, :] if rope_part is not None else jnp.zeros(lead + (MLA_ROPE,), F32)
        npart = nope_part[..., h, :] if nope_part is not None else jnp.zeros(lead + (MLA_NOPE,), F32)
        cols += [rp, jnp.zeros(lead + (MLA_NOPE_OFF - MLA_ROPE,), F32), npart]
    return jnp.concatenate(cols, axis=-1)


def _mla_project_call(body, name, u, tables, params, layer, outs, stacks, rows):
    b, l, _ = u.shape
    t = _Tiling(b, l, rows)
    uspec = lambda n: pl.BlockSpec((t.gb, t.r, n), lambda i: (*t.tok_index(i), 0))
    tspec = pl.BlockSpec((t.r, LANES), lambda i: (i % t.nl, 0))
    sspec = lambda a: pl.BlockSpec((None, t.gb, t.r, a.shape[-1]), lambda i: (layer, *t.tok_index(i), 0))
    n_in = 4 + len(params)
    return pl.pallas_call(
        body,
        out_shape=[jax.ShapeDtypeStruct((b, l, n), dt) for n, dt in outs]
                  + [jax.ShapeDtypeStruct(a.shape, a.dtype) for a in stacks],
        grid=(t.steps,),
        in_specs=[uspec(PROJ_MLA), tspec, tspec, tspec] + [_lspec(a, layer) for a in params]
                 + [_any_spec() for _ in stacks],
        out_specs=[uspec(n) for n, _ in outs] + [sspec(a) for a in stacks],
        input_output_aliases={n_in + k: len(outs) + k for k in range(len(stacks))},
        compiler_params=_cparams("arbitrary"),
        name=name,
    )(u, *tables, *params, *stacks)


def mla_project_prompt(u, tables, p, layer, lat_out, kr_out, *, rows=512):
    params = [p['mla_q_norm'], p['mla_wq_blocked'], p['mla_gq_blocked'], p['mla_kv_norm'], p['mla_gkr_pad'],
              p['mla_wkv_blocked'], p['mla_gk_blocked']]
    return _mla_project_call(_mla_proj_body, "mla_project_prompt", u, tables, params, layer,
                             [(MLA_QK, BF16)] * 3, [lat_out, kr_out], rows)


def mla_project_sample(u, tables, p, layer, lat_out, kr_out, *, rows=512):
    params = [p['mla_q_norm'], p['mla_wq_compact'], p['mla_gn_compact'], p['mla_gr_compact'], p['mla_kv_norm'],
              p['mla_gkr_tiled']]
    return _mla_project_call(_mla_proj_sample_body, "mla_project_sample", u, tables, params, layer,
                             [(MLA_HEADS * MLA_NOPE, F32), (LANES, F32), (LANES, F32)], [lat_out, kr_out], rows)


def _flash_body(q_ref, k_ref, v_ref, o_ref, *, tq, nq):
    i = pl.program_id(1)
    row = lax.broadcasted_iota(jnp.int32, (tq, tq), 0)
    col = lax.broadcasted_iota(jnp.int32, (tq, tq), 1)
    c = MLA_SCALE * math.log2(math.e)
    blk = lambda h: pl.ds(h * MLA_BLOCK, MLA_BLOCK)
    qs = [q_ref[:, blk(h)] for h in range(MLA_HEADS)]

    def update(carry, rows, mask):
        new = []
        for h in range(MLA_HEADS):
            m, acc = carry[h]
            s = _dot_nt(qs[h], k_ref[rows, blk(h)])
            if mask:
                s = jnp.where(row >= col, s, -jnp.inf)
            m_new = jnp.maximum(m, jnp.max(s, axis=-1, keepdims=True))
            alpha = jnp.exp2((m - m_new) * c)
            p = jnp.exp2((s - m_new) * c)
            acc = alpha * acc + jnp.dot(p.astype(BF16), v_ref[rows, blk(h)], preferred_element_type=F32)
            new.append((m_new, acc))
        return tuple(new)

    def run(n_past):
        carry = tuple((jnp.full((tq, 1), -jnp.inf, F32), jnp.zeros((tq, MLA_BLOCK), F32)) for _ in range(MLA_HEADS))
        for j in range(n_past):
            carry = update(carry, pl.ds(j * tq, tq), False)
        carry = update(carry, pl.ds(n_past * tq, tq), True)
        out = jnp.concatenate([acc[:, :MLA_V] / acc[:, MLA_V:MLA_V + 1] for _, acc in carry], axis=1)
        o_ref[...] = out.astype(o_ref.dtype)

    for n_past in range(nq):
        pl.when(i == n_past)(functools.partial(run, n_past))


def mla_prompt_attention(q, k, v, *, tq=512):
    b, l, _ = q.shape
    tq = min(tq, l)
    assert l % tq == 0
    return pl.pallas_call(
        functools.partial(_flash_body, tq=tq, nq=l // tq),
        out_shape=jax.ShapeDtypeStruct((b, l, MLA_HEADS * MLA_V), _mixer_out_dtype(l)),
        grid=(b, l // tq),
        in_specs=[pl.BlockSpec((None, tq, MLA_QK), lambda bi, i: (bi, i, 0)),
                  pl.BlockSpec((None, l, MLA_QK), lambda bi, i: (bi, 0, 0)),
                  pl.BlockSpec((None, l, MLA_QK), lambda bi, i: (bi, 0, 0))],
        out_specs=pl.BlockSpec((None, tq, MLA_HEADS * MLA_V), lambda bi, i: (bi, i, 0)),
        compiler_params=_cparams("arbitrary", "arbitrary"),
        name="mla_prompt_attention",
    )(q, k, v)


def prepare_params(raw):
    depth = raw['w_in'].shape[0]
    row = lambda a: a.reshape(depth, 1, -1)
    pad_heads = lambda a: jnp.pad(a, ((0, 0), (0, LANES - SSD_HEADS))).reshape(depth, 1, LANES)
    eye = jnp.eye(LRU_BLOCKS, dtype=F32)
    block_diag = lambda w: jnp.einsum('lhij,hg->lhigj', w, eye).reshape(depth, LRU_W, LRU_W)
    ones_h = jnp.ones((1, MLA_HEADS, 1), F32)
    per_head = lambda g: ones_h * g[:, None, :]
    tile4 = lambda g: jnp.tile(g, (1, MLA_HEADS)).reshape(depth, 1, -1)
    wq = raw['mla_w_uq'].reshape(depth, Q_LORA, MLA_HEADS, MLA_NOPE + MLA_ROPE)
    wkv = raw['mla_w_ukv'].reshape(depth, KV_LORA, MLA_HEADS, MLA_NOPE + MLA_V)
    w_k = wkv[..., :MLA_NOPE].reshape(depth, KV_LORA, MLA_HEADS * MLA_NOPE)
    w_v = wkv[..., MLA_NOPE:]
    w_v_blocked = jnp.pad(w_v, ((0, 0), (0, 0), (0, 0), (0, MLA_BLOCK - MLA_V))).reshape(depth, KV_LORA, MLA_QK)
    return dict(
        norm_ffn1=row(raw['norm_ffn1']), norm_mix=row(raw['norm_mix']), norm_ffn2=row(raw['norm_ffn2']),
        out_norm=row(raw['out_norm']),
        ffn1_in=raw['w_ffn1_in'].astype(BF16), ffn1_out=raw['w_ffn1_out'].astype(BF16),
        ffn2_in=raw['w_ffn2_in'].astype(BF16), ffn2_out=raw['w_ffn2_out'].astype(BF16),
        w_in=raw['w_in'], w_out=raw['w_out'].astype(BF16),
        ssd_conv_w=raw['ssd_conv_w'], ssd_conv_b=row(raw['ssd_conv_b']), ssd_dt_bias=pad_heads(raw['ssd_dt_bias']),
        ssd_a_log=pad_heads(raw['ssd_a_log']), ssd_d=row(jnp.repeat(raw['ssd_d'], SSD_P, axis=1)),
        lru_conv_w=raw['lru_conv_w'], lru_conv_b=row(raw['lru_conv_b']),
        lru_w_ri=jnp.concatenate([block_diag(raw['lru_w_r']), block_diag(raw['lru_w_i'])], axis=2).astype(BF16),
        lru_b_ri=row(jnp.concatenate([raw['lru_b_r'], raw['lru_b_i']], axis=1)), lru_lambda=row(raw['lru_lambda']),
        hgrn_lb=raw['hgrn_lb'],
        mla_q_norm=row(raw['mla_q_norm']), mla_kv_norm=row(raw['mla_kv_norm']),
        mla_wq_blocked=_head_blocked(wq[..., MLA_NOPE:], wq[..., :MLA_NOPE]).astype(BF16),
        mla_gq_blocked=row(_head_blocked(per_head(raw['mla_qn_rope']), per_head(raw['mla_qn_nope']))),
        mla_gkr_pad=row(jnp.pad(raw['mla_kn_rope'], ((0, 0), (0, MLA_BLOCK - MLA_ROPE)))),
        mla_wkv_blocked=jnp.concatenate([_head_blocked(None, wkv[..., :MLA_NOPE]), w_v_blocked], axis=2).astype(BF16),
        mla_gk_blocked=row(_head_blocked(None, per_head(raw['mla_kn_nope']))),
        mla_wq_compact=jnp.concatenate([wq[..., :MLA_NOPE].reshape(depth, Q_LORA, -1),
                                        wq[..., MLA_NOPE:].reshape(depth, Q_LORA, -1)], axis=2).astype(BF16),
        mla_gn_compact=tile4(raw['mla_qn_nope'] * raw['mla_kn_nope']), mla_gr_compact=tile4(raw['mla_qn_rope']),
        mla_gkr_tiled=tile4(raw['mla_kn_rope']),
        mla_wk_t=jnp.swapaxes(w_k, 1, 2).astype(BF16),
        mla_wv=w_v.reshape(depth, KV_LORA, MLA_HEADS * MLA_V).astype(BF16),
    )


def _layer(x, mods, layer, row0, p, init_layer, init, outs, tables, paged, seq_block):
    conv_a, ssm_a, conv_b, h_b, s_c = init
    lat_o, kr_o, ssm_o, conva_o, h_o, convb_o, s_o = outs
    x = ffn_halfstep(x, mods, layer, row0, 0, p['norm_ffn1'], p['ffn1_in'], p['ffn1_out'])
    kr_copies = 1 if paged is None else LANES // MLA_ROPE
    u_a, u_b, u_c, u_d = mixer_projection(x, mods, layer, row0, p['norm_mix'], p['w_in'], kr_copies)
    y_a, conva_o, ssm_o = ssd_mixer(u_a, conv_a, ssm_a, init_layer, p, layer, conva_o, ssm_o, gb=seq_block)
    y_b, convb_o, h_o = rglru_mixer(u_b, conv_b, h_b, init_layer, p, layer, convb_o, h_o, gb=seq_block)
    y_c, s_o = hgrn2_mixer(u_c, s_c, init_layer, p['hgrn_lb'], layer, s_o, gb=seq_block)
    if paged is None:
        q, k, v, lat_o, kr_o = mla_project_prompt(u_d, tables, p, layer, lat_o, kr_o)
        y_d = mla_prompt_attention(q, k, v)
    else:
        cache_lat, cache_kr_t, page_table = paged
        qn, qr, kr4, lat_o, kr_o = mla_project_sample(u_d, tables, p, layer, lat_o, kr_o)
        y_d = mla_sample_attention(qn, qr, lat_o, kr4, cache_lat, cache_kr_t, page_table, layer, p)
    x = merge_ffn_halfstep(x, (y_a, y_b, y_c, y_d), mods, layer, row0, p['out_norm'], p['w_out'],
                           p['norm_ffn2'], p['ffn2_in'], p['ffn2_out'])
    return x, (lat_o, kr_o, ssm_o, conva_o, h_o, convb_o, s_o)


def _result_stacks(b, l):
    z = lambda *s: jnp.zeros((DEPTH, b) + s, F32)
    return (z(l, KV_LORA), z(l, MLA_ROPE), z(SSD_HEADS * SSD_P, SSD_N), z(CONV_W - 1, SSD_CONV_DIM),
            z(1, LRU_W), z(CONV_W - 1, LRU_W), z(HG_HEADS, HG_DK, HG_DV))


def kernel(x_prompt, x_sample, cache_mla_latent, cache_mla_krope, state_ssd, state_ssd_conv, state_lru,
           state_lru_conv, state_hgrn, page_table, c_prompt, c_sample, w_ada, b_ada, norm_ffn1, w_ffn1_in,
           w_ffn1_out, norm_mix, w_in, ssd_conv_w, ssd_conv_b, ssd_dt_bias, ssd_a_log, ssd_d, lru_conv_w,
           lru_conv_b, lru_w_r, lru_b_r, lru_w_i, lru_b_i, lru_lambda, hgrn_lb, mla_q_norm, mla_w_uq,
           mla_kv_norm, mla_w_ukv, mla_qn_nope, mla_qn_rope, mla_kn_nope, mla_kn_rope, out_norm, w_out,
           norm_ffn2, w_ffn2_in, w_ffn2_out):
    p = prepare_params(dict(
        norm_ffn1=norm_ffn1, w_ffn1_in=w_ffn1_in, w_ffn1_out=w_ffn1_out, norm_mix=norm_mix, w_in=w_in,
        ssd_conv_w=ssd_conv_w, ssd_conv_b=ssd_conv_b, ssd_dt_bias=ssd_dt_bias, ssd_a_log=ssd_a_log, ssd_d=ssd_d,
        lru_conv_w=lru_conv_w, lru_conv_b=lru_conv_b, lru_w_r=lru_w_r, lru_b_r=lru_b_r, lru_w_i=lru_w_i,
        lru_b_i=lru_b_i, lru_lambda=lru_lambda, hgrn_lb=hgrn_lb, mla_q_norm=mla_q_norm, mla_w_uq=mla_w_uq,
        mla_kv_norm=mla_kv_norm, mla_w_ukv=mla_w_ukv, mla_qn_nope=mla_qn_nope, mla_qn_rope=mla_qn_rope,
        mla_kn_nope=mla_kn_nope, mla_kn_rope=mla_kn_rope, out_norm=out_norm, w_out=w_out, norm_ffn2=norm_ffn2,
        w_ffn2_in=w_ffn2_in, w_ffn2_out=w_ffn2_out))
    bp, lp = x_prompt.shape[:2]
    bs, ls = x_sample.shape[:2]
    dt = x_prompt.dtype
    mods = ada_modulation(jnp.concatenate([c_sample, c_prompt], axis=0), w_ada, b_ada)
    mods = mods.reshape(DEPTH, bs + bp, 1, N_MOD * D_MODEL)
    cache_kr_t = jnp.swapaxes(cache_mla_krope, 2, 3)
    nrow = SSD_HEADS * SSD_P
    zero = lambda *s: jnp.zeros((1, bp) + s, F32)
    init_p = (zero(CONV_W - 1, SSD_CONV_DIM), zero(nrow, SSD_N), zero(CONV_W - 1, LRU_W), zero(1, LRU_W),
              zero(HG_HEADS, HG_DK, HG_DV))
    init_s = (state_ssd_conv, state_ssd.reshape(DEPTH, bs, nrow, SSD_N), state_lru_conv,
              state_lru.reshape(DEPTH, bs, 1, LRU_W), state_hgrn)
    tables_p = _rope_tables(0, lp, 1)
    tables_s = _rope_tables(PAST_LEN, ls, LANES // MLA_ROPE)
    out_p, out_s = _result_stacks(bp, lp), _result_stacks(bs, ls)
    yp, ys = x_prompt, x_sample
    sample_block = math.gcd(bs, SUBLANES)
    for l in range(DEPTH):
        yp, out_p = _layer(yp, mods, l, bs, p, 0, init_p, out_p, tables_p, None, 1)
        ys, out_s = _layer(ys, mods, l, 0, p, l, init_s, out_s, tables_s,
                           (cache_mla_latent, cache_kr_t, page_table), sample_block)

    def finish(outs, b):
        lat, kr, ssm, conva, h, convb, s = outs
        return (lat, kr, ssm.reshape(DEPTH, b, SSD_HEADS, SSD_P, SSD_N), conva, h.reshape(DEPTH, b, LRU_W), convb, s)

    lat_p, kr_p, ssd_p, ssdc_p, lru_p, lruc_p, hg_p = [a.astype(dt) for a in finish(out_p, bp)]
    lat_s, kr_s, ssd_s, ssdc_s, lru_s, lruc_s, hg_s = [a.astype(dt) for a in finish(out_s, bs)]
    return (yp, ys, lat_p, lat_s, kr_p, kr_s, ssd_p, ssd_s, ssdc_p, ssdc_s, lru_p, lru_s, lruc_p, lruc_s, hg_p, hg_s)
```

```python
import functools
import math

import jax
import jax.numpy as jnp
from jax import lax
from jax.experimental import pallas as pl
from jax.experimental.pallas import tpu as pltpu

F32 = jnp.float32
BF16 = jnp.bfloat16

D_MODEL = 1024
DEPTH = 4
PAST_LEN = 8192
PAGE_SIZE = 128
W_GROUP = 256
HEAD_DIM = 64
SSD_HEADS = 4
SSD_P = 64
SSD_N = 128
SSD_GROUPS = 2
SSD_CHUNK = 128
CONV_W = 4
SSD_CONV_DIM = W_GROUP + 2 * SSD_GROUPS * SSD_N
LRU_W = W_GROUP
LRU_BLOCKS = 4
LRU_BW = LRU_W // LRU_BLOCKS
LRU_C = 8.0
HG_HEADS = 4
HG_DK = 64
HG_DV = 64
HG_CHUNK = 64
MLA_HEADS = 4
MLA_NOPE = 64
MLA_ROPE = 32
MLA_V = 64
Q_LORA = 256
KV_LORA = 128
ROPE_THETA = 10000.0
MLA_SCALE = 1.0 / math.sqrt(MLA_NOPE + MLA_ROPE)
D_FF = 2816
N_MOD = 9
EPS = 1e-6
IN_SSD = W_GROUP + SSD_CONV_DIM + SSD_HEADS
IN_LRU = 2 * LRU_W
IN_HG = 4 * HG_HEADS * HG_DK
IN_MLA = Q_LORA + KV_LORA + MLA_ROPE
OFF_LRU = IN_SSD
OFF_HG = OFF_LRU + IN_LRU
OFF_MLA = OFF_HG + IN_HG
N_IN = OFF_MLA + IN_MLA

LANES = 128
SUBLANES = 8
VMEM_BYTES_V7X = 64 * 1024 * 1024
VMEM_LIMIT = VMEM_BYTES_V7X * 3 // 4

PROJ_SSD = W_GROUP + SSD_CONV_DIM + LANES
PROJ_LRU = IN_LRU
PROJ_HG = IN_HG
PROJ_MLA = Q_LORA + KV_LORA + LANES
PROJ_ALL = PROJ_SSD + PROJ_LRU + PROJ_HG + PROJ_MLA


def _cparams(*sem):
    return pltpu.CompilerParams(dimension_semantics=sem, vmem_limit_bytes=VMEM_LIMIT)


def _silu(x):
    return x * jax.nn.sigmoid(x)


def _rms(x, w):
    return x * lax.rsqrt(jnp.mean(x * x, axis=-1, keepdims=True) + EPS) * w


def _lspec(arr, layer):
    nd = arr.ndim - 1
    return pl.BlockSpec((None,) + arr.shape[1:], lambda *_: (layer,) + (0,) * nd)


def _any_spec():
    return pl.BlockSpec(memory_space=pl.ANY)


def _mixer_out_dtype(l):
    return BF16 if l % (2 * SUBLANES) == 0 else F32


def _ada_body(c_ref, w_ref, b_ref, o_ref):
    a = _silu(c_ref[...]).astype(BF16)
    o_ref[...] = jnp.dot(a, w_ref[...].astype(BF16), preferred_element_type=F32) + b_ref[...]


def ada_modulation(c_all, w_ada, b_ada, *, tn=1152):
    r, d = c_all.shape
    depth, _, n = w_ada.shape
    return pl.pallas_call(
        _ada_body,
        out_shape=jax.ShapeDtypeStruct((depth, r, n), F32),
        grid=(depth, n // tn),
        in_specs=[
            pl.BlockSpec((r, d), lambda l, j: (0, 0)),
            pl.BlockSpec((None, d, tn), lambda l, j: (l, 0, j)),
            pl.BlockSpec((None, 1, tn), lambda l, j: (l, 0, j)),
        ],
        out_specs=pl.BlockSpec((None, r, tn), lambda l, j: (l, 0, j)),
        compiler_params=_cparams("arbitrary", "arbitrary"),
        name="ada_modulation",
    )(c_all, w_ada, b_ada.reshape(depth, 1, n))


class _Tiling:
    def __init__(self, b, l, rows):
        if l >= rows:
            assert l % rows == 0
            self.gb, self.r = 1, rows
        else:
            gb = min(b, max(1, rows // l))
            assert b % gb == 0
            self.gb, self.r = gb, l
        self.b, self.l = b, l
        self.nb = b // self.gb
        self.nl = l // self.r
        self.steps = self.nb * self.nl
        self.m = self.gb * self.r

    def tok_index(self, i):
        return (i // self.nl, i % self.nl)


def _mod_spec(t, layer, row0, k):
    assert row0 % t.gb == 0
    return pl.BlockSpec((None, t.gb, 1, D_MODEL), lambda i, *_: (layer, row0 // t.gb + i // t.nl, 0, k))


def _ffn_body(x_ref, sh_ref, sc_ref, gt_ref, nw_ref, w_in_ref, w_out_ref, o_ref):
    gb, r, d = x_ref.shape
    x = x_ref[...]
    h = _rms(x, nw_ref[...]) * (1.0 + sc_ref[...]) + sh_ref[...]
    gu = jnp.dot(h.reshape(gb * r, d).astype(BF16), w_in_ref[...], preferred_element_type=F32)
    a = _silu(gu[:, :D_FF]) * gu[:, D_FF:]
    y = jnp.dot(a.astype(BF16), w_out_ref[...], preferred_element_type=F32)
    o_ref[...] = x + 0.5 * gt_ref[...] * y.reshape(gb, r, d)


def ffn_halfstep(x, mods, layer, row0, k0, norm_w, w_in, w_out, *, rows=512):
    b, l, d = x.shape
    t = _Tiling(b, l, rows)
    xspec = pl.BlockSpec((t.gb, t.r, d), lambda i: (*t.tok_index(i), 0))
    return pl.pallas_call(
        _ffn_body,
        out_shape=jax.ShapeDtypeStruct(x.shape, F32),
        grid=(t.steps,),
        in_specs=[xspec, _mod_spec(t, layer, row0, k0), _mod_spec(t, layer, row0, k0 + 1),
                  _mod_spec(t, layer, row0, k0 + 2), _lspec(norm_w, layer), _lspec(w_in, layer), _lspec(w_out, layer)],
        out_specs=xspec,
        compiler_params=_cparams("arbitrary"),
        name="ffn_halfstep",
    )(x, mods, mods, mods, norm_w, w_in, w_out)


def _pack_w_in(w_ref, wp_ref, kr_copies):
    d = w_ref.shape[0]
    rows = 256
    off_kr = OFF_MLA + Q_LORA + KV_LORA
    dst_mla = PROJ_SSD + PROJ_LRU + PROJ_HG

    def chunk(c, _):
        r = pl.ds(pl.multiple_of(c * rows, rows), rows)
        put = lambda dst, src, n: wp_ref.__setitem__((r, pl.ds(dst, n)), w_ref[r, pl.ds(src, n)].astype(BF16))
        zero = lambda dst, n: wp_ref.__setitem__((r, pl.ds(dst, n)), jnp.zeros((rows, n), BF16))
        put(0, 0, W_GROUP + SSD_CONV_DIM)
        zero(W_GROUP + SSD_CONV_DIM, LANES)
        put(W_GROUP + SSD_CONV_DIM, W_GROUP + SSD_CONV_DIM, SSD_HEADS)
        put(PROJ_SSD, OFF_LRU, IN_LRU)
        put(PROJ_SSD + PROJ_LRU, OFF_HG, IN_HG)
        put(dst_mla, OFF_MLA, Q_LORA + KV_LORA)
        zero(dst_mla + Q_LORA + KV_LORA, LANES)
        for k in range(kr_copies):
            put(dst_mla + Q_LORA + KV_LORA + k * MLA_ROPE, off_kr, MLA_ROPE)

    lax.fori_loop(0, d // rows, chunk, None)


def _proj_body(x_ref, sh_ref, sc_ref, nw_ref, w_ref, ssd_ref, lru_ref, hg_ref, mla_ref, wp_ref, *, kr_copies):
    @pl.when(pl.program_id(0) == 0)
    def _():
        _pack_w_in(w_ref, wp_ref, kr_copies)

    gb, r, d = x_ref.shape
    h = _rms(x_ref[...], nw_ref[...]) * (1.0 + sc_ref[...]) + sh_ref[...]
    h = h.reshape(gb * r, d).astype(BF16)
    off = 0
    for o_ref in (ssd_ref, lru_ref, hg_ref, mla_ref):
        n = o_ref.shape[-1]
        u = jnp.dot(h, wp_ref[:, off:off + n], preferred_element_type=F32)
        o_ref[...] = u.reshape(gb, r, n)
        off += n


def mixer_projection(x, mods, layer, row0, norm_w, w_in, kr_copies, *, rows=512):
    b, l, d = x.shape
    t = _Tiling(b, l, rows)
    widths = (PROJ_SSD, PROJ_LRU, PROJ_HG, PROJ_MLA)
    xspec = pl.BlockSpec((t.gb, t.r, d), lambda i: (*t.tok_index(i), 0))
    w_spec = pl.BlockSpec((None, d, N_IN), lambda i: (layer, 0, 0), pipeline_mode=pl.Buffered(1))
    return pl.pallas_call(
        functools.partial(_proj_body, kr_copies=kr_copies),
        out_shape=[jax.ShapeDtypeStruct((b, l, n), F32) for n in widths],
        grid=(t.steps,),
        in_specs=[xspec, _mod_spec(t, layer, row0, 3), _mod_spec(t, layer, row0, 4), _lspec(norm_w, layer), w_spec],
        out_specs=[pl.BlockSpec((t.gb, t.r, n), lambda i: (*t.tok_index(i), 0)) for n in widths],
        scratch_shapes=[pltpu.VMEM((d, PROJ_ALL), BF16)],
        compiler_params=_cparams("arbitrary"),
        name="mixer_projection",
    )(x, mods, mods, norm_w, w_in)


def _head_ones(n):
    i = lax.broadcasted_iota(jnp.int32, (n, n), 0) // HEAD_DIM
    j = lax.broadcasted_iota(jnp.int32, (n, n), 1) // HEAD_DIM
    return (i == j).astype(BF16)


def _merge_ffn_body(x_ref, ya_ref, yb_ref, yc_ref, yd_ref, g2_ref, sh_ref, sc_ref, g3_ref, onw_ref, wo_ref,
                    nw_ref, w_in_ref, w_out_ref, o_ref):
    gb, r, d = x_ref.shape
    ones = _head_ones(W_GROUP)
    acc = jnp.zeros((gb * r, d), F32)
    for k, y_ref in enumerate((ya_ref, yb_ref, yc_ref, yd_ref)):
        y = y_ref[...].astype(F32).reshape(gb * r, W_GROUP)
        ss = jnp.dot((y * y).astype(BF16), ones, preferred_element_type=F32)
        yn = y * lax.rsqrt(ss * (1.0 / HEAD_DIM) + EPS) * onw_ref[:, k * W_GROUP:(k + 1) * W_GROUP]
        acc += jnp.dot(yn.astype(BF16), wo_ref[k * W_GROUP:(k + 1) * W_GROUP, :], preferred_element_type=F32)
    x1 = x_ref[...] + g2_ref[...] * acc.reshape(gb, r, d)
    h = _rms(x1, nw_ref[...]) * (1.0 + sc_ref[...]) + sh_ref[...]
    gu = jnp.dot(h.reshape(gb * r, d).astype(BF16), w_in_ref[...], preferred_element_type=F32)
    a = _silu(gu[:, :D_FF]) * gu[:, D_FF:]
    y2 = jnp.dot(a.astype(BF16), w_out_ref[...], preferred_element_type=F32)
    o_ref[...] = x1 + 0.5 * g3_ref[...] * y2.reshape(gb, r, d)


def merge_ffn_halfstep(x, ys, mods, layer, row0, out_norm, w_out, norm_w, ffn_in, ffn_out, *, rows=512):
    b, l, d = x.shape
    t = _Tiling(b, l, rows)
    xspec = pl.BlockSpec((t.gb, t.r, d), lambda i: (*t.tok_index(i), 0))
    yspec = pl.BlockSpec((t.gb, t.r, W_GROUP), lambda i: (*t.tok_index(i), 0))
    return pl.pallas_call(
        _merge_ffn_body,
        out_shape=jax.ShapeDtypeStruct(x.shape, F32),
        grid=(t.steps,),
        in_specs=[xspec, yspec, yspec, yspec, yspec,
                  *[_mod_spec(t, layer, row0, k) for k in (5, 6, 7, 8)],
                  _lspec(out_norm, layer), _lspec(w_out, layer), _lspec(norm_w, layer),
                  _lspec(ffn_in, layer), _lspec(ffn_out, layer)],
        out_specs=xspec,
        compiler_params=_cparams("arbitrary"),
        name="merge_ffn_halfstep",
    )(x, *ys, mods, mods, mods, mods, out_norm, w_out, norm_w, ffn_in, ffn_out)


def _cumsum_rows(x):
    n = x.shape[0]
    row = lax.broadcasted_iota(jnp.int32, x.shape, 0)
    d = 1
    while d < n:
        x = x + jnp.where(row >= d, pltpu.roll(x, d, 0), 0.0)
        d *= 2
    return x


def _dot_nt(a, b):
    return lax.dot_general(a, b, (((1,), (1,)), ((), ())), preferred_element_type=F32)


def _dot_tn(a, b):
    return lax.dot_general(a, b, (((0,), (0,)), ((), ())), preferred_element_type=F32)


def _conv_init(ext_ref, cs_ref, q):
    ext_ref[:, pl.ds(q, SUBLANES), :] = jnp.zeros((ext_ref.shape[0], SUBLANES, ext_ref.shape[2]), F32)
    ext_ref[:, pl.ds(q + SUBLANES - (CONV_W - 1), CONV_W - 1), :] = cs_ref[...]


def _causal_conv(ext_ref, s, x, cw_ref, cb_ref, q):
    ext_ref[s, pl.ds(0, SUBLANES), :] = ext_ref[s, pl.ds(q, SUBLANES), :]
    ext_ref[s, pl.ds(SUBLANES, q), :] = x
    out = cb_ref[...]
    for k in range(CONV_W):
        out = out + cw_ref[k:k + 1, :] * ext_ref[s, pl.ds(SUBLANES - (CONV_W - 1) + k, q), :]
    return out


def _state_specs(gb, init_layer, layer, *shapes):
    ins = [pl.BlockSpec((None, gb) + s, lambda i, c, n=len(s): (init_layer, i) + (0,) * n) for s in shapes]
    outs = [pl.BlockSpec((None, gb) + s, lambda i, c, n=len(s): (layer, i) + (0,) * n) for s in shapes]
    return ins, outs


def _ssd_body(u_ref, cs_ref, st_ref, cw_ref, cb_ref, dtb_ref, alog_ref, dsk_ref, _c_alias, _s_alias,
              y_ref, cnew_ref, snew_ref, ext_ref, h_ref, *, q, nsteps):
    step = pl.program_id(1)
    gb, tb, _ = u_ref.shape
    hp = SSD_HEADS // SSD_GROUPS * SSD_P

    @pl.when(step == 0)
    def _():
        _conv_init(ext_ref, cs_ref, q)
        h_ref[...] = st_ref[...]

    row = lax.broadcasted_iota(jnp.int32, (q, q), 0)
    col = lax.broadcasted_iota(jnp.int32, (q, q), 1)
    causal = row >= col
    a = -jnp.exp(alog_ref[...])

    def chunk(s, i):
        tok = pl.ds(pl.multiple_of(i * q, q), q)
        u = u_ref[s, tok, :]
        z = u[:, :W_GROUP]
        conv = _causal_conv(ext_ref, s, u[:, W_GROUP:W_GROUP + SSD_CONV_DIM], cw_ref, cb_ref, q)
        xbc = _silu(conv)
        xs = xbc[:, :W_GROUP]
        bm = xbc[:, W_GROUP:W_GROUP + SSD_GROUPS * SSD_N].astype(BF16)
        cm = xbc[:, W_GROUP + SSD_GROUPS * SSD_N:].astype(BF16)
        dt = jax.nn.softplus(u[:, W_GROUP + SSD_CONV_DIM:] + dtb_ref[...])
        acum = _cumsum_rows(dt * a)
        acum_t = acum.T
        a_end = acum[q - 1:q, :]
        ys = []
        for g in range(SSD_GROUPS):
            bg = bm[:, g * SSD_N:(g + 1) * SSD_N]
            cg = cm[:, g * SSD_N:(g + 1) * SSD_N]
            cb = _dot_nt(cg, bg)
            h_prev = h_ref[s, g * hp:(g + 1) * hp, :]
            y_off = _dot_nt(cg, h_prev.astype(BF16))
            for j in range(SSD_HEADS // SSD_GROUPS):
                h = g * (SSD_HEADS // SSD_GROUPS) + j
                dt_h = dt[:, h:h + 1]
                ac_h = acum[:, h:h + 1]
                ae_h = a_end[:, h:h + 1]
                decay = jnp.exp(jnp.where(causal, ac_h - acum_t[h:h + 1, :], -jnp.inf))
                xs_h = xs[:, h * SSD_P:(h + 1) * SSD_P]
                xdt_h = xs_h * dt_h
                y_h = jnp.dot((cb * decay).astype(BF16), xdt_h.astype(BF16), preferred_element_type=F32)
                y_h = y_h + y_off[:, j * SSD_P:(j + 1) * SSD_P] * jnp.exp(ac_h)
                y_h = y_h + dsk_ref[:, h * SSD_P:(h + 1) * SSD_P] * xs_h
                ys.append(y_h)
                w_h = (xdt_h * jnp.exp(ae_h - ac_h)).astype(BF16)
                st_h = _dot_tn(w_h, bg)
                rows = pl.ds(h * SSD_P, SSD_P)
                h_ref[s, rows, :] = jnp.exp(ae_h) * h_ref[s, rows, :] + st_h
        y_ref[s, tok, :] = (jnp.concatenate(ys, axis=1) * _silu(z)).astype(y_ref.dtype)

    for s in range(gb):
        if tb == q:
            chunk(s, 0)
        else:
            lax.fori_loop(0, tb // q, lambda i, _: chunk(s, i), None, unroll=2)

    @pl.when(step == nsteps - 1)
    def _():
        cnew_ref[...] = ext_ref[:, pl.ds(q + SUBLANES - (CONV_W - 1), CONV_W - 1), :]
        snew_ref[...] = h_ref[...]


def ssd_mixer(u, conv_state, ssm_state, init_layer, p, layer, conv_out, ssm_out, *, gb=1, rows=1024):
    b, l, _ = u.shape
    q = min(SSD_CHUNK, l)
    tb = min(rows, l)
    assert l % tb == 0 and tb % q == 0 and b % gb == 0 and q % SUBLANES == 0
    nsteps = l // tb
    nrow = SSD_HEADS * SSD_P
    st_in, st_out = _state_specs(gb, init_layer, layer, (CONV_W - 1, SSD_CONV_DIM), (nrow, SSD_N))
    params = [p['ssd_conv_w'], p['ssd_conv_b'], p['ssd_dt_bias'], p['ssd_a_log'], p['ssd_d']]
    return pl.pallas_call(
        functools.partial(_ssd_body, q=q, nsteps=nsteps),
        out_shape=[jax.ShapeDtypeStruct((b, l, W_GROUP), _mixer_out_dtype(l)),
                   jax.ShapeDtypeStruct(conv_out.shape, F32), jax.ShapeDtypeStruct(ssm_out.shape, F32)],
        grid=(b // gb, nsteps),
        in_specs=[pl.BlockSpec((gb, tb, PROJ_SSD), lambda i, c: (i, c, 0)), *st_in,
                  *[_lspec(a, layer) for a in params], _any_spec(), _any_spec()],
        out_specs=[pl.BlockSpec((gb, tb, W_GROUP), lambda i, c: (i, c, 0)), *st_out],
        scratch_shapes=[pltpu.VMEM((gb, q + SUBLANES, SSD_CONV_DIM), F32), pltpu.VMEM((gb, nrow, SSD_N), F32)],
        input_output_aliases={8: 1, 9: 2},
        compiler_params=_cparams("arbitrary", "arbitrary"),
        name="ssd_mixer",
    )(u, conv_state, ssm_state, *params, conv_out, ssm_out)


def _linear_scan_rows(a, b, h_prev):
    n, w = a.shape
    seg = min(n, SUBLANES)
    pos = lax.broadcasted_iota(jnp.int32, a.shape, 0) % seg
    d = 1
    while d < seg:
        m = pos >= d
        b = jnp.where(m, a * pltpu.roll(b, d, 0) + b, b)
        a = jnp.where(m, a * pltpu.roll(a, d, 0), a)
        d *= 2
    hs, carry = [], h_prev
    for k in range(n // seg):
        h_k = b[k * seg:(k + 1) * seg] + a[k * seg:(k + 1) * seg] * carry
        hs.append(h_k)
        carry = h_k[seg - 1:seg, :]
    return jnp.concatenate(hs, axis=0)


def _lru_body(u_ref, cs_ref, h0_ref, cw_ref, cb_ref, wri_ref, bri_ref, lam_ref, _c_alias, _h_alias,
              y_ref, cnew_ref, hnew_ref, ext_ref, h_ref, *, q, nsteps):
    step = pl.program_id(1)
    gb, tb, _ = u_ref.shape

    @pl.when(step == 0)
    def _():
        _conv_init(ext_ref, cs_ref, q)
        h_ref[...] = h0_ref[...]

    sp = jax.nn.softplus(-lam_ref[...])

    def chunk(s, i):
        tok = pl.ds(pl.multiple_of(i * q, q), q)
        u = u_ref[s, tok, :]
        xb = _causal_conv(ext_ref, s, u[:, :LRU_W], cw_ref, cb_ref, q)
        ri = jnp.dot(xb.astype(BF16), wri_ref[...], preferred_element_type=F32) + bri_ref[...]
        r = jax.nn.sigmoid(ri[:, :LRU_W])
        i = jax.nn.sigmoid(ri[:, LRU_W:])
        log_a = -LRU_C * r * sp
        a = jnp.exp(log_a)
        bterm = jnp.sqrt(-jnp.tanh(log_a) * (a * a + 1.0)) * (i * xb)
        h = _linear_scan_rows(a, bterm, h_ref[s])
        h_ref[s] = h[q - 1:q, :]
        y_ref[s, tok, :] = (h * jax.nn.gelu(u[:, LRU_W:])).astype(y_ref.dtype)

    for s in range(gb):
        if tb == q:
            chunk(s, 0)
        else:
            lax.fori_loop(0, tb // q, lambda i, _: chunk(s, i), None)

    @pl.when(step == nsteps - 1)
    def _():
        cnew_ref[...] = ext_ref[:, pl.ds(q + SUBLANES - (CONV_W - 1), CONV_W - 1), :]
        hnew_ref[...] = h_ref[...]


def rglru_mixer(u, conv_state, h0, init_layer, p, layer, conv_out, h_out, *, gb=1, chunk=256, rows=2048):
    b, l, _ = u.shape
    q = min(chunk, l)
    tb = min(rows, l)
    assert l % tb == 0 and tb % q == 0 and b % gb == 0 and q % SUBLANES == 0
    nsteps = l // tb
    st_in, st_out = _state_specs(gb, init_layer, layer, (CONV_W - 1, LRU_W), (1, LRU_W))
    params = [p['lru_conv_w'], p['lru_conv_b'], p['lru_w_ri'], p['lru_b_ri'], p['lru_lambda']]
    return pl.pallas_call(
        functools.partial(_lru_body, q=q, nsteps=nsteps),
        out_shape=[jax.ShapeDtypeStruct((b, l, LRU_W), _mixer_out_dtype(l)),
                   jax.ShapeDtypeStruct(conv_out.shape, F32), jax.ShapeDtypeStruct(h_out.shape, F32)],
        grid=(b // gb, nsteps),
        in_specs=[pl.BlockSpec((gb, tb, PROJ_LRU), lambda i, c: (i, c, 0)), *st_in,
                  *[_lspec(a, layer) for a in params], _any_spec(), _any_spec()],
        out_specs=[pl.BlockSpec((gb, tb, LRU_W), lambda i, c: (i, c, 0)), *st_out],
        scratch_shapes=[pltpu.VMEM((gb, q + SUBLANES, LRU_W), F32), pltpu.VMEM((gb, 1, LRU_W), F32)],
        input_output_aliases={8: 1, 9: 2},
        compiler_params=_cparams("arbitrary", "arbitrary"),
        name="rglru_mixer",
    )(u, conv_state, h0, *params, conv_out, h_out)


def _hgrn_levels(c):
    ws, w = [], c // 2
    while w >= 1:
        ws.append(w)
        w //= 2
    return ws


def _boundary_rows(g, w):
    c, n = g.shape
    if 2 * w >= SUBLANES:
        gr = g.reshape(c // (2 * w), 2 * w, n)[:, w - 1:w, :]
        return jnp.broadcast_to(gr, (c // (2 * w), 2 * w, n)).reshape(c, n)
    r = lax.broadcasted_iota(jnp.int32, g.shape, 0) % (2 * w)
    out = g
    for delta in range(-(w - 1), w + 1):
        if delta != 0:
            out = jnp.where(r - (w - 1) == delta, pltpu.roll(g, delta % c, 0), out)
    return out


def _hgrn_body(u_ref, st0_ref, lbp_ref, _alias, y_ref, stn_ref, st_ref, *, c, nsteps, layer):
    step = pl.program_id(1)
    gb, tb, _ = u_ref.shape
    nk = HG_HEADS * HG_DK
    hv = lax.broadcasted_iota(jnp.int32, (nk, nk), 0) // HG_DV
    hk = lax.broadcasted_iota(jnp.int32, (nk, nk), 1) // HG_DK
    state_mask = hv == hk

    @pl.when(step == 0)
    def _():
        for s in range(gb):
            t = st0_ref[s].reshape(nk, HG_DV).T
            st_ref[s] = jnp.where(state_mask, jnp.concatenate([t] * HG_HEADS, axis=0), 0.0)

    p = lbp_ref[...]
    e = jnp.exp(p - jnp.max(p, axis=0, keepdims=True))
    lb = jnp.zeros((1, nk), F32)
    for i in range(1, layer + 1):
        lb = lb + e[i:i + 1, :]
    lb = lb / jnp.sum(e, axis=0, keepdims=True)
    log_lb = jnp.log(lb)
    log_1mlb = jnp.log1p(-lb)

    t_idx = lax.broadcasted_iota(jnp.int32, (c, nk), 0)
    row_s = lax.broadcasted_iota(jnp.int32, (HG_HEADS * c, nk), 0)
    lane = lax.broadcasted_iota(jnp.int32, (HG_HEADS * c, nk), 1)
    head_rows = (row_s // c) == (lane // HG_DK)
    t_sc = lax.broadcasted_iota(jnp.int32, (c, HG_HEADS * c), 0)
    s_sc = lax.broadcasted_iota(jnp.int32, (c, HG_HEADS * c), 1) % c

    def tile_heads(x):
        xt = jnp.concatenate([x.astype(BF16)] * HG_HEADS, axis=0)
        return jnp.where(head_rows, xt, jnp.zeros_like(xt))

    def chunk(s, i):
        rows = pl.ds(pl.multiple_of(i * c, c), c)
        u = u_ref[s, rows, :]
        q = _silu(u[:, :nk])
        fz = u[:, nk:2 * nk]
        v = u[:, 2 * nk:3 * nk]
        gate = u[:, 3 * nk:]
        lf = jnp.logaddexp(log_lb, log_1mlb + jax.nn.log_sigmoid(fz))
        k = (1.0 - lb) * jax.nn.sigmoid(-fz)
        g = _cumsum_rows(lf)
        g_end = g[c - 1:c, :]
        st = st_ref[s]
        o = _dot_nt((q * jnp.exp(g)).astype(BF16), st.astype(BF16))
        sc = jnp.where(t_sc == s_sc, _dot_nt(q.astype(BF16), tile_heads(k)), 0.0)
        for w in _hgrn_levels(c):
            gr = _boundary_rows(g, w)
            upper = (t_idx // w) % 2 == 1
            ef = jnp.exp(jnp.where(upper, g - gr, gr - g))
            qt = jnp.where(upper, q * ef, 0.0)
            kt = jnp.where(upper, 0.0, k * ef)
            blk = (t_sc // (2 * w)) == (s_sc // (2 * w))
            sc = sc + jnp.where(blk, _dot_nt(qt.astype(BF16), tile_heads(kt)), 0.0)
        o = o + jnp.dot(sc.astype(BF16), tile_heads(v), preferred_element_type=F32)
        y_ref[s, rows, :] = (o * _silu(gate)).astype(y_ref.dtype)
        upd = _dot_tn(v.astype(BF16), (k * jnp.exp(g_end - g)).astype(BF16))
        st_ref[s] = st * jnp.exp(g_end) + jnp.where(state_mask, upd, 0.0)

    for s in range(gb):
        if tb == c:
            chunk(s, 0)
        else:
            lax.fori_loop(0, tb // c, lambda i, _: chunk(s, i), None, unroll=4)

    @pl.when(step == nsteps - 1)
    def _():
        for s in range(gb):
            st = jnp.where(state_mask, st_ref[s], 0.0)
            t = st[:HG_DV]
            for h in range(1, HG_HEADS):
                t = t + st[h * HG_DV:(h + 1) * HG_DV]
            stn_ref[s] = t.T.reshape(HG_HEADS, HG_DK, HG_DV)


def hgrn2_mixer(u, state, init_layer, lb_param, layer, state_out, *, gb=1, rows=2048):
    b, l, _ = u.shape
    c = min(HG_CHUNK, l)
    tb = min(rows, l)
    assert l % tb == 0 and tb % c == 0 and b % gb == 0
    nsteps = l // tb
    nk = HG_HEADS * HG_DK
    st_in, st_out = _state_specs(gb, init_layer, layer, (HG_HEADS, HG_DK, HG_DV))
    return pl.pallas_call(
        functools.partial(_hgrn_body, c=c, nsteps=nsteps, layer=layer),
        out_shape=[jax.ShapeDtypeStruct((b, l, nk), _mixer_out_dtype(l)), jax.ShapeDtypeStruct(state_out.shape, F32)],
        grid=(b // gb, nsteps),
        in_specs=[pl.BlockSpec((gb, tb, PROJ_HG), lambda i, j: (i, j, 0)), *st_in,
                  pl.BlockSpec(lb_param.shape, lambda i, j: (0, 0)), _any_spec()],
        out_specs=[pl.BlockSpec((gb, tb, nk), lambda i, j: (i, j, 0)), *st_out],
        scratch_shapes=[pltpu.VMEM((gb, nk, nk), F32)],
        input_output_aliases={3: 1},
        compiler_params=_cparams("arbitrary", "arbitrary"),
        name="hgrn2_mixer",
    )(u, state, lb_param, state_out)


MLA_BLOCK = LANES
MLA_QK = MLA_HEADS * MLA_BLOCK
MLA_NOPE_OFF = MLA_BLOCK - MLA_NOPE


def _segment_ones():
    seg = lambda i: jnp.where(i < MLA_ROPE, 0, jnp.where(i < MLA_NOPE_OFF, 1, 2))
    i = seg(lax.broadcasted_iota(jnp.int32, (MLA_BLOCK, MLA_BLOCK), 0))
    j = seg(lax.broadcasted_iota(jnp.int32, (MLA_BLOCK, MLA_BLOCK), 1))
    return (i == j).astype(BF16)


def _blocked_rms(x, ones, inv_cnt, gain):
    parts = []
    for h in range(MLA_HEADS):
        xb = x[:, h * MLA_BLOCK:(h + 1) * MLA_BLOCK]
        ss = jnp.dot((xb * xb).astype(BF16), ones, preferred_element_type=F32)
        parts.append(xb * lax.rsqrt(ss * inv_cnt + EPS))
    return jnp.concatenate(parts, axis=1) * gain


def _rope(x, cos, sin_lo, sin_hi):
    n = x.shape[-1]
    half = MLA_ROPE // 2
    return x * cos + pltpu.roll(x, n - half, 1) * sin_lo + pltpu.roll(x, half, 1) * sin_hi


def _paged_body(pt_ref, qn_ref, qr_ref, latn_ref, kr4n_ref, wkt_ref, wv_ref, clat_hbm, ckr_hbm, o_ref,
                lat_buf, kr_buf, latbf_ref, s_ref, sem, *, layer, n_pages, chunk_pages):
    b = pl.program_id(0)
    nb = pl.num_programs(0)
    ng, t_new, _ = qn_ref.shape
    hq = MLA_HEADS * t_new
    nk = MLA_HEADS * MLA_NOPE
    n_tok = chunk_pages * PAGE_SIZE
    issue_unroll = 8

    def page_copies(step, slot, k, p):
        page = pt_ref[step * ng + k, p]
        return (pltpu.make_async_copy(clat_hbm.at[layer, page], lat_buf.at[slot, k, p], sem.at[slot, 0]),
                pltpu.make_async_copy(ckr_hbm.at[layer, page], kr_buf.at[slot, k, p], sem.at[slot, 1]))

    @pl.when(b == 0)
    def _():
        def body(g, _):
            for k in range(ng):
                for i in range(issue_unroll):
                    for cp in page_copies(0, 0, k, g * issue_unroll + i):
                        cp.start()
        lax.fori_loop(0, n_pages // issue_unroll, body, None)

    step_next = lax.rem(b + 1, nb)
    slot_next = lax.rem(b + 1, 2)

    tile_rows = lambda x: jnp.concatenate([x] * MLA_HEADS, axis=0)
    own_n = (lax.broadcasted_iota(jnp.int32, (hq, nk), 0) // t_new
             == lax.broadcasted_iota(jnp.int32, (hq, nk), 1) // MLA_NOPE)
    own_r = (lax.broadcasted_iota(jnp.int32, (hq, LANES), 0) // t_new
             == lax.broadcasted_iota(jnp.int32, (hq, LANES), 1) // MLA_ROPE)
    wkt = wkt_ref[...]

    def query_operands(k):
        qn_sel = jnp.where(own_n, tile_rows(qn_ref[k]), 0.0).astype(BF16)
        q_abs = jnp.dot(qn_sel, wkt, preferred_element_type=F32).astype(BF16)
        qr_sel = jnp.where(own_r, tile_rows(qr_ref[k]), 0.0).astype(BF16)
        return jnp.concatenate([wkt, q_abs], axis=0), qr_sel

    def scores(operands, lat_bf, kr4_bf):
        lhs, qr_sel = operands
        n = lat_bf.shape[0]
        both = _dot_nt(lhs, lat_bf)
        sq = both[:nk] * both[:nk]
        ssq = []
        for h in range(MLA_HEADS):
            part = sq[h * MLA_NOPE:(h + 1) * MLA_NOPE].reshape(MLA_NOPE // SUBLANES, SUBLANES, n).sum(axis=0)
            shift = SUBLANES // 2
            while shift >= 1:
                part = part + pltpu.roll(part, shift, 0)
                shift //= 2
            ssq.append(part)
        ssq = jnp.concatenate(ssq, axis=0)
        s_rope = jnp.dot(qr_sel, kr4_bf, preferred_element_type=F32)
        return (both[nk:] * lax.rsqrt(ssq * (1.0 / MLA_NOPE) + EPS) + s_rope) * MLA_SCALE

    t_q = lax.broadcasted_iota(jnp.int32, (hq, t_new), 0) % t_new
    t_k = lax.broadcasted_iota(jnp.int32, (hq, t_new), 1)
    ops, latn_bf, s_new, m = [], [], [], []
    for k in range(ng):
        ops.append(query_operands(k))
        latn_bf.append(latn_ref[k].astype(BF16))
        s = scores(ops[k], latn_bf[k], kr4n_ref[k].T.astype(BF16))
        s_new.append(jnp.where(t_k <= t_q, s, -jnp.inf))
        m.append(jnp.max(s_new[k], axis=-1, keepdims=True))

    slot = b % 2
    for k in range(ng):
        for p in range(n_pages):
            for cp in page_copies(b, slot, k, p):
                cp.wait()

    for c in range(n_pages // chunk_pages):
        p0 = c * chunk_pages
        for k in range(ng):
            for i in range(chunk_pages):
                for cp in page_copies(step_next, slot_next, k, p0 + i):
                    cp.start()
            lat_bf = lat_buf[slot, k, p0:p0 + chunk_pages].reshape(n_tok, KV_LORA).astype(BF16)
            latbf_ref[k, c * n_tok:(c + 1) * n_tok, :] = lat_bf
            kr_t = jnp.concatenate([kr_buf[slot, k, p0 + i] for i in range(chunk_pages)], axis=1)
            s = scores(ops[k], lat_bf, tile_rows(kr_t.astype(BF16)))
            s_ref[k, :, c * n_tok:(c + 1) * n_tok] = s
            m[k] = jnp.maximum(m[k], jnp.max(s, axis=-1, keepdims=True))

    lane_v = lax.broadcasted_iota(jnp.int32, (t_new, MLA_HEADS * MLA_V), 1) // MLA_V
    for k in range(ng):
        p_old = jnp.exp(s_ref[k] - m[k])
        p_new = jnp.exp(s_new[k] - m[k])
        l = jnp.sum(p_old, axis=-1, keepdims=True) + jnp.sum(p_new, axis=-1, keepdims=True)
        acc = jnp.dot(p_old.astype(BF16), latbf_ref[k], preferred_element_type=F32)
        acc = acc + jnp.dot(p_new.astype(BF16), latn_bf[k], preferred_element_type=F32)
        ov = jnp.dot((acc / l).astype(BF16), wv_ref[...], preferred_element_type=F32)
        out = jnp.zeros((t_new, MLA_HEADS * MLA_V), F32)
        for h in range(MLA_HEADS):
            out = out + jnp.where(lane_v == h, ov[h * t_new:(h + 1) * t_new, :], 0.0)
        o_ref[k] = out

    @pl.when(b == nb - 1)
    def _():
        for k in range(ng):
            for p in range(n_pages):
                for cp in page_copies(step_next, slot_next, k, p):
                    cp.wait()


def mla_sample_attention(qn, qr, lat_all, kr4_new, cache_lat, cache_kr_t, page_table, layer, p, *, chunk_pages=16):
    b, t_new, _ = qn.shape
    n_pages = page_table.shape[1]
    ng = 2 if b % 2 == 0 else 1
    chunk_pages = math.gcd(n_pages, chunk_pages)
    assert n_pages % 8 == 0
    assert t_new == SUBLANES
    n_past = n_pages * PAGE_SIZE
    seq = lambda n: pl.BlockSpec((ng, t_new, n), lambda i, pt: (i, 0, 0))
    lspec = lambda a: pl.BlockSpec((None,) + a.shape[1:], lambda i, pt: (layer, 0, 0))
    return pl.pallas_call(
        functools.partial(_paged_body, layer=layer, n_pages=n_pages, chunk_pages=chunk_pages),
        out_shape=jax.ShapeDtypeStruct((b, t_new, MLA_HEADS * MLA_V), F32),
        grid_spec=pltpu.PrefetchScalarGridSpec(
            num_scalar_prefetch=1,
            grid=(b // ng,),
            in_specs=[seq(MLA_HEADS * MLA_NOPE), seq(LANES),
                      pl.BlockSpec((None, ng, t_new, KV_LORA), lambda i, pt: (layer, i, 0, 0)), seq(LANES),
                      lspec(p['mla_wk_t']), lspec(p['mla_wv']), _any_spec(), _any_spec()],
            out_specs=seq(MLA_HEADS * MLA_V),
            scratch_shapes=[pltpu.VMEM((2, ng, n_pages, PAGE_SIZE, KV_LORA), F32),
                            pltpu.VMEM((2, ng, n_pages, MLA_ROPE, PAGE_SIZE), F32),
                            pltpu.VMEM((ng, n_past, KV_LORA), BF16),
                            pltpu.VMEM((ng, MLA_HEADS * t_new, n_past), F32),
                            pltpu.SemaphoreType.DMA((2, 2))],
        ),
        compiler_params=_cparams("arbitrary"),
        name="mla_sample_attention",
    )(page_table, qn, qr, lat_all, kr4_new, p['mla_wk_t'], p['mla_wv'], cache_lat, cache_kr_t)


def _mla_proj_sample_body(u_ref, cos_ref, slo_ref, shi_ref, qnorm_ref, wq_ref, gn_ref, gr_ref, kvnorm_ref, gkr_ref,
                          _lat_alias, _kr_alias, qn_ref, qr_ref, kr4_ref, lat_ref, kr_ref):
    gb, r, _ = u_ref.shape
    m = gb * r
    nk = MLA_HEADS * MLA_NOPE
    u = u_ref[...]
    tab = lambda t_ref: jnp.broadcast_to(t_ref[...], (gb, r, LANES)).reshape(m, LANES)
    cos, slo, shi = tab(cos_ref), tab(slo_ref), tab(shi_ref)
    i = lax.broadcasted_iota(jnp.int32, (LANES, LANES), 0) // MLA_ROPE
    j = lax.broadcasted_iota(jnp.int32, (LANES, LANES), 1) // MLA_ROPE
    ones_r = (i == j).astype(BF16)

    cq = _rms(u[..., :Q_LORA], qnorm_ref[...]).reshape(m, Q_LORA)
    q = jnp.dot(cq.astype(BF16), wq_ref[...], preferred_element_type=F32)
    qn, qr = q[:, :nk], q[:, nk:]
    ssn = jnp.dot((qn * qn).astype(BF16), _head_ones(nk), preferred_element_type=F32)
    qn_ref[...] = (qn * lax.rsqrt(ssn * (1.0 / MLA_NOPE) + EPS) * gn_ref[...]).reshape(gb, r, nk)
    ssr = jnp.dot((qr * qr).astype(BF16), ones_r, preferred_element_type=F32)
    qr = qr * lax.rsqrt(ssr * (1.0 / MLA_ROPE) + EPS) * gr_ref[...]
    qr_ref[...] = _rope(qr, cos, slo, shi).reshape(gb, r, LANES)

    lat_ref[...] = _rms(u[..., Q_LORA:Q_LORA + KV_LORA], kvnorm_ref[...])
    kr = u[..., Q_LORA + KV_LORA:].reshape(m, LANES)
    kr = kr * lax.rsqrt(jnp.mean(kr * kr, axis=-1, keepdims=True) + EPS) * gkr_ref[...]
    kr = _rope(kr, cos, slo, shi)
    kr4_ref[...] = kr.reshape(gb, r, LANES)
    kr_ref[...] = kr[:, :MLA_ROPE].reshape(gb, r, MLA_ROPE)


def _mla_proj_body(u_ref, cos_ref, slo_ref, shi_ref, qnorm_ref, wq_ref, gq_ref, kvnorm_ref, gkr_ref,
                   wkv_ref, gk_ref, _lat_alias, _kr_alias, q_ref, k_ref, v_ref, lat_ref, kr_ref):
    gb, r, _ = u_ref.shape
    m = gb * r
    u = u_ref[...]
    tab = lambda t_ref: jnp.broadcast_to(t_ref[...], (gb, r, MLA_BLOCK)).reshape(m, MLA_BLOCK)
    cos, slo, shi = tab(cos_ref), tab(slo_ref), tab(shi_ref)
    tile = lambda t: jnp.concatenate([t] * MLA_HEADS, axis=1)
    ones = _segment_ones()
    lane = lax.broadcasted_iota(jnp.int32, (1, MLA_BLOCK), 1)
    inv_cnt = jnp.where(lane < MLA_NOPE_OFF, 1.0 / MLA_ROPE, 1.0 / MLA_NOPE)

    cq = _rms(u[..., :Q_LORA], qnorm_ref[...]).reshape(m, Q_LORA)
    q = jnp.dot(cq.astype(BF16), wq_ref[...], preferred_element_type=F32)
    q = _blocked_rms(q, ones, inv_cnt, gq_ref[...])
    q = _rope(q, tile(cos), tile(slo), tile(shi))
    q_ref[...] = q.reshape(gb, r, MLA_QK).astype(q_ref.dtype)

    lat = _rms(u[..., Q_LORA:Q_LORA + KV_LORA], kvnorm_ref[...])
    lat_ref[...] = lat
    kr = u[..., Q_LORA + KV_LORA:].reshape(m, MLA_BLOCK)
    kr = kr * lax.rsqrt(jnp.sum(kr * kr, axis=-1, keepdims=True) * (1.0 / MLA_ROPE) + EPS) * gkr_ref[...]
    kr = _rope(kr, cos, slo, shi)
    kr_ref[...] = kr[:, :MLA_ROPE].reshape(gb, r, MLA_ROPE)

    kv = jnp.dot(lat.reshape(m, KV_LORA).astype(BF16), wkv_ref[...], preferred_element_type=F32)
    k = _blocked_rms(kv[:, :MLA_QK], ones, inv_cnt, gk_ref[...]) + tile(kr)
    k_ref[...] = k.reshape(gb, r, MLA_QK).astype(k_ref.dtype)
    lane_v = lax.broadcasted_iota(jnp.int32, (1, MLA_QK), 1) % MLA_BLOCK
    v = kv[:, MLA_QK:] + jnp.where(lane_v == MLA_V, 1.0, 0.0)
    v_ref[...] = v.reshape(gb, r, MLA_QK).astype(v_ref.dtype)


def _rope_tables(pos0, l, copies):
    half = MLA_ROPE // 2
    inv_freq = ROPE_THETA ** (-jnp.arange(half, dtype=F32) / half)
    ang = (pos0 + jnp.arange(l)).astype(F32)[:, None] * inv_freq
    cos, sin = jnp.cos(ang), jnp.sin(ang)
    rest = LANES - copies * MLA_ROPE
    lay = lambda t, fill: jnp.concatenate([jnp.tile(t, (1, copies)), jnp.full((l, rest), fill, F32)], axis=1)
    zero = jnp.zeros_like(sin)
    return (lay(jnp.concatenate([cos, cos], 1), 1.0), lay(jnp.concatenate([-sin, zero], 1), 0.0),
            lay(jnp.concatenate([zero, sin], 1), 0.0))


def _head_blocked(rope_part, nope_part):
    ref = rope_part if rope_part is not None else nope_part
    lead = ref.shape[:-2]
    cols = []
    for h in range(MLA_HEADS):
        rp = rope_part[..., h, :] if rope_part is not None else jnp.zeros(lead + (MLA_ROPE,), F32)
        npart = nope_part[..., h, :] if nope_part is not None else jnp.zeros(lead + (MLA_NOPE,), F32)
        cols += [rp, jnp.zeros(lead + (MLA_NOPE_OFF - MLA_ROPE,), F32), npart]
    return jnp.concatenate(cols, axis=-1)


def _mla_project_call(body, name, u, tables, params, layer, outs, stacks, rows):
    b, l, _ = u.shape
    t = _Tiling(b, l, rows)
    uspec = lambda n: pl.BlockSpec((t.gb, t.r, n), lambda i: (*t.tok_index(i), 0))
    tspec = pl.BlockSpec((t.r, LANES), lambda i: (i % t.nl, 0))
    sspec = lambda a: pl.BlockSpec((None, t.gb, t.r, a.shape[-1]), lambda i: (layer, *t.tok_index(i), 0))
    n_in = 4 + len(params)
    return pl.pallas_call(
        body,
        out_shape=[jax.ShapeDtypeStruct((b, l, n), dt) for n, dt in outs]
                  + [jax.ShapeDtypeStruct(a.shape, a.dtype) for a in stacks],
        grid=(t.steps,),
        in_specs=[uspec(PROJ_MLA), tspec, tspec, tspec] + [_lspec(a, layer) for a in params]
                 + [_any_spec() for _ in stacks],
        out_specs=[uspec(n) for n, _ in outs] + [sspec(a) for a in stacks],
        input_output_aliases={n_in + k: len(outs) + k for k in range(len(stacks))},
        compiler_params=_cparams("arbitrary"),
        name=name,
    )(u, *tables, *params, *stacks)


def mla_project_prompt(u, tables, p, layer, lat_out, kr_out, *, rows=512):
    params = [p['mla_q_norm'], p['mla_wq_blocked'], p['mla_gq_blocked'], p['mla_kv_norm'], p['mla_gkr_pad'],
              p['mla_wkv_blocked'], p['mla_gk_blocked']]
    return _mla_project_call(_mla_proj_body, "mla_project_prompt", u, tables, params, layer,
                             [(MLA_QK, BF16)] * 3, [lat_out, kr_out], rows)


def mla_project_sample(u, tables, p, layer, lat_out, kr_out, *, rows=512):
    params = [p['mla_q_norm'], p['mla_wq_compact'], p['mla_gn_compact'], p['mla_gr_compact'], p['mla_kv_norm'],
              p['mla_gkr_tiled']]
    return _mla_project_call(_mla_proj_sample_body, "mla_project_sample", u, tables, params, layer,
                             [(MLA_HEADS * MLA_NOPE, F32), (LANES, F32), (LANES, F32)], [lat_out, kr_out], rows)


def _flash_body(q_ref, k_ref, v_ref, o_ref, *, tq, nq):
    i = pl.program_id(1)
    row = lax.broadcasted_iota(jnp.int32, (tq, tq), 0)
    col = lax.broadcasted_iota(jnp.int32, (tq, tq), 1)
    c = MLA_SCALE * math.log2(math.e)
    blk = lambda h: pl.ds(h * MLA_BLOCK, MLA_BLOCK)
    qs = [q_ref[:, blk(h)] for h in range(MLA_HEADS)]

    def update(carry, rows, mask):
        new = []
        for h in range(MLA_HEADS):
            m, acc = carry[h]
            s = _dot_nt(qs[h], k_ref[rows, blk(h)])
            if mask:
                s = jnp.where(row >= col, s, -jnp.inf)
            m_new = jnp.maximum(m, jnp.max(s, axis=-1, keepdims=True))
            alpha = jnp.exp2((m - m_new) * c)
            p = jnp.exp2((s - m_new) * c)
            acc = alpha * acc + jnp.dot(p.astype(BF16), v_ref[rows, blk(h)], preferred_element_type=F32)
            new.append((m_new, acc))
        return tuple(new)

    def run(n_past):
        carry = tuple((jnp.full((tq, 1), -jnp.inf, F32), jnp.zeros((tq, MLA_BLOCK), F32)) for _ in range(MLA_HEADS))
        for j in range(n_past):
            carry = update(carry, pl.ds(j * tq, tq), False)
        carry = update(carry, pl.ds(n_past * tq, tq), True)
        out = jnp.concatenate([acc[:, :MLA_V] / acc[:, MLA_V:MLA_V + 1] for _, acc in carry], axis=1)
        o_ref[...] = out.astype(o_ref.dtype)

    for n_past in range(nq):
        pl.when(i == n_past)(functools.partial(run, n_past))


def mla_prompt_attention(q, k, v, *, tq=512):
    b, l, _ = q.shape
    tq = min(tq, l)
    assert l % tq == 0
    return pl.pallas_call(
        functools.partial(_flash_body, tq=tq, nq=l // tq),
        out_shape=jax.ShapeDtypeStruct((b, l, MLA_HEADS * MLA_V), _mixer_out_dtype(l)),
        grid=(b, l // tq),
        in_specs=[pl.BlockSpec((None, tq, MLA_QK), lambda bi, i: (bi, i, 0)),
                  pl.BlockSpec((None, l, MLA_QK), lambda bi, i: (bi, 0, 0)),
                  pl.BlockSpec((None, l, MLA_QK), lambda bi, i: (bi, 0, 0))],
        out_specs=pl.BlockSpec((None, tq, MLA_HEADS * MLA_V), lambda bi, i: (bi, i, 0)),
        compiler_params=_cparams("arbitrary", "arbitrary"),
        name="mla_prompt_attention",
    )(q, k, v)


def prepare_params(raw):
    depth = raw['w_in'].shape[0]
    row = lambda a: a.reshape(depth, 1, -1)
    pad_heads = lambda a: jnp.pad(a, ((0, 0), (0, LANES - SSD_HEADS))).reshape(depth, 1, LANES)
    eye = jnp.eye(LRU_BLOCKS, dtype=F32)
    block_diag = lambda w: jnp.einsum('lhij,hg->lhigj', w, eye).reshape(depth, LRU_W, LRU_W)
    ones_h = jnp.ones((1, MLA_HEADS, 1), F32)
    per_head = lambda g: ones_h * g[:, None, :]
    tile4 = lambda g: jnp.tile(g, (1, MLA_HEADS)).reshape(depth, 1, -1)
    wq = raw['mla_w_uq'].reshape(depth, Q_LORA, MLA_HEADS, MLA_NOPE + MLA_ROPE)
    wkv = raw['mla_w_ukv'].reshape(depth, KV_LORA, MLA_HEADS, MLA_NOPE + MLA_V)
    w_k = wkv[..., :MLA_NOPE].reshape(depth, KV_LORA, MLA_HEADS * MLA_NOPE)
    w_v = wkv[..., MLA_NOPE:]
    w_v_blocked = jnp.pad(w_v, ((0, 0), (0, 0), (0, 0), (0, MLA_BLOCK - MLA_V))).reshape(depth, KV_LORA, MLA_QK)
    return dict(
        norm_ffn1=row(raw['norm_ffn1']), norm_mix=row(raw['norm_mix']), norm_ffn2=row(raw['norm_ffn2']),
        out_norm=row(raw['out_norm']),
        ffn1_in=raw['w_ffn1_in'].astype(BF16), ffn1_out=raw['w_ffn1_out'].astype(BF16),
        ffn2_in=raw['w_ffn2_in'].astype(BF16), ffn2_out=raw['w_ffn2_out'].astype(BF16),
        w_in=raw['w_in'], w_out=raw['w_out'].astype(BF16),
        ssd_conv_w=raw['ssd_conv_w'], ssd_conv_b=row(raw['ssd_conv_b']), ssd_dt_bias=pad_heads(raw['ssd_dt_bias']),
        ssd_a_log=pad_heads(raw['ssd_a_log']), ssd_d=row(jnp.repeat(raw['ssd_d'], SSD_P, axis=1)),
        lru_conv_w=raw['lru_conv_w'], lru_conv_b=row(raw['lru_conv_b']),
        lru_w_ri=jnp.concatenate([block_diag(raw['lru_w_r']), block_diag(raw['lru_w_i'])], axis=2).astype(BF16),
        lru_b_ri=row(jnp.concatenate([raw['lru_b_r'], raw['lru_b_i']], axis=1)), lru_lambda=row(raw['lru_lambda']),
        hgrn_lb=raw['hgrn_lb'],
        mla_q_norm=row(raw['mla_q_norm']), mla_kv_norm=row(raw['mla_kv_norm']),
        mla_wq_blocked=_head_blocked(wq[..., MLA_NOPE:], wq[..., :MLA_NOPE]).astype(BF16),
        mla_gq_blocked=row(_head_blocked(per_head(raw['mla_qn_rope']), per_head(raw['mla_qn_nope']))),
        mla_gkr_pad=row(jnp.pad(raw['mla_kn_rope'], ((0, 0), (0, MLA_BLOCK - MLA_ROPE)))),
        mla_wkv_blocked=jnp.concatenate([_head_blocked(None, wkv[..., :MLA_NOPE]), w_v_blocked], axis=2).astype(BF16),
        mla_gk_blocked=row(_head_blocked(None, per_head(raw['mla_kn_nope']))),
        mla_wq_compact=jnp.concatenate([wq[..., :MLA_NOPE].reshape(depth, Q_LORA, -1),
                                        wq[..., MLA_NOPE:].reshape(depth, Q_LORA, -1)], axis=2).astype(BF16),
        mla_gn_compact=tile4(raw['mla_qn_nope'] * raw['mla_kn_nope']), mla_gr_compact=tile4(raw['mla_qn_rope']),
        mla_gkr_tiled=tile4(raw['mla_kn_rope']),
        mla_wk_t=jnp.swapaxes(w_k, 1, 2).astype(BF16),
        mla_wv=w_v.reshape(depth, KV_LORA, MLA_HEADS * MLA_V).astype(BF16),
    )


def _layer(x, mods, layer, row0, p, init_layer, init, outs, tables, paged, seq_block):
    conv_a, ssm_a, conv_b, h_b, s_c = init
    lat_o, kr_o, ssm_o, conva_o, h_o, convb_o, s_o = outs
    x = ffn_halfstep(x, mods, layer, row0, 0, p['norm_ffn1'], p['ffn1_in'], p['ffn1_out'])
    kr_copies = 1 if paged is None else LANES // MLA_ROPE
    u_a, u_b, u_c, u_d = mixer_projection(x, mods, layer, row0, p['norm_mix'], p['w_in'], kr_copies)
    y_a, conva_o, ssm_o = ssd_mixer(u_a, conv_a, ssm_a, init_layer, p, layer, conva_o, ssm_o, gb=seq_block)
    y_b, convb_o, h_o = rglru_mixer(u_b, conv_b, h_b, init_layer, p, layer, convb_o, h_o, gb=seq_block)
    y_c, s_o = hgrn2_mixer(u_c, s_c, init_layer, p['hgrn_lb'], layer, s_o, gb=seq_block)
    if paged is None:
        q, k, v, lat_o, kr_o = mla_project_prompt(u_d, tables, p, layer, lat_o, kr_o)
        y_d = mla_prompt_attention(q, k, v)
    else:
        cache_lat, cache_kr_t, page_table = paged
        qn, qr, kr4, lat_o, kr_o = mla_project_sample(u_d, tables, p, layer, lat_o, kr_o)
        y_d = mla_sample_attention(qn, qr, lat_o, kr4, cache_lat, cache_kr_t, page_table, layer, p)
    x = merge_ffn_halfstep(x, (y_a, y_b, y_c, y_d), mods, layer, row0, p['out_norm'], p['w_out'],
                           p['norm_ffn2'], p['ffn2_in'], p['ffn2_out'])
    return x, (lat_o, kr_o, ssm_o, conva_o, h_o, convb_o, s_o)


def _result_stacks(b, l):
    z = lambda *s: jnp.zeros((DEPTH, b) + s, F32)
    return (z(l, KV_LORA), z(l, MLA_ROPE), z(SSD_HEADS * SSD_P, SSD_N), z(CONV_W - 1, SSD_CONV_DIM),
            z(1, LRU_W), z(CONV_W - 1, LRU_W), z(HG_HEADS, HG_DK, HG_DV))


def kernel(x_prompt, x_sample, cache_mla_latent, cache_mla_krope, state_ssd, state_ssd_conv, state_lru,
           state_lru_conv, state_hgrn, page_table, c_prompt, c_sample, w_ada, b_ada, norm_ffn1, w_ffn1_in,
           w_ffn1_out, norm_mix, w_in, ssd_conv_w, ssd_conv_b, ssd_dt_bias, ssd_a_log, ssd_d, lru_conv_w,
           lru_conv_b, lru_w_r, lru_b_r, lru_w_i, lru_b_i, lru_lambda, hgrn_lb, mla_q_norm, mla_w_uq,
           mla_kv_norm, mla_w_ukv, mla_qn_nope, mla_qn_rope, mla_kn_nope, mla_kn_rope, out_norm, w_out,
           norm_ffn2, w_ffn2_in, w_ffn2_out):
    p = prepare_params(dict(
        norm_ffn1=norm_ffn1, w_ffn1_in=w_ffn1_in, w_ffn1_out=w_ffn1_out, norm_mix=norm_mix, w_in=w_in,
        ssd_conv_w=ssd_conv_w, ssd_conv_b=ssd_conv_b, ssd_dt_bias=ssd_dt_bias, ssd_a_log=ssd_a_log, ssd_d=ssd_d,
        lru_conv_w=lru_conv_w, lru_conv_b=lru_conv_b, lru_w_r=lru_w_r, lru_b_r=lru_b_r, lru_w_i=lru_w_i,
        lru_b_i=lru_b_i, lru_lambda=lru_lambda, hgrn_lb=hgrn_lb, mla_q_norm=mla_q_norm, mla_w_uq=mla_w_uq,
        mla_kv_norm=mla_kv_norm, mla_w_ukv=mla_w_ukv, mla_qn_nope=mla_qn_nope, mla_qn_rope=mla_qn_rope,
        mla_kn_nope=mla_kn_nope, mla_kn_rope=mla_kn_rope, out_norm=out_norm, w_out=w_out, norm_ffn2=norm_ffn2,
        w_ffn2_in=w_ffn2_in, w_ffn2_out=w_ffn2_out))
    bp, lp = x_prompt.shape[:2]
    bs, ls = x_sample.shape[:2]
    dt = x_prompt.dtype
    mods = ada_modulation(jnp.concatenate([c_sample, c_prompt], axis=0), w_ada, b_ada)
    mods = mods.reshape(DEPTH, bs + bp, 1, N_MOD * D_MODEL)
    cache_kr_t = jnp.swapaxes(cache_mla_krope, 2, 3)
    nrow = SSD_HEADS * SSD_P
    zero = lambda *s: jnp.zeros((1, bp) + s, F32)
    init_p = (zero(CONV_W - 1, SSD_CONV_DIM), zero(nrow, SSD_N), zero(CONV_W - 1, LRU_W), zero(1, LRU_W),
              zero(HG_HEADS, HG_DK, HG_DV))
    init_s = (state_ssd_conv, state_ssd.reshape(DEPTH, bs, nrow, SSD_N), state_lru_conv,
              state_lru.reshape(DEPTH, bs, 1, LRU_W), state_hgrn)
    tables_p = _rope_tables(0, lp, 1)
    tables_s = _rope_tables(PAST_LEN, ls, LANES // MLA_ROPE)
    out_p, out_s = _result_stacks(bp, lp), _result_stacks(bs, ls)
    yp, ys = x_prompt, x_sample
    sample_block = math.gcd(bs, SUBLANES)
    for l in range(DEPTH):
        yp, out_p = _layer(yp, mods, l, bs, p, 0, init_p, out_p, tables_p, None, 1)
        ys, out_s = _layer(ys, mods, l, 0, p, l, init_s, out_s, tables_s,
                           (cache_mla_latent, cache_kr_t, page_table), sample_block)

    def finish(outs, b):
        lat, kr, ssm, conva, h, convb, s = outs
        return (lat, kr, ssm.reshape(DEPTH, b, SSD_HEADS, SSD_P, SSD_N), conva, h.reshape(DEPTH, b, LRU_W), convb, s)

    lat_p, kr_p, ssd_p, ssdc_p, lru_p, lruc_p, hg_p = [a.astype(dt) for a in finish(out_p, bp)]
    lat_s, kr_s, ssd_s, ssdc_s, lru_s, lruc_s, hg_s = [a.astype(dt) for a in finish(out_s, bs)]
    return (yp, ys, lat_p, lat_s, kr_p, kr_s, ssd_p, ssd_s, ssdc_p, ssdc_s, lru_p, lru_s, lruc_p, lruc_s, hg_p, hg_s)
```
